```python
import math
import jax, jax.numpy as jnp
from jax import lax
import numpy as np

D_MODEL = 2048
BATCH = 4
SEQ = 2048
DEPTH = 1
DEC_BATCH = 128
DEC_SEQ = 8
PAST_LEN = 2048
PAGE_SIZE = 128

N_ATT_HEADS = 8
ATT_HEAD_DIM = 128
ATT_WIDTH = N_ATT_HEADS * ATT_HEAD_DIM
N_IDX_HEADS = 16
IDX_DIM = 64
TOPK_MAX = 256
N_SSM_HEADS = 16
SSM_HEAD_DIM = 64
SSM_WIDTH = N_SSM_HEADS * SSM_HEAD_DIM
N_SSM_GROUPS = 2
SSM_STATE = 128
CONV_W = 4
CONV_DIM = SSM_WIDTH + 2 * N_SSM_GROUPS * SSM_STATE
SSD_CHUNK = 128
MIX_WIDTH = ATT_WIDTH + SSM_WIDTH
IN_WIDTH = 3 * ATT_WIDTH + N_IDX_HEADS * IDX_DIM + IDX_DIM + N_IDX_HEADS + SSM_WIDTH + CONV_DIM + N_SSM_HEADS
FFN_HIDDEN = -(-8 * D_MODEL // (3 * 256)) * 256
N_BUCKETS = 32
MAX_DISTANCE = 128
NORM_EPS = 1e-6
PROMPT_QBLOCK = 64
SAMPLE_QBLOCK = 1

kernel_name = "hymba_dsa_ssd_decoder_step"


def _split_points():
    sizes = [ATT_WIDTH, ATT_WIDTH, ATT_WIDTH, N_IDX_HEADS * IDX_DIM, IDX_DIM, N_IDX_HEADS,
             SSM_WIDTH, CONV_DIM, N_SSM_HEADS]
    return [int(s) for s in np.cumsum(sizes)[:-1]]


def rmsnorm(x, g):
    xf = x.astype(jnp.float32)
    y = xf * lax.rsqrt(jnp.mean(xf * xf, axis=-1, keepdims=True) + NORM_EPS)
    return (y * g.astype(jnp.float32)).astype(x.dtype)


def t5_bucket(rel):
    max_exact = N_BUCKETS // 2
    relf = jnp.maximum(rel, 1).astype(jnp.float32)
    large = max_exact + (jnp.log(relf / max_exact) / math.log(MAX_DISTANCE / max_exact)
                         * (N_BUCKETS - max_exact)).astype(jnp.int32)
    large = jnp.minimum(large, N_BUCKETS - 1)
    return jnp.where(rel < max_exact, rel, large)


def indexer_topk(qi, w, kidx, qpos, topk):
    s = jax.nn.relu(jnp.einsum('bqhd,bld->bqhl', qi, kidx))
    score = jnp.einsum('bqhl,bqh->bql', s, w).astype(jnp.float32)
    allowed = jnp.arange(kidx.shape[1], dtype=jnp.int32)[None, :] <= qpos[:, None]
    score = jnp.where(allowed[None], score, -jnp.inf)
    _, idx = lax.top_k(score, topk)
    return idx


def sparse_attend(q, k_sel, v_sel, idx, qpos, rel_bias):
    logits = jnp.einsum('bqhd,bqkhd->bqhk', q, k_sel).astype(jnp.float32) * (ATT_HEAD_DIM ** -0.5)
    rel = qpos[None, :, None] - idx
    bias = rel_bias[t5_bucket(jnp.maximum(rel, 0))]
    logits = logits + jnp.moveaxis(bias, -1, 2).astype(jnp.float32)
    logits = jnp.where((rel >= 0)[:, :, None, :], logits, -jnp.inf)
    p = jax.nn.softmax(logits, axis=-1)
    return jnp.einsum('bqhk,bqkhd->bqhd', p.astype(v_sel.dtype), v_sel)


def _gather_rows(a, i):
    return jax.vmap(lambda ab, ib: ab[ib])(a, i)


def dsa_prompt(q, k, v, qi, w, kidx, rel_bias):
    B, S = q.shape[:2]
    topk = min(TOPK_MAX, S // 4)
    nb = S // PROMPT_QBLOCK

    def blk(xs):
        qb, qib, wb, qpos = xs
        idx = indexer_topk(qib, wb, kidx, qpos, topk)
        return sparse_attend(qb, _gather_rows(k, idx), _gather_rows(v, idx), idx, qpos, rel_bias)

    to_blocks = lambda a: jnp.moveaxis(a.reshape(B, nb, PROMPT_QBLOCK, *a.shape[2:]), 1, 0)
    qpos = jnp.arange(S, dtype=jnp.int32).reshape(nb, PROMPT_QBLOCK)
    out = lax.map(blk, (to_blocks(q), to_blocks(qi), to_blocks(w), qpos))
    return jnp.moveaxis(out, 0, 1).reshape(B, S, ATT_WIDTH)


def dsa_sample(q, k_new, v_new, qi, w, kidx_new, cache_k, cache_v, cache_kidx, page_table, rel_bias):
    B, T = q.shape[:2]
    n_pages = page_table.shape[1]
    past = n_pages * PAGE_SIZE
    kidx_past = cache_kidx[page_table].reshape(B, past, IDX_DIM)
    kidx_all = jnp.concatenate([kidx_past, kidx_new], axis=1)
    topk = min(TOPK_MAX, (past + T) // 4)

    def gather(cache, new, idx):
        in_past = idx < past
        pidx = jnp.minimum(idx, past - 1)
        phys = jnp.take_along_axis(page_table, (pidx // PAGE_SIZE).reshape(B, -1), axis=1).reshape(idx.shape)
        rows_past = cache[phys, pidx % PAGE_SIZE]
        rows_new = _gather_rows(new, jnp.clip(idx - past, 0, T - 1))
        return jnp.where(in_past[..., None, None], rows_past, rows_new)

    nb = T // SAMPLE_QBLOCK

    def blk(xs):
        qb, qib, wb, qpos = xs
        idx = indexer_topk(qib, wb, kidx_all, qpos, topk)
        return sparse_attend(qb, gather(cache_k, k_new, idx), gather(cache_v, v_new, idx), idx, qpos, rel_bias)

    to_blocks = lambda a: jnp.moveaxis(a.reshape(B, nb, SAMPLE_QBLOCK, *a.shape[2:]), 1, 0)
    qpos = (past + jnp.arange(T, dtype=jnp.int32)).reshape(nb, SAMPLE_QBLOCK)
    out = lax.map(blk, (to_blocks(q), to_blocks(qi), to_blocks(w), qpos))
    return jnp.moveaxis(out, 0, 1).reshape(B, T, ATT_WIDTH)


def causal_conv(xbc, conv_w, conv_b, prefix):
    T = xbc.shape[1]
    xp = jnp.concatenate([prefix.astype(xbc.dtype), xbc], axis=1)
    out = conv_b + sum(xp[:, j:j + T] * conv_w[j] for j in range(CONV_W))
    return jax.nn.silu(out), xp[:, -(CONV_W - 1):]


def ssd_scan(x, dt, A, Bm, Cm, h0, chunk):
    Bsz, T = x.shape[:2]
    nc = T // chunk
    rep = N_SSM_HEADS // N_SSM_GROUPS
    f32 = jnp.float32
    Bh = jnp.repeat(Bm.astype(f32), rep, axis=2)
    Ch = jnp.repeat(Cm.astype(f32), rep, axis=2)
    a = dt * A
    to_chunks = lambda t: jnp.moveaxis(t.reshape(Bsz, nc, chunk, *t.shape[2:]), 1, 0)
    causal = jnp.tril(jnp.ones((chunk, chunk), dtype=bool))

    def step(h, xs):
        xc, dtc, ac, Bc, Cc = xs
        acum = jnp.cumsum(ac, axis=1)
        seg = acum[:, :, None, :] - acum[:, None, :, :]
        decay = jnp.exp(jnp.where(causal[None, :, :, None], seg, -jnp.inf))
        cb = jnp.einsum('bihn,bjhn->bijh', Cc, Bc)
        y_intra = jnp.einsum('bijh,bjh,bjhp->bihp', cb * decay, dtc, xc)
        y_inter = jnp.einsum('bihn,bhpn->bihp', Cc, h) * jnp.exp(acum)[..., None]
        last = acum[:, -1]
        wts = jnp.exp(last[:, None, :] - acum) * dtc
        h_new = h * jnp.exp(last)[..., None, None] + jnp.einsum('bjh,bjhn,bjhp->bhpn', wts, Bc, xc)
        return h_new, y_intra + y_inter

    h_fin, ys = lax.scan(step, h0.astype(f32),
                         (to_chunks(x.astype(f32)), to_chunks(dt), to_chunks(a), to_chunks(Bh), to_chunks(Ch)))
    y = jnp.moveaxis(ys, 0, 1).reshape(Bsz, T, N_SSM_HEADS, SSM_HEAD_DIM)
    return y, h_fin


def layer(x, att_fn, conv_prefix, h0, chunk, norm_attn, w_in, idx_k_norm, conv_w, conv_b,
          dt_bias, a_log, d_skip, ssm_norm, w_out, norm_ffn, w_gate, w_up, w_down):
    B, T, _ = x.shape
    u = rmsnorm(x, norm_attn)
    proj = u @ w_in
    q, k, v, qi, ki, wi, z, xbc, dtr = jnp.split(proj, _split_points(), axis=-1)
    q = q.reshape(B, T, N_ATT_HEADS, ATT_HEAD_DIM)
    k = k.reshape(B, T, N_ATT_HEADS, ATT_HEAD_DIM)
    v = v.reshape(B, T, N_ATT_HEADS, ATT_HEAD_DIM)
    qi = qi.reshape(B, T, N_IDX_HEADS, IDX_DIM)
    ki = rmsnorm(ki, idx_k_norm)
    att = att_fn(q, k, v, qi, wi, ki)

    xc, conv_tail = causal_conv(xbc, conv_w, conv_b, conv_prefix)
    xs = xc[..., :SSM_WIDTH].reshape(B, T, N_SSM_HEADS, SSM_HEAD_DIM)
    Bm = xc[..., SSM_WIDTH:SSM_WIDTH + N_SSM_GROUPS * SSM_STATE].reshape(B, T, N_SSM_GROUPS, SSM_STATE)
    Cm = xc[..., SSM_WIDTH + N_SSM_GROUPS * SSM_STATE:].reshape(B, T, N_SSM_GROUPS, SSM_STATE)
    dt = jax.nn.softplus(dtr.astype(jnp.float32) + dt_bias.astype(jnp.float32))
    A = -jnp.exp(a_log.astype(jnp.float32))
    y, h_fin = ssd_scan(xs, dt, A, Bm, Cm, h0, chunk)
    y = (y + d_skip.astype(jnp.float32)[:, None] * xs.astype(jnp.float32)).astype(x.dtype)
    y = rmsnorm(y.reshape(B, T, SSM_WIDTH) * jax.nn.silu(z), ssm_norm)

    h = x + jnp.concatenate([att, y], axis=-1) @ w_out
    f = rmsnorm(h, norm_ffn)
    h = h + (jax.nn.silu(f @ w_gate) * (f @ w_up)) @ w_down
    return h, k, v, ki, conv_tail, h_fin.astype(h0.dtype)


def setup_inputs(seed: int = 0) -> dict:
    key = jax.random.key(seed)
    ks = jax.random.split(key, 24)
    f32 = jnp.float32
    n_pages = PAST_LEN // PAGE_SIZE
    used = DEC_BATCH * n_pages
    n_phys = used + max(1, used // 4)
    nrm = lambda k, shape, s: s * jax.random.normal(k, shape, f32)
    dt0 = jnp.exp(jax.random.uniform(ks[14], (DEPTH, N_SSM_HEADS), f32, math.log(1e-3), math.log(1e-1)))
    return {
        "x_prompt": nrm(ks[0], (BATCH, SEQ, D_MODEL), 1.0),
        "x_sample": nrm(ks[1], (DEC_BATCH, DEC_SEQ, D_MODEL), 1.0),
        "cache_k": nrm(ks[2], (DEPTH, n_phys, PAGE_SIZE, N_ATT_HEADS, ATT_HEAD_DIM), 1.0),
        "cache_v": nrm(ks[3], (DEPTH, n_phys, PAGE_SIZE, N_ATT_HEADS, ATT_HEAD_DIM), 1.0),
        "cache_kidx": nrm(ks[4], (DEPTH, n_phys, PAGE_SIZE, IDX_DIM), 1.0),
        "state_conv": nrm(ks[5], (DEPTH, DEC_BATCH, CONV_W - 1, CONV_DIM), 1.0),
        "state_ssm": nrm(ks[6], (DEPTH, DEC_BATCH, N_SSM_HEADS, SSM_HEAD_DIM, SSM_STATE), 0.1),
        "page_table": jax.random.permutation(ks[7], n_phys)[:used].reshape(DEC_BATCH, n_pages).astype(jnp.int32),
        "rel_bias": nrm(ks[8], (N_BUCKETS, N_ATT_HEADS), 0.5),
        "norm_attn": 1.0 + nrm(ks[9], (DEPTH, D_MODEL), 0.02),
        "w_in": nrm(ks[10], (DEPTH, D_MODEL, IN_WIDTH), D_MODEL ** -0.5),
        "idx_k_norm": 1.0 + nrm(ks[11], (DEPTH, IDX_DIM), 0.02),
        "conv_w": nrm(ks[12], (DEPTH, CONV_W, CONV_DIM), CONV_W ** -0.5),
        "conv_b": nrm(ks[13], (DEPTH, CONV_DIM), 0.02),
        "dt_bias": dt0 + jnp.log(-jnp.expm1(-dt0)),
        "a_log": jnp.log(jax.random.uniform(ks[15], (DEPTH, N_SSM_HEADS), f32, 1.0, 16.0)),
        "d_skip": 1.0 + nrm(ks[16], (DEPTH, N_SSM_HEADS), 0.1),
        "ssm_norm": 1.0 + nrm(ks[17], (DEPTH, SSM_WIDTH), 0.02),
        "w_out": nrm(ks[18], (DEPTH, MIX_WIDTH, D_MODEL), MIX_WIDTH ** -0.5),
        "norm_ffn": 1.0 + nrm(ks[19], (DEPTH, D_MODEL), 0.02),
        "w_gate": nrm(ks[20], (DEPTH, D_MODEL, FFN_HIDDEN), D_MODEL ** -0.5),
        "w_up": nrm(ks[21], (DEPTH, D_MODEL, FFN_HIDDEN), D_MODEL ** -0.5),
        "w_down": nrm(ks[22], (DEPTH, FFN_HIDDEN, D_MODEL), FFN_HIDDEN ** -0.5),
        "norm_final": 1.0 + nrm(ks[23], (D_MODEL,), 0.02),
    }


def reference(x_prompt, x_sample, cache_k, cache_v, cache_kidx, state_conv, state_ssm, page_table,
              rel_bias, norm_attn, w_in, idx_k_norm, conv_w, conv_b, dt_bias, a_log, d_skip, ssm_norm,
              w_out, norm_ffn, w_gate, w_up, w_down, norm_final):
    hp, hs = x_prompt, x_sample
    kp, vp, kip, cp, sp = [], [], [], [], []
    ksm, vsm, kism, csm, ssm_s = [], [], [], [], []
    zero_conv = jnp.zeros((x_prompt.shape[0], CONV_W - 1, CONV_DIM), x_prompt.dtype)
    zero_ssm = jnp.zeros((x_prompt.shape[0], N_SSM_HEADS, SSM_HEAD_DIM, SSM_STATE), state_ssm.dtype)
    for l in range(DEPTH):
        wl = (norm_attn[l], w_in[l], idx_k_norm[l], conv_w[l], conv_b[l], dt_bias[l], a_log[l],
              d_skip[l], ssm_norm[l], w_out[l], norm_ffn[l], w_gate[l], w_up[l], w_down[l])
        att_p = lambda q, k, v, qi, wi, ki: dsa_prompt(q, k, v, qi, wi, ki, rel_bias)
        hp, k1, v1, ki1, c1, s1 = layer(hp, att_p, zero_conv, zero_ssm, SSD_CHUNK, *wl)
        ck, cv, cki = cache_k[l], cache_v[l], cache_kidx[l]
        att_s = lambda q, k, v, qi, wi, ki: dsa_sample(q, k, v, qi, wi, ki, ck, cv, cki, page_table, rel_bias)
        hs, k2, v2, ki2, c2, s2 = layer(hs, att_s, state_conv[l], state_ssm[l], hs.shape[1], *wl)
        kp.append(k1); vp.append(v1); kip.append(ki1); cp.append(c1); sp.append(s1)
        ksm.append(k2); vsm.append(v2); kism.append(ki2); csm.append(c2); ssm_s.append(s2)
    y_prompt = rmsnorm(hp, norm_final)
    y_sample = rmsnorm(hs, norm_final)
    return (y_prompt, y_sample,
            jnp.stack(kp), jnp.stack(vp), jnp.stack(kip), jnp.stack(cp), jnp.stack(sp),
            jnp.stack(ksm), jnp.stack(vsm), jnp.stack(kism), jnp.stack(csm), jnp.stack(ssm_s))
```

```python
import functools
import math

import numpy as np
import jax
import jax.numpy as jnp
from jax import lax
from jax.experimental import pallas as pl
from jax.experimental.pallas import tpu as pltpu

N_ATT_HEADS = 8
ATT_HEAD_DIM = 128
ATT_WIDTH = N_ATT_HEADS * ATT_HEAD_DIM
N_IDX_HEADS = 16
IDX_DIM = 64
TOPK_MAX = 256
N_SSM_HEADS = 16
SSM_HEAD_DIM = 64
SSM_WIDTH = N_SSM_HEADS * SSM_HEAD_DIM
N_SSM_GROUPS = 2
SSM_STATE = 128
CONV_W = 4
CONV_DIM = SSM_WIDTH + 2 * N_SSM_GROUPS * SSM_STATE
SSD_CHUNK = 128
PAGE_SIZE = 128
N_BUCKETS = 32
MAX_DISTANCE = 128
NORM_EPS = 1e-6

LANES = 128
SUBLANES = 8
VMEM_LIMIT = 56 * 1024 * 1024

COL_Q, COL_K, COL_V, COL_QI, COL_Z, COL_X = (i * ATT_WIDTH for i in range(6))
COL_B = COL_X + SSM_WIDTH
COL_C = COL_B + N_SSM_GROUPS * SSM_STATE
BIG_WIDTH = COL_C + N_SSM_GROUPS * SSM_STATE
SM_KI, SM_WI, SM_DT = 0, IDX_DIM, IDX_DIM + N_IDX_HEADS
SMALL_WIDTH = LANES

F32 = jnp.float32
BF16 = jnp.bfloat16

_INT_MIN = -(2 ** 31)
_INT_MAX = 2 ** 31 - 1
_KEY_NEG_INF = int(np.array([-np.inf], np.float32).view(np.int32)[0]) ^ 0x7FFFFFFF
_NEG_BIG = -1e30


def _dot(a, b):
    return jnp.dot(a, b, preferred_element_type=F32)


def _dot_nt(a, b):
    return lax.dot_general(a, b, (((1,), (1,)), ((), ())), preferred_element_type=F32)


def _dot_tn(a, b):
    return lax.dot_general(a, b, (((0,), (0,)), ((), ())), preferred_element_type=F32)


def _rms(x, g):
    return x * lax.rsqrt(jnp.mean(x * x, axis=-1, keepdims=True) + NORM_EPS) * g


def _silu(x):
    return x * (1.0 / (1.0 + jnp.exp(-x)))


def _inproj_body(x_ref, g_ref, wb_ref, ws_ref, gk_ref, big_ref, small_ref, kin_ref, u_scr):
    @pl.when(pl.program_id(1) == 0)
    def _():
        ub = _rms(x_ref[...], g_ref[...]).astype(BF16)
        u_scr[...] = ub
        sm = _dot(ub, ws_ref[...])
        small_ref[...] = sm
        kin_ref[...] = _rms(sm[:, SM_KI:SM_KI + IDX_DIM], gk_ref[...])

    big_ref[...] = _dot(u_scr[...], wb_ref[...])


def _inproj(x2, g, wb, ws, gk, *, tm, tn):
    m, d = x2.shape
    return pl.pallas_call(
        _inproj_body,
        grid=(m // tm, BIG_WIDTH // tn),
        in_specs=[
            pl.BlockSpec((tm, d), lambda i, j: (i, 0)),
            pl.BlockSpec((1, d), lambda i, j: (0, 0)),
            pl.BlockSpec((d, tn), lambda i, j: (0, j)),
            pl.BlockSpec((d, SMALL_WIDTH), lambda i, j: (0, 0)),
            pl.BlockSpec((1, IDX_DIM), lambda i, j: (0, 0)),
        ],
        out_specs=[
            pl.BlockSpec((tm, tn), lambda i, j: (i, j)),
            pl.BlockSpec((tm, SMALL_WIDTH), lambda i, j: (i, 0)),
            pl.BlockSpec((tm, IDX_DIM), lambda i, j: (i, 0)),
        ],
        out_shape=[
            jax.ShapeDtypeStruct((m, BIG_WIDTH), F32),
            jax.ShapeDtypeStruct((m, SMALL_WIDTH), F32),
            jax.ShapeDtypeStruct((m, IDX_DIM), F32),
        ],
        scratch_shapes=[pltpu.VMEM((tm, d), BF16)],
        compiler_params=pltpu.CompilerParams(
            dimension_semantics=("parallel", "arbitrary"), vmem_limit_bytes=VMEM_LIMIT),
        name="inproj",
    )(x2, g, wb, ws, gk)


def _sortable_key(s):
    s = jnp.where(s == 0.0, 0.0, s)
    b = pltpu.bitcast(s, jnp.int32)
    return b ^ ((b >> 31) & 0x7FFFFFFF)


def _count_rows(key_ref, nblk, pred):
    rows = key_ref.shape[1]

    def body(kb, acc):
        return acc + jnp.where(pred(key_ref[kb], kb), 1.0, 0.0)

    acc = lax.fori_loop(0, nblk, body, jnp.zeros((rows, LANES), F32))
    return jnp.sum(acc, axis=-1, keepdims=True)


def _topk_select_params(key_ref, nblk, topk):
    rows = key_ref.shape[1]
    kf = float(topk)

    def count_ge(cand):
        cb = jnp.broadcast_to(cand, (rows, LANES))
        return _count_rows(key_ref, nblk, lambda k, kb: k >= cb)

    res = jnp.where(count_ge(jnp.zeros((rows, 1), jnp.int32)) >= kf, 0, _INT_MIN).astype(jnp.int32)

    def bit_body(it, res):
        cand = res | jnp.left_shift(jnp.int32(1), 30 - it)
        return jnp.where(count_ge(cand) >= kf, cand, res)

    thr = lax.fori_loop(0, 31, bit_body, res)
    thr_b = jnp.broadcast_to(thr, (rows, LANES))
    n_gt = _count_rows(key_ref, nblk, lambda k, kb: k > thr_b)
    n_ge = count_ge(thr)
    tied = jnp.logical_and(n_ge > kf, thr > _KEY_NEG_INF)
    any_tied = jnp.max(jnp.where(tied, 1.0, 0.0)) > 0.0
    need = kf - n_gt
    lane = lax.broadcasted_iota(jnp.int32, (rows, LANES), 1)

    def tie_break():
        def body(it, cut):
            cand = cut | jnp.left_shift(jnp.int32(1), 30 - it)
            cand_b = jnp.broadcast_to(cand, (rows, LANES))
            n = _count_rows(key_ref, nblk,
                            lambda k, kb: jnp.logical_and(k == thr_b, lane + kb * LANES < cand_b))
            return jnp.where(n <= need, cand, cut)

        return lax.fori_loop(0, 31, body, jnp.zeros((rows, 1), jnp.int32))

    cut = lax.cond(any_tied, tie_break, lambda: jnp.full((rows, 1), _INT_MAX, jnp.int32))
    return thr, cut


def _attn_prompt_body(q_ref, qi_ref, sm_ref, kin_ref, k_ref, v_ref, bias_ref, o_ref,
                      key_scr, wb_scr, m_scr, l_scr, acc_scr, *, tq, topk):
    i = pl.program_id(1)
    nkb = (i + 1) * (tq // LANES)
    q0 = i * tq
    scale = ATT_HEAD_DIM ** -0.5

    wi = sm_ref[0, :, SM_WI:SM_WI + N_IDX_HEADS]
    for h in range(N_IDX_HEADS):
        wb_scr[h] = jnp.broadcast_to(wi[:, h:h + 1], (tq, LANES))

    qi = qi_ref[0].reshape(N_IDX_HEADS * tq, IDX_DIM).astype(BF16)
    row = lax.broadcasted_iota(jnp.int32, (tq, LANES), 0) + q0
    lane = lax.broadcasted_iota(jnp.int32, (tq, LANES), 1)

    def score_body(kb, carry):
        kin = kin_ref[0, pl.ds(pl.multiple_of(kb * LANES, LANES), LANES), :].astype(BF16)
        s = _dot_nt(qi, kin).reshape(N_IDX_HEADS, tq, LANES)
        score = jnp.sum(jnp.maximum(s, 0.0) * wb_scr[...], axis=0)
        score = jnp.where(lane + kb * LANES <= row, score, -jnp.inf)
        key_scr[kb] = _sortable_key(score)
        return carry

    lax.fori_loop(0, nkb, score_body, 0)

    thr, cut = _topk_select_params(key_scr, nkb, topk)
    thr_b = jnp.broadcast_to(thr, (tq, LANES))
    cut_b = jnp.broadcast_to(cut, (tq, LANES))

    m_scr[...] = jnp.full(m_scr.shape, _NEG_BIG, F32)
    l_scr[...] = jnp.zeros(l_scr.shape, F32)
    acc_scr[...] = jnp.zeros(acc_scr.shape, F32)
    qb = q_ref[0].astype(BF16)

    def attn_body(kb, carry):
        col = lane + kb * LANES
        key = key_scr[kb]
        sel = jnp.logical_or(key > thr_b, jnp.logical_and(key == thr_b, col < cut_b))
        valid = jnp.logical_and(sel, col <= row)
        start = pl.multiple_of(kb * LANES, LANES)
        kblk = k_ref[0, pl.ds(start, LANES), :]
        vblk = v_ref[0, pl.ds(start, LANES), :]
        tix = 2 - jnp.minimum(i - kb, 2)
        for h in range(N_ATT_HEADS):
            hs = slice(h * ATT_HEAD_DIM, (h + 1) * ATT_HEAD_DIM)
            lg = _dot_nt(qb[:, hs], kblk[:, hs]) * scale + bias_ref[tix, h]
            lg = jnp.where(valid, lg, _NEG_BIG)
            m_old = m_scr[h]
            m_new = jnp.maximum(m_old, jnp.max(lg, axis=-1, keepdims=True))
            alpha = jnp.exp(m_old - m_new)
            p = jnp.where(valid, jnp.exp(lg - m_new), 0.0)
            l_scr[h] = alpha * l_scr[h] + jnp.sum(p, axis=-1, keepdims=True)
            acc_scr[h] = alpha * acc_scr[h] + _dot(p.astype(BF16), vblk[:, hs])
            m_scr[h] = m_new
        return carry

    lax.fori_loop(0, nkb, attn_body, 0)
    for h in range(N_ATT_HEADS):
        o_ref[0, :, h * ATT_HEAD_DIM:(h + 1) * ATT_HEAD_DIM] = (acc_scr[h] / l_scr[h]).astype(o_ref.dtype)


def _attn_prompt(big3, qi_t, small3, kin3, k16, v16, bias_tiles, *, tq, topk):
    b, s, _ = big3.shape
    assert tq == LANES
    body = functools.partial(_attn_prompt_body, tq=tq, topk=topk)
    return pl.pallas_call(
        body,
        grid=(b, s // tq),
        in_specs=[
            pl.BlockSpec((1, tq, ATT_WIDTH), lambda bi, i: (bi, i, COL_Q // ATT_WIDTH)),
            pl.BlockSpec((1, N_IDX_HEADS, tq, IDX_DIM), lambda bi, i: (bi, 0, i, 0)),
            pl.BlockSpec((1, tq, SMALL_WIDTH), lambda bi, i: (bi, i, 0)),
            pl.BlockSpec((1, s, IDX_DIM), lambda bi, i: (bi, 0, 0)),
            pl.BlockSpec((1, s, ATT_WIDTH), lambda bi, i: (bi, 0, 0)),
            pl.BlockSpec((1, s, ATT_WIDTH), lambda bi, i: (bi, 0, 0)),
            pl.BlockSpec(bias_tiles.shape, lambda bi, i: (0, 0, 0, 0)),
        ],
        out_specs=pl.BlockSpec((1, tq, ATT_WIDTH), lambda bi, i: (bi, i, 0)),
        out_shape=jax.ShapeDtypeStruct((b, s, ATT_WIDTH), BF16),
        scratch_shapes=[
            pltpu.VMEM((s // LANES, tq, LANES), jnp.int32),
            pltpu.VMEM((N_IDX_HEADS, tq, LANES), F32),
            pltpu.VMEM((N_ATT_HEADS, tq, 1), F32),
            pltpu.VMEM((N_ATT_HEADS, tq, 1), F32),
            pltpu.VMEM((N_ATT_HEADS, tq, ATT_HEAD_DIM), F32),
        ],
        compiler_params=pltpu.CompilerParams(
            dimension_semantics=("parallel", "arbitrary"), vmem_limit_bytes=VMEM_LIMIT),
        name="attn_prompt",
    )(big3, qi_t, small3, kin3, k16, v16, bias_tiles)


def _attn_sample_body(pt_ref, q_ref, knew_ref, vnew_ref, qi_ref, sm_ref, kinew_ref, bias_ref, *rest,
                      n_pages, t, topk):
    kidx_refs = rest[:n_pages]
    k_refs = rest[n_pages:2 * n_pages]
    v_refs = rest[2 * n_pages:3 * n_pages]
    o_ref = rest[3 * n_pages]
    key_scr, wb_scr, lg_scr, acc_scr = rest[3 * n_pages + 1:]
    nblk = n_pages + 1
    rows = N_ATT_HEADS * t
    scale = ATT_HEAD_DIM ** -0.5

    wi = sm_ref[0, :, SM_WI:SM_WI + N_IDX_HEADS]
    for h in range(N_IDX_HEADS):
        wb_scr[h] = jnp.broadcast_to(wi[:, h:h + 1], (t, LANES))
    qi = qi_ref[0].reshape(N_IDX_HEADS * t, IDX_DIM).astype(BF16)
    trow = lax.broadcasted_iota(jnp.int32, (t, LANES), 0)
    lane = lax.broadcasted_iota(jnp.int32, (t, LANES), 1)
    new_ok = lane <= trow

    def scores(kin):
        s = _dot_nt(qi, kin.astype(BF16)).reshape(N_IDX_HEADS, t, LANES)
        return jnp.sum(jnp.maximum(s, 0.0) * wb_scr[...], axis=0)

    for p in range(n_pages):
        key_scr[p] = _sortable_key(scores(kidx_refs[p][0]))
    kin_new = jnp.concatenate([kinew_ref[0], jnp.zeros((LANES - t, IDX_DIM), F32)], axis=0)
    key_scr[n_pages] = _sortable_key(jnp.where(new_ok, scores(kin_new), -jnp.inf))

    thr, cut = _topk_select_params(key_scr, nblk, topk)
    thr_b = jnp.broadcast_to(thr, (t, LANES))
    cut_b = jnp.broadcast_to(cut, (t, LANES))

    q = q_ref[0]
    colhead = lax.broadcasted_iota(jnp.int32, (t, ATT_WIDTH), 1) // ATT_HEAD_DIM
    qx = jnp.concatenate([jnp.where(colhead == h, q, 0.0) for h in range(N_ATT_HEADS)], axis=0).astype(BF16)

    def block_kv(p):
        if p < n_pages:
            return k_refs[p][0], v_refs[p][0]
        pad = jnp.zeros((LANES - t, ATT_WIDTH), F32)
        return (jnp.concatenate([knew_ref[0], pad], axis=0), jnp.concatenate([vnew_ref[0], pad], axis=0))

    m = jnp.full((rows, 1), _NEG_BIG, F32)
    for p in range(nblk):
        key = key_scr[p]
        sel = jnp.logical_or(key > thr_b, jnp.logical_and(key == thr_b, lane + p * LANES < cut_b))
        if p == n_pages:
            sel = jnp.logical_and(sel, new_ok)
        valid = jnp.concatenate([sel] * N_ATT_HEADS, axis=0)
        tix = 2 - min(n_pages - p, 2)
        kblk, _ = block_kv(p)
        lg = _dot_nt(qx, kblk.astype(BF16)) * scale + bias_ref[tix]
        lg = jnp.where(valid, lg, _NEG_BIG)
        lg_scr[p] = lg
        m = jnp.maximum(m, jnp.max(lg, axis=-1, keepdims=True))

    l = jnp.zeros((rows, 1), F32)
    acc_scr[...] = jnp.zeros(acc_scr.shape, F32)
    for p in range(nblk):
        pr = jnp.exp(lg_scr[p] - m)
        l = l + jnp.sum(pr, axis=-1, keepdims=True)
        _, vblk = block_kv(p)
        acc_scr[...] += _dot(pr.astype(BF16), vblk.astype(BF16))
    out = acc_scr[...] / l
    for h in range(N_ATT_HEADS):
        hs = slice(h * ATT_HEAD_DIM, (h + 1) * ATT_HEAD_DIM)
        o_ref[0, :, hs] = out[h * t:(h + 1) * t, hs].astype(o_ref.dtype)


def _attn_sample(page_table, big3, qi_t, small3, kin3, bias_rows, cache_kidx, cache_k, cache_v, *, topk):
    b, t, _ = big3.shape
    n_pages = page_table.shape[1]
    body = functools.partial(_attn_sample_body, n_pages=n_pages, t=t, topk=topk)

    def page_spec(width, p):
        return pl.BlockSpec((1, PAGE_SIZE, width), lambda bi, pt: (pt[bi, p], 0, 0))

    in_specs = [
        pl.BlockSpec((1, t, ATT_WIDTH), lambda bi, pt: (bi, 0, COL_Q // ATT_WIDTH)),
        pl.BlockSpec((1, t, ATT_WIDTH), lambda bi, pt: (bi, 0, COL_K // ATT_WIDTH)),
        pl.BlockSpec((1, t, ATT_WIDTH), lambda bi, pt: (bi, 0, COL_V // ATT_WIDTH)),
        pl.BlockSpec((1, N_IDX_HEADS, t, IDX_DIM), lambda bi, pt: (bi, 0, 0, 0)),
        pl.BlockSpec((1, t, SMALL_WIDTH), lambda bi, pt: (bi, 0, 0)),
        pl.BlockSpec((1, t, IDX_DIM), lambda bi, pt: (bi, 0, 0)),
        pl.BlockSpec(bias_rows.shape, lambda bi, pt: (0, 0, 0)),
    ]
    in_specs += [page_spec(IDX_DIM, p) for p in range(n_pages)]
    in_specs += [page_spec(ATT_WIDTH, p) for p in range(n_pages)]
    in_specs += [page_spec(ATT_WIDTH, p) for p in range(n_pages)]
    rows = N_ATT_HEADS * t
    return pl.pallas_call(
        body,
        grid_spec=pltpu.PrefetchScalarGridSpec(
            num_scalar_prefetch=1,
            grid=(b,),
            in_specs=in_specs,
            out_specs=pl.BlockSpec((1, t, ATT_WIDTH), lambda bi, pt: (bi, 0, 0)),
            scratch_shapes=[
                pltpu.VMEM((n_pages + 1, t, LANES), jnp.int32),
                pltpu.VMEM((N_IDX_HEADS, t, LANES), F32),
                pltpu.VMEM((n_pages + 1, rows, LANES), F32),
                pltpu.VMEM((rows, ATT_WIDTH), F32),
            ],
        ),
        out_shape=jax.ShapeDtypeStruct((b, t, ATT_WIDTH), BF16),
        compiler_params=pltpu.CompilerParams(
            dimension_semantics=("arbitrary",), vmem_limit_bytes=VMEM_LIMIT),
        name="attn_sample",
    )(page_table, big3, big3, big3, qi_t, small3, kin3, bias_rows,
      *([cache_kidx] * n_pages), *([cache_k] * n_pages), *([cache_v] * n_pages))


def _split3(a):
    a1 = a.astype(BF16)
    r1 = a - a1.astype(F32)
    a2 = r1.astype(BF16)
    a3 = (r1 - a2.astype(F32)).astype(BF16)
    return a1, a2, a3


def _ssd_body(x_ref, bm_ref, cm_ref, z_ref, sm_ref, smt_ref, pre_ref, h0_ref,
              cw_ref, cb_ref, dtb_ref, dtbt_ref, alog_ref, alogt_ref, dskip_ref, gn_ref,
              y_ref, tail_ref, hout_ref, xp_scr, h_scr, y_scr, *, l, l_valid):
    c = pl.program_id(1)
    nc = pl.num_programs(1)
    off = SUBLANES

    @pl.when(c == 0)
    def _():
        xp_scr[off - (CONV_W - 1):off, :] = pre_ref[0]
        h_scr[...] = h0_ref[0]

    if l_valid < l:
        xp_scr[off + l_valid:off + l, :] = jnp.zeros((l - l_valid, CONV_DIM), F32)
    xp_scr[off:off + l_valid, 0:SSM_WIDTH] = x_ref[0]
    xp_scr[off:off + l_valid, SSM_WIDTH:COL_C - COL_X] = bm_ref[0]
    xp_scr[off:off + l_valid, COL_C - COL_X:CONV_DIM] = cm_ref[0]

    conv = cb_ref[...]
    for j in range(CONV_W):
        s0 = off - (CONV_W - 1) + j
        conv = conv + xp_scr[s0:s0 + l, :] * cw_ref[j:j + 1, :]
    xc = _silu(conv)
    tail = xp_scr[off + l_valid - (CONV_W - 1):off + l_valid, :]
    tail_ref[0] = tail
    xp_scr[off - (CONV_W - 1):off, :] = tail

    xs = xc[:, :SSM_WIDTH]
    bc = [xc[:, SSM_WIDTH + g * SSM_STATE:SSM_WIDTH + (g + 1) * SSM_STATE].astype(BF16)
          for g in range(N_SSM_GROUPS)]
    cc = [xc[:, SSM_WIDTH + (N_SSM_GROUPS + g) * SSM_STATE:SSM_WIDTH + (N_SSM_GROUPS + g + 1) * SSM_STATE]
          .astype(BF16) for g in range(N_SSM_GROUPS)]

    def softplus(v):
        return jnp.maximum(v, 0.0) + jnp.log1p(jnp.exp(-jnp.abs(v)))

    dt_raw = sm_ref[0, :, SM_DT:SM_DT + N_SSM_HEADS]
    dtt_raw = smt_ref[0, SM_DT:SM_DT + N_SSM_HEADS, :]
    if l_valid < l:
        dt_raw = jnp.concatenate([dt_raw, jnp.zeros((l - l_valid, N_SSM_HEADS), F32)], axis=0)
        dtt_raw = jnp.concatenate([dtt_raw, jnp.zeros((N_SSM_HEADS, l - l_valid), F32)], axis=1)
    dt = softplus(dt_raw + dtb_ref[...])
    dtt = softplus(dtt_raw + dtbt_ref[...])
    if l_valid < l:
        dt = jnp.where(lax.broadcasted_iota(jnp.int32, dt.shape, 0) < l_valid, dt, 0.0)
        dtt = jnp.where(lax.broadcasted_iota(jnp.int32, dtt.shape, 1) < l_valid, dtt, 0.0)
    a = dt * (-jnp.exp(alog_ref[...]))
    at = dtt * (-jnp.exp(alogt_ref[...]))

    ri = lax.broadcasted_iota(jnp.int32, (l, l), 0)
    ci = lax.broadcasted_iota(jnp.int32, (l, l), 1)
    causal = ri >= ci
    tri = jnp.where(causal, 1.0, 0.0).astype(BF16)
    tri_u = jnp.where(ri <= ci, 1.0, 0.0).astype(BF16)
    acum = sum(_dot(tri, piece) for piece in _split3(a))
    acum_t = sum(_dot(piece, tri_u) for piece in _split3(at))

    cb = [_dot_nt(cc[g], bc[g]) for g in range(N_SSM_GROUPS)]
    heads_per_group = N_SSM_HEADS // N_SSM_GROUPS
    for h in range(N_SSM_HEADS):
        g = h // heads_per_group
        hs = slice(h * SSM_HEAD_DIM, (h + 1) * SSM_HEAD_DIM)
        col = acum[:, h:h + 1]
        rowv = acum_t[h:h + 1, :]
        last = acum[l - 1:l, h:h + 1]
        decay = jnp.exp(jnp.where(causal, col - rowv, -jnp.inf))
        mmat = cb[g] * decay * dtt[h:h + 1, :]
        xh = xs[:, hs]
        hst = h_scr[h]
        y_intra = _dot(mmat.astype(BF16), xh.astype(BF16))
        y_inter = _dot_nt(cc[g], hst.astype(BF16)) * jnp.exp(col)
        wts = jnp.exp(last - col) * dt[:, h:h + 1]
        xw = (xh * wts).astype(BF16)
        h_scr[h] = hst * jnp.exp(last) + _dot_tn(xw, bc[g])
        y_scr[:, hs] = y_intra + y_inter + dskip_ref[h] * xh

    y = y_scr[0:l_valid, :]
    y_ref[0] = _rms(y * _silu(z_ref[0]), gn_ref[...]).astype(y_ref.dtype)

    @pl.when(c == nc - 1)
    def _():
        hout_ref[0] = h_scr[...]


def _ssd(big3, small3, small3_t, prefix, h0, conv_w, conv_b, dt_bias, a_log, d_skip, ssm_norm, *, l, l_valid):
    b, t, _ = big3.shape
    nc = t // l_valid
    body = functools.partial(_ssd_body, l=l, l_valid=l_valid)
    gw = N_SSM_GROUPS * SSM_STATE
    row = lambda v: v.reshape(1, -1)
    colv = lambda v: v.reshape(-1, 1)
    const2 = lambda shape: pl.BlockSpec(shape, lambda bi, c: (0, 0))
    return pl.pallas_call(
        body,
        grid=(b, nc),
        in_specs=[
            pl.BlockSpec((1, l_valid, SSM_WIDTH), lambda bi, c: (bi, c, COL_X // SSM_WIDTH)),
            pl.BlockSpec((1, l_valid, gw), lambda bi, c: (bi, c, COL_B // gw)),
            pl.BlockSpec((1, l_valid, gw), lambda bi, c: (bi, c, COL_C // gw)),
            pl.BlockSpec((1, l_valid, SSM_WIDTH), lambda bi, c: (bi, c, COL_Z // SSM_WIDTH)),
            pl.BlockSpec((1, l_valid, SMALL_WIDTH), lambda bi, c: (bi, c, 0)),
            pl.BlockSpec((1, SMALL_WIDTH, l_valid), lambda bi, c: (bi * nc + c, 0, 0)),
            pl.BlockSpec((1, CONV_W - 1, CONV_DIM), lambda bi, c: (bi, 0, 0)),
            pl.BlockSpec((1, N_SSM_HEADS, SSM_HEAD_DIM, SSM_STATE), lambda bi, c: (bi, 0, 0, 0)),
            const2((CONV_W, CONV_DIM)),
            const2((1, CONV_DIM)),
            const2((1, N_SSM_HEADS)),
            const2((N_SSM_HEADS, 1)),
            const2((1, N_SSM_HEADS)),
            const2((N_SSM_HEADS, 1)),
            pl.BlockSpec(memory_space=pltpu.SMEM),
            const2((1, SSM_WIDTH)),
        ],
        out_specs=[
            pl.BlockSpec((1, l_valid, SSM_WIDTH), lambda bi, c: (bi, c, 0)),
            pl.BlockSpec((1, CONV_W - 1, CONV_DIM), lambda bi, c: (bi, 0, 0)),
            pl.BlockSpec((1, N_SSM_HEADS, SSM_HEAD_DIM, SSM_STATE), lambda bi, c: (bi, 0, 0, 0)),
        ],
        out_shape=[
            jax.ShapeDtypeStruct((b, t, SSM_WIDTH), BF16),
            jax.ShapeDtypeStruct((b, CONV_W - 1, CONV_DIM), F32),
            jax.ShapeDtypeStruct((b, N_SSM_HEADS, SSM_HEAD_DIM, SSM_STATE), F32),
        ],
        scratch_shapes=[
            pltpu.VMEM((SUBLANES + l, CONV_DIM), F32),
            pltpu.VMEM((N_SSM_HEADS, SSM_HEAD_DIM, SSM_STATE), F32),
            pltpu.VMEM((l, SSM_WIDTH), F32),
        ],
        compiler_params=pltpu.CompilerParams(
            dimension_semantics=("parallel", "arbitrary"), vmem_limit_bytes=VMEM_LIMIT),
        name="ssd",
    )(big3, big3, big3, big3, small3, small3_t, prefix, h0,
      conv_w, row(conv_b), row(dt_bias), colv(dt_bias), row(a_log), colv(a_log), d_skip, row(ssm_norm))


def _ffn_body(x_ref, att_ref, y_ref, woa_ref, woy_ref, gf_ref, wg_ref, wu_ref, wd_ref, gl_ref,
              o_ref, f_scr):
    j = pl.program_id(1)

    @pl.when(j == 0)
    def _():
        h = x_ref[...] + _dot(att_ref[...], woa_ref[...]) + _dot(y_ref[...], woy_ref[...])
        o_ref[...] = h
        f_scr[...] = _rms(h, gf_ref[...]).astype(BF16)

    f = f_scr[...]
    act = _silu(_dot(f, wg_ref[...])) * _dot(f, wu_ref[...])
    o_ref[...] += _dot(act.astype(BF16), wd_ref[...])

    @pl.when(j == pl.num_programs(1) - 1)
    def _():
        o_ref[...] = _rms(o_ref[...], gl_ref[...])


def _out_ffn(x2, att2, y2, woa, woy, gf, wg, wu, wd, gl, *, tm, th):
    m, d = x2.shape
    hidden = wg.shape[1]
    resident = dict(pipeline_mode=pl.Buffered(1))
    return pl.pallas_call(
        _ffn_body,
        grid=(m // tm, hidden // th),
        in_specs=[
            pl.BlockSpec((tm, d), lambda i, j: (i, 0)),
            pl.BlockSpec((tm, ATT_WIDTH), lambda i, j: (i, 0)),
            pl.BlockSpec((tm, SSM_WIDTH), lambda i, j: (i, 0)),
            pl.BlockSpec((ATT_WIDTH, d), lambda i, j: (0, 0), **resident),
            pl.BlockSpec((SSM_WIDTH, d), lambda i, j: (0, 0), **resident),
            pl.BlockSpec((1, d), lambda i, j: (0, 0)),
            pl.BlockSpec((d, th), lambda i, j: (0, j)),
            pl.BlockSpec((d, th), lambda i, j: (0, j)),
            pl.BlockSpec((th, d), lambda i, j: (j, 0)),
            pl.BlockSpec((1, d), lambda i, j: (0, 0)),
        ],
        out_specs=pl.BlockSpec((tm, d), lambda i, j: (i, 0)),
        out_shape=jax.ShapeDtypeStruct((m, d), F32),
        scratch_shapes=[pltpu.VMEM((tm, d), BF16)],
        compiler_params=pltpu.CompilerParams(
            dimension_semantics=("parallel", "arbitrary"), vmem_limit_bytes=VMEM_LIMIT),
        name="out_ffn",
    )(x2, att2, y2, woa, woy, gf, wg, wu, wd, gl)


def _t5_bucket(rel):
    max_exact = N_BUCKETS // 2
    relf = jnp.maximum(rel, 1).astype(jnp.float32)
    large = max_exact + (jnp.log(relf / max_exact) / math.log(MAX_DISTANCE / max_exact)
                         * (N_BUCKETS - max_exact)).astype(jnp.int32)
    large = jnp.minimum(large, N_BUCKETS - 1)
    return jnp.where(rel < max_exact, rel, large)


def _bias_tiles(rel_bias, rows):
    assert MAX_DISTANCE <= LANES
    ii = jnp.arange(rows, dtype=jnp.int32)[:, None]
    jj = jnp.arange(LANES, dtype=jnp.int32)[None, :]
    tiles = []
    for dd in (2, 1, 0):
        rel = jnp.maximum(dd * LANES + ii - jj, 0)
        tiles.append(jnp.moveaxis(rel_bias[_t5_bucket(rel)], -1, 0))
    return jnp.stack(tiles)


def _layer(x, att_call, prefix, h0, l, l_valid, rel_bias, p, *, tm):
    b, t, d = x.shape
    m = b * t
    tm = min(tm, m)
    big, small, kin = _inproj(x.reshape(m, d), p["norm_attn"], p["wb"], p["ws"], p["idx_k_norm"], tm=tm, tn=p["tn"])
    big3 = big.reshape(b, t, BIG_WIDTH)
    small3 = small.reshape(b, t, SMALL_WIDTH)
    kin3 = kin.reshape(b, t, IDX_DIM)
    k32 = big3[:, :, COL_K:COL_K + ATT_WIDTH]
    v32 = big3[:, :, COL_V:COL_V + ATT_WIDTH]
    qi_t = jnp.transpose(big3[:, :, COL_QI:COL_QI + N_IDX_HEADS * IDX_DIM].reshape(b, t, N_IDX_HEADS, IDX_DIM),
                         (0, 2, 1, 3))
    att = att_call(big3, qi_t, small3, kin3, k32, v32)
    nc = t // l_valid
    small3_t = jnp.transpose(small3.reshape(b * nc, l_valid, SMALL_WIDTH), (0, 2, 1))
    y, tail, h_fin = _ssd(big3, small3, small3_t, prefix, h0, p["conv_w"], p["conv_b"], p["dt_bias"],
                          p["a_log"], p["d_skip"], p["ssm_norm"], l=l, l_valid=l_valid)
    out = _out_ffn(x.reshape(m, d), att.reshape(m, ATT_WIDTH), y.reshape(m, SSM_WIDTH), p["woa"], p["woy"],
                   p["norm_ffn"], p["wg"], p["wu"], p["wd"], p["norm_final"], tm=tm, th=p["th"])
    hd = (b, t, N_ATT_HEADS, ATT_HEAD_DIM)
    return out.reshape(b, t, d), k32.reshape(hd), v32.reshape(hd), kin3, tail, h_fin


def kernel(x_prompt, x_sample, cache_k, cache_v, cache_kidx, state_conv, state_ssm, page_table, rel_bias,
           norm_attn, w_in, idx_k_norm, conv_w, conv_b, dt_bias, a_log, d_skip, ssm_norm, w_out, norm_ffn,
           w_gate, w_up, w_down, norm_final):
    depth = w_in.shape[0]
    assert depth == 1
    bp, s, d = x_prompt.shape
    bs, t, _ = x_sample.shape
    n_pages = page_table.shape[1]
    past = n_pages * PAGE_SIZE
    lyr = 0

    w = w_in[lyr]
    o_ki = 3 * ATT_WIDTH + N_IDX_HEADS * IDX_DIM
    o_z = o_ki + IDX_DIM + N_IDX_HEADS
    o_dt = o_z + SSM_WIDTH + CONV_DIM
    wb = jnp.concatenate([w[:, :o_ki], w[:, o_z:o_dt]], axis=1).astype(BF16)
    ws = jnp.concatenate([w[:, o_ki:o_z], w[:, o_dt:],
                          jnp.zeros((d, SMALL_WIDTH - (o_z - o_ki) - N_SSM_HEADS), F32)], axis=1).astype(BF16)
    row = lambda v: v.reshape(1, -1)
    p = dict(
        norm_attn=row(norm_attn[lyr]), wb=wb, ws=ws, idx_k_norm=row(idx_k_norm[lyr]),
        conv_w=conv_w[lyr], conv_b=conv_b[lyr], dt_bias=dt_bias[lyr], a_log=a_log[lyr], d_skip=d_skip[lyr],
        ssm_norm=ssm_norm[lyr],
        woa=w_out[lyr, :ATT_WIDTH].astype(BF16), woy=w_out[lyr, ATT_WIDTH:].astype(BF16),
        norm_ffn=row(norm_ffn[lyr]), wg=w_gate[lyr].astype(BF16), wu=w_up[lyr].astype(BF16),
        wd=w_down[lyr].astype(BF16), norm_final=row(norm_final),
        tn=BIG_WIDTH // 4, th=512,
    )

    topk_p = min(TOPK_MAX, s // 4)
    tiles_p = _bias_tiles(rel_bias, LANES)

    def att_prompt(big3, qi_t, small3, kin3, k32, v32):
        return _attn_prompt(big3, qi_t, small3, kin3, k32.astype(BF16), v32.astype(BF16), tiles_p,
                            tq=LANES, topk=topk_p)

    zero_conv = jnp.zeros((bp, CONV_W - 1, CONV_DIM), F32)
    zero_ssm = jnp.zeros((bp, N_SSM_HEADS, SSM_HEAD_DIM, SSM_STATE), F32)
    yp, kp, vp, kip, cp, sp = _layer(x_prompt, att_prompt, zero_conv, zero_ssm, SSD_CHUNK, SSD_CHUNK,
                                     rel_bias, p, tm=512)

    topk_s = min(TOPK_MAX, (past + t) // 4)
    tiles_s = _bias_tiles(rel_bias, t).reshape(3, N_ATT_HEADS * t, LANES)
    n_phys = cache_k.shape[1]
    ck = cache_k[lyr].reshape(n_phys, PAGE_SIZE, ATT_WIDTH)
    cv = cache_v[lyr].reshape(n_phys, PAGE_SIZE, ATT_WIDTH)
    cki = cache_kidx[lyr]

    def att_sample(big3, qi_t, small3, kin3, k32, v32):
        return _attn_sample(page_table, big3, qi_t, small3, kin3, tiles_s, cki, ck, cv, topk=topk_s)

    ys, ks, vs, kis, cs, ss = _layer(x_sample, att_sample, state_conv[lyr], state_ssm[lyr], SSD_CHUNK, t,
                                     rel_bias, p, tm=512)

    st = lambda a: a[None]
    return (yp, ys, st(kp), st(vp), st(kip), st(cp), st(sp), st(ks), st(vs), st(kis), st(cs), st(ss))
```

```python
import functools
import math

import numpy as np
import jax
import jax.numpy as jnp
from jax import lax
from jax.experimental import pallas as pl
from jax.experimental.pallas import tpu as pltpu

N_ATT_HEADS = 8
ATT_HEAD_DIM = 128
ATT_WIDTH = N_ATT_HEADS * ATT_HEAD_DIM
N_IDX_HEADS = 16
IDX_DIM = 64
TOPK_MAX = 256
N_SSM_HEADS = 16
SSM_HEAD_DIM = 64
SSM_WIDTH = N_SSM_HEADS * SSM_HEAD_DIM
N_SSM_GROUPS = 2
SSM_STATE = 128
CONV_W = 4
CONV_DIM = SSM_WIDTH + 2 * N_SSM_GROUPS * SSM_STATE
SSD_CHUNK = 128
PAGE_SIZE = 128
N_BUCKETS = 32
MAX_DISTANCE = 128
NORM_EPS = 1e-6

LANES = 128
SUBLANES = 8
VMEM_LIMIT = 56 * 1024 * 1024

COL_Q, COL_K, COL_V, COL_QI, COL_Z, COL_X = (i * ATT_WIDTH for i in range(6))
COL_B = COL_X + SSM_WIDTH
COL_C = COL_B + N_SSM_GROUPS * SSM_STATE
BIG_WIDTH = COL_C + N_SSM_GROUPS * SSM_STATE
SM_KI, SM_WI, SM_DT = 0, IDX_DIM, IDX_DIM + N_IDX_HEADS
SMALL_WIDTH = LANES

F32 = jnp.float32
BF16 = jnp.bfloat16

_INT_MIN = -(2 ** 31)
_INT_MAX = 2 ** 31 - 1
_KEY_NEG_INF = int(np.array([-np.inf], np.float32).view(np.int32)[0]) ^ 0x7FFFFFFF
_NEG_BIG = -1e30


def _dot(a, b):
    return jnp.dot(a, b, preferred_element_type=F32)


def _dot_nt(a, b):
    return lax.dot_general(a, b, (((1,), (1,)), ((), ())), preferred_element_type=F32)


def _dot_tn(a, b):
    return lax.dot_general(a, b, (((0,), (0,)), ((), ())), preferred_element_type=F32)


def _rms(x, g):
    return x * lax.rsqrt(jnp.mean(x * x, axis=-1, keepdims=True) + NORM_EPS) * g


def _silu(x):
    return x * (1.0 / (1.0 + jnp.exp(-x)))


def _inproj_body(x_ref, g_ref, wb_ref, ws_ref, gk_ref, big_ref, small_ref, kin_ref, u_scr):
    @pl.when(pl.program_id(1) == 0)
    def _():
        ub = _rms(x_ref[...], g_ref[...]).astype(BF16)
        u_scr[...] = ub
        sm = _dot(ub, ws_ref[...])
        small_ref[...] = sm
        kin_ref[...] = _rms(sm[:, SM_KI:SM_KI + IDX_DIM], gk_ref[...])

    big_ref[...] = _dot(u_scr[...], wb_ref[...])


def _inproj(x2, g, wb, ws, gk, *, tm, tn):
    m, d = x2.shape
    return pl.pallas_call(
        _inproj_body,
        grid=(m // tm, BIG_WIDTH // tn),
        in_specs=[
            pl.BlockSpec((tm, d), lambda i, j: (i, 0)),
            pl.BlockSpec((1, d), lambda i, j: (0, 0)),
            pl.BlockSpec((d, tn), lambda i, j: (0, j)),
            pl.BlockSpec((d, SMALL_WIDTH), lambda i, j: (0, 0)),
            pl.BlockSpec((1, IDX_DIM), lambda i, j: (0, 0)),
        ],
        out_specs=[
            pl.BlockSpec((tm, tn), lambda i, j: (i, j)),
            pl.BlockSpec((tm, SMALL_WIDTH), lambda i, j: (i, 0)),
            pl.BlockSpec((tm, IDX_DIM), lambda i, j: (i, 0)),
        ],
        out_shape=[
            jax.ShapeDtypeStruct((m, BIG_WIDTH), F32),
            jax.ShapeDtypeStruct((m, SMALL_WIDTH), F32),
            jax.ShapeDtypeStruct((m, IDX_DIM), F32),
        ],
        scratch_shapes=[pltpu.VMEM((tm, d), BF16)],
        compiler_params=pltpu.CompilerParams(
            dimension_semantics=("parallel", "arbitrary"), vmem_limit_bytes=VMEM_LIMIT),
        name="inproj",
    )(x2, g, wb, ws, gk)


def _sortable_key(s):
    s = jnp.where(s == 0.0, 0.0, s)
    b = pltpu.bitcast(s, jnp.int32)
    return b ^ ((b >> 31) & 0x7FFFFFFF)


def _count_keys(key_ref, nblk, pred, key_axis):
    def body(kb, acc):
        return acc + jnp.where(pred(key_ref[kb], kb), 1.0, 0.0)

    acc = jnp.zeros(key_ref.shape[1:], F32)
    if isinstance(nblk, int):
        for kb in range(nblk):
            acc = body(kb, acc)
    else:
        acc = lax.fori_loop(0, nblk, body, acc)
    return jnp.sum(acc, axis=key_axis, keepdims=True)


def _topk_select_params(key_ref, nblk, topk, key_axis):
    shape = key_ref.shape[1:]
    qshape = tuple(1 if a == key_axis else n for a, n in enumerate(shape))
    blk = shape[key_axis]
    kf = float(topk)

    def count_ge(cand):
        cb = jnp.broadcast_to(cand, shape)
        return _count_keys(key_ref, nblk, lambda k, kb: k >= cb, key_axis)

    res = jnp.where(count_ge(jnp.zeros(qshape, jnp.int32)) >= kf, 0, _INT_MIN).astype(jnp.int32)

    def bit_body(it, res):
        cand = res | jnp.left_shift(jnp.int32(1), 30 - it)
        return jnp.where(count_ge(cand) >= kf, cand, res)

    thr = lax.fori_loop(0, 31, bit_body, res)
    thr_b = jnp.broadcast_to(thr, shape)
    n_gt = _count_keys(key_ref, nblk, lambda k, kb: k > thr_b, key_axis)
    n_ge = count_ge(thr)
    tied = jnp.logical_and(n_ge > kf, thr > _KEY_NEG_INF)
    any_tied = jnp.max(jnp.where(tied, 1.0, 0.0)) > 0.0
    need = kf - n_gt
    kpos = lax.broadcasted_iota(jnp.int32, shape, key_axis)

    def tie_break():
        def body(it, cut):
            cand = cut | jnp.left_shift(jnp.int32(1), 30 - it)
            cand_b = jnp.broadcast_to(cand, shape)
            n = _count_keys(key_ref, nblk,
                            lambda k, kb: jnp.logical_and(k == thr_b, kpos + kb * blk < cand_b), key_axis)
            return jnp.where(n <= need, cand, cut)

        return lax.fori_loop(0, 31, body, jnp.zeros(qshape, jnp.int32))

    cut = lax.cond(any_tied, tie_break, lambda: jnp.full(qshape, _INT_MAX, jnp.int32))
    return thr, cut


def _attn_prompt_body(q_ref, qi_ref, smt_ref, kin_ref, k_ref, vt_ref, bias_ref, o_ref,
                      key_scr, mask_scr, lg_scr, *, topk, group):
    i = pl.program_id(1)
    q0 = i * LANES
    blk = (LANES, LANES)
    scale = ATT_HEAD_DIM ** -0.5
    nq = key_scr.shape[0]

    w_t = smt_ref[0, SM_WI:SM_WI + N_IDX_HEADS, :]
    qi = qi_ref[0].reshape(N_IDX_HEADS * LANES, IDX_DIM).astype(BF16)
    kpos = lax.broadcasted_iota(jnp.int32, blk, 0)
    qpos = lax.broadcasted_iota(jnp.int32, blk, 1) + q0
    qb = q_ref[0].astype(BF16)

    def tile(nb, first_near):
        for kb in range(nb):
            kin = kin_ref[0, kb * LANES:(kb + 1) * LANES, :].astype(BF16)
            s = _dot_nt(kin, qi)
            score = jnp.zeros(blk, F32)
            for h in range(N_IDX_HEADS):
                score = score + jnp.maximum(s[:, h * LANES:(h + 1) * LANES], 0.0) * w_t[h:h + 1, :]
            if kb >= first_near:
                score = jnp.where(kpos + kb * LANES <= qpos, score, -jnp.inf)
            key_scr[kb] = _sortable_key(score)

        thr, cut = _topk_select_params(key_scr, nb, topk, key_axis=0)
        thr_b = jnp.broadcast_to(thr, blk)
        cut_b = jnp.broadcast_to(cut, blk)
        for kb in range(nb):
            kp = kpos + kb * LANES
            key = key_scr[kb]
            valid = jnp.logical_or(key > thr_b, jnp.logical_and(key == thr_b, kp < cut_b))
            if kb >= first_near:
                valid = jnp.logical_and(valid, kp <= qpos)
            mask_scr[kb * LANES:(kb + 1) * LANES, :] = jnp.where(valid, 0.0, -jnp.inf)

        nk = nb * LANES
        for h in range(N_ATT_HEADS):
            hs = slice(h * ATT_HEAD_DIM, (h + 1) * ATT_HEAD_DIM)
            lg = _dot_nt(k_ref[0, 0:nk, hs], qb[:, hs]) * scale + mask_scr[0:nk, :]
            for kb in range(first_near, nb):
                tix = 2 - jnp.clip(i - kb, 0, 2)
                lg_scr[kb * LANES:(kb + 1) * LANES, :] = lg[kb * LANES:(kb + 1) * LANES, :] + bias_ref[tix, h]
            if first_near > 0:
                lg_scr[0:first_near * LANES, :] = lg[0:first_near * LANES, :]
            lg = lg_scr[0:nk, :]
            m = jnp.max(lg, axis=0, keepdims=True)
            p = jnp.exp(lg - m)
            l = jnp.sum(p, axis=0, keepdims=True)
            out_t = _dot(vt_ref[0, hs, 0:nk], p.astype(BF16)) / l
            o_ref[0, :, hs] = out_t.T.astype(o_ref.dtype)

    ngroups = -(-nq // group)
    for c in range(ngroups):
        nb = min((c + 1) * group, nq)

        @pl.when(i // group == c)
        def _(nb=nb, c=c):
            tile(nb, max(c * group - 1, 0))


def _attn_prompt(big3, qi_t, small3_t, kin3, k16, vt16, bias_tiles_t, *, topk, group):
    b, s, _ = big3.shape
    nq = s // LANES
    body = functools.partial(_attn_prompt_body, topk=topk, group=group)
    return pl.pallas_call(
        body,
        grid=(b, nq),
        in_specs=[
            pl.BlockSpec((1, LANES, ATT_WIDTH), lambda bi, i: (bi, i, COL_Q // ATT_WIDTH)),
            pl.BlockSpec((1, N_IDX_HEADS, LANES, IDX_DIM), lambda bi, i: (bi, 0, i, 0)),
            pl.BlockSpec((1, SMALL_WIDTH, LANES), lambda bi, i: (bi * nq + i, 0, 0)),
            pl.BlockSpec((1, s, IDX_DIM), lambda bi, i: (bi, 0, 0)),
            pl.BlockSpec((1, s, ATT_WIDTH), lambda bi, i: (bi, 0, 0)),
            pl.BlockSpec((1, ATT_WIDTH, s), lambda bi, i: (bi, 0, 0)),
            pl.BlockSpec(bias_tiles_t.shape, lambda bi, i: (0, 0, 0, 0)),
        ],
        out_specs=pl.BlockSpec((1, LANES, ATT_WIDTH), lambda bi, i: (bi, i, 0)),
        out_shape=jax.ShapeDtypeStruct((b, s, ATT_WIDTH), BF16),
        scratch_shapes=[
            pltpu.VMEM((nq, LANES, LANES), jnp.int32),
            pltpu.VMEM((s, LANES), F32),
            pltpu.VMEM((s, LANES), F32),
        ],
        compiler_params=pltpu.CompilerParams(
            dimension_semantics=("parallel", "arbitrary"), vmem_limit_bytes=VMEM_LIMIT),
        name="attn_prompt",
    )(big3, qi_t, small3_t, kin3, k16, vt16, bias_tiles_t)


def _attn_sample_body(pt_ref, q_ref, knew_ref, vnew_ref, qi_ref, sm_ref, kinew_ref, bias_ref, *rest,
                      n_pages, t, topk):
    kidx_refs = rest[:n_pages]
    k_refs = rest[n_pages:2 * n_pages]
    v_refs = rest[2 * n_pages:3 * n_pages]
    o_ref = rest[3 * n_pages]
    key_scr, wb_scr, lg_scr = rest[3 * n_pages + 1:]
    nblk = n_pages + 1
    rows = N_ATT_HEADS * t
    flat = PAGE_SIZE * N_ATT_HEADS
    scale = ATT_HEAD_DIM ** -0.5

    wi = sm_ref[0, :, SM_WI:SM_WI + N_IDX_HEADS]
    for h in range(N_IDX_HEADS):
        wb_scr[h] = jnp.broadcast_to(wi[:, h:h + 1], (t, LANES))
    qi = qi_ref[0].reshape(N_IDX_HEADS * t, IDX_DIM).astype(BF16)
    trow = lax.broadcasted_iota(jnp.int32, (t, LANES), 0)
    lane = lax.broadcasted_iota(jnp.int32, (t, LANES), 1)
    new_ok = lane <= trow

    def scores(kin):
        s = _dot_nt(qi, kin.astype(BF16)).reshape(N_IDX_HEADS, t, LANES)
        return jnp.sum(jnp.maximum(s, 0.0) * wb_scr[...], axis=0)

    for p in range(n_pages):
        key_scr[p] = _sortable_key(scores(kidx_refs[p][0, 0]))
    kin_new = jnp.concatenate([kinew_ref[0], jnp.zeros((LANES - t, IDX_DIM), F32)], axis=0)
    key_scr[n_pages] = _sortable_key(jnp.where(new_ok, scores(kin_new), -jnp.inf))

    thr, cut = _topk_select_params(key_scr, nblk, topk, key_axis=1)
    thr_b = jnp.broadcast_to(thr, (t, LANES))
    cut_b = jnp.broadcast_to(cut, (t, LANES))

    q = q_ref[0]
    q64 = jnp.concatenate([q[:, h * ATT_HEAD_DIM:(h + 1) * ATT_HEAD_DIM] for h in range(N_ATT_HEADS)],
                          axis=0).astype(BF16)
    rr = lax.broadcasted_iota(jnp.int32, (rows, flat), 0)
    cc = lax.broadcasted_iota(jnp.int32, (rows, flat), 1)
    head_match = (cc & (N_ATT_HEADS - 1)) == (rr >> (t.bit_length() - 1))
    er = lax.broadcasted_iota(jnp.int32, (LANES, flat), 0)
    ec = lax.broadcasted_iota(jnp.int32, (LANES, flat), 1)
    expand = jnp.where((ec >> (N_ATT_HEADS.bit_length() - 1)) == er, 1.0, 0.0).astype(BF16)

    def flat_kv(refs, p):
        if p < n_pages:
            return refs[p][0, 0].reshape(flat, ATT_HEAD_DIM).astype(BF16), flat
        new = refs[p][0].reshape(t * N_ATT_HEADS, ATT_HEAD_DIM)
        pad = jnp.zeros((LANES - t * N_ATT_HEADS, ATT_HEAD_DIM), F32)
        return jnp.concatenate([new, pad], axis=0).astype(BF16), LANES

    k_all = list(k_refs) + [knew_ref]
    v_all = list(v_refs) + [vnew_ref]
    m = jnp.full((rows, 1), _NEG_BIG, F32)
    for p in range(nblk):
        key = key_scr[p]
        sel = jnp.logical_or(key > thr_b, jnp.logical_and(key == thr_b, lane + p * LANES < cut_b))
        if p == n_pages:
            sel = jnp.logical_and(sel, new_ok)
        kflat, width = flat_kv(k_all, p)
        selx = _dot(jnp.where(sel, 1.0, 0.0).astype(BF16), expand[:, :width])
        valid = jnp.logical_and(head_match[:, :width], jnp.concatenate([selx] * N_ATT_HEADS, axis=0) > 0.5)
        tix = 2 - min(n_pages - p, 2)
        lg = _dot_nt(q64, kflat) * scale + bias_ref[tix, :, :width]
        lg = jnp.where(valid, lg, _NEG_BIG)
        lg_scr[p, :, :width] = lg
        m = jnp.maximum(m, jnp.max(lg, axis=-1, keepdims=True))

    l = jnp.zeros((rows, 1), F32)
    acc = jnp.zeros((rows, ATT_HEAD_DIM), F32)
    for p in range(nblk):
        vflat, width = flat_kv(v_all, p)
        pr = jnp.exp(lg_scr[p, :, :width] - m)
        l = l + jnp.sum(pr, axis=-1, keepdims=True)
        acc = acc + _dot(pr.astype(BF16), vflat)
    out = acc / l
    for h in range(N_ATT_HEADS):
        o_ref[0, :, h * ATT_HEAD_DIM:(h + 1) * ATT_HEAD_DIM] = out[h * t:(h + 1) * t, :].astype(o_ref.dtype)


def _attn_sample(page_table, big3, k_new, v_new, qi_t, small3, kin3, bias_rows, cache_kidx, cache_k, cache_v,
                 *, topk):
    b, t, _ = big3.shape
    n_pages = page_table.shape[1]
    assert t & (t - 1) == 0 and N_ATT_HEADS & (N_ATT_HEADS - 1) == 0 and t * N_ATT_HEADS <= LANES
    body = functools.partial(_attn_sample_body, n_pages=n_pages, t=t, topk=topk)
    kv_block = (1, 1, PAGE_SIZE, N_ATT_HEADS, ATT_HEAD_DIM)
    new_block = (1, t, N_ATT_HEADS, ATT_HEAD_DIM)

    in_specs = [
        pl.BlockSpec((1, t, ATT_WIDTH), lambda bi, pt: (bi, 0, COL_Q // ATT_WIDTH)),
        pl.BlockSpec(new_block, lambda bi, pt: (bi, 0, 0, 0)),
        pl.BlockSpec(new_block, lambda bi, pt: (bi, 0, 0, 0)),
        pl.BlockSpec((1, N_IDX_HEADS, t, IDX_DIM), lambda bi, pt: (bi, 0, 0, 0)),
        pl.BlockSpec((1, t, SMALL_WIDTH), lambda bi, pt: (bi, 0, 0)),
        pl.BlockSpec((1, t, IDX_DIM), lambda bi, pt: (bi, 0, 0)),
        pl.BlockSpec(bias_rows.shape, lambda bi, pt: (0, 0, 0)),
    ]
    in_specs += [pl.BlockSpec((1, 1, PAGE_SIZE, IDX_DIM), lambda bi, pt, p=p: (0, pt[bi, p], 0, 0))
                 for p in range(n_pages)]
    in_specs += [pl.BlockSpec(kv_block, lambda bi, pt, p=p: (0, pt[bi, p], 0, 0, 0)) for p in range(n_pages)] * 2
    rows = N_ATT_HEADS * t
    return pl.pallas_call(
        body,
        grid_spec=pltpu.PrefetchScalarGridSpec(
            num_scalar_prefetch=1,
            grid=(b,),
            in_specs=in_specs,
            out_specs=pl.BlockSpec((1, t, ATT_WIDTH), lambda bi, pt: (bi, 0, 0)),
            scratch_shapes=[
                pltpu.VMEM((n_pages + 1, t, LANES), jnp.int32),
                pltpu.VMEM((N_IDX_HEADS, t, LANES), F32),
                pltpu.VMEM((n_pages + 1, rows, PAGE_SIZE * N_ATT_HEADS), F32),
            ],
        ),
        out_shape=jax.ShapeDtypeStruct((b, t, ATT_WIDTH), BF16),
        compiler_params=pltpu.CompilerParams(
            dimension_semantics=("arbitrary",), vmem_limit_bytes=VMEM_LIMIT),
        name="attn_sample",
    )(page_table, big3, k_new, v_new, qi_t, small3, kin3, bias_rows,
      *([cache_kidx] * n_pages), *([cache_k] * n_pages), *([cache_v] * n_pages))


def _split3(a):
    a1 = a.astype(BF16)
    r1 = a - a1.astype(F32)
    a2 = r1.astype(BF16)
    a3 = (r1 - a2.astype(F32)).astype(BF16)
    return a1, a2, a3


def _ssd_body(x_ref, bm_ref, cm_ref, z_ref, sm_ref, smt_ref, pre_ref, h0_ref,
              cw_ref, cb_ref, dtb_ref, dtbt_ref, alog_ref, alogt_ref, dskip_ref, gn_ref,
              y_ref, tail_ref, hout_ref, xp_scr, h_scr, y_scr, *, l, l_valid):
    c = pl.program_id(1)
    nc = pl.num_programs(1)
    off = SUBLANES

    @pl.when(c == 0)
    def _():
        xp_scr[off - (CONV_W - 1):off, :] = pre_ref[0]
        h_scr[...] = h0_ref[0]

    if l_valid < l:
        xp_scr[off + l_valid:off + l, :] = jnp.zeros((l - l_valid, CONV_DIM), F32)
    xp_scr[off:off + l_valid, 0:SSM_WIDTH] = x_ref[0]
    xp_scr[off:off + l_valid, SSM_WIDTH:COL_C - COL_X] = bm_ref[0]
    xp_scr[off:off + l_valid, COL_C - COL_X:CONV_DIM] = cm_ref[0]

    conv = cb_ref[...]
    for j in range(CONV_W):
        s0 = off - (CONV_W - 1) + j
        conv = conv + xp_scr[s0:s0 + l, :] * cw_ref[j:j + 1, :]
    xc = _silu(conv)
    tail = xp_scr[off + l_valid - (CONV_W - 1):off + l_valid, :]
    tail_ref[0] = tail
    xp_scr[off - (CONV_W - 1):off, :] = tail

    xs = xc[:, :SSM_WIDTH]
    bc = [xc[:, SSM_WIDTH + g * SSM_STATE:SSM_WIDTH + (g + 1) * SSM_STATE].astype(BF16)
          for g in range(N_SSM_GROUPS)]
    cc = [xc[:, SSM_WIDTH + (N_SSM_GROUPS + g) * SSM_STATE:SSM_WIDTH + (N_SSM_GROUPS + g + 1) * SSM_STATE]
          .astype(BF16) for g in range(N_SSM_GROUPS)]

    def softplus(v):
        return jnp.maximum(v, 0.0) + jnp.log1p(jnp.exp(-jnp.abs(v)))

    dt_raw = sm_ref[0, :, SM_DT:SM_DT + N_SSM_HEADS]
    dtt_raw = smt_ref[0, SM_DT:SM_DT + N_SSM_HEADS, :]
    if l_valid < l:
        dt_raw = jnp.concatenate([dt_raw, jnp.zeros((l - l_valid, N_SSM_HEADS), F32)], axis=0)
        dtt_raw = jnp.concatenate([dtt_raw, jnp.zeros((N_SSM_HEADS, l - l_valid), F32)], axis=1)
    dt = softplus(dt_raw + dtb_ref[...])
    dtt = softplus(dtt_raw + dtbt_ref[...])
    if l_valid < l:
        dt = jnp.where(lax.broadcasted_iota(jnp.int32, dt.shape, 0) < l_valid, dt, 0.0)
        dtt = jnp.where(lax.broadcasted_iota(jnp.int32, dtt.shape, 1) < l_valid, dtt, 0.0)
    a = dt * (-jnp.exp(alog_ref[...]))
    at = dtt * (-jnp.exp(alogt_ref[...]))

    ri = lax.broadcasted_iota(jnp.int32, (l, l), 0)
    ci = lax.broadcasted_iota(jnp.int32, (l, l), 1)
    causal = ri >= ci
    tri = jnp.where(causal, 1.0, 0.0).astype(BF16)
    tri_u = jnp.where(ri <= ci, 1.0, 0.0).astype(BF16)
    acum = sum(_dot(tri, piece) for piece in _split3(a))
    acum_t = sum(_dot(piece, tri_u) for piece in _split3(at))

    cb = [_dot_nt(cc[g], bc[g]) for g in range(N_SSM_GROUPS)]
    heads_per_group = N_SSM_HEADS // N_SSM_GROUPS
    for h in range(N_SSM_HEADS):
        g = h // heads_per_group
        hs = slice(h * SSM_HEAD_DIM, (h + 1) * SSM_HEAD_DIM)
        col = acum[:, h:h + 1]
        rowv = acum_t[h:h + 1, :]
        last = acum[l - 1:l, h:h + 1]
        decay = jnp.exp(jnp.where(causal, col - rowv, -jnp.inf))
        mmat = cb[g] * decay * dtt[h:h + 1, :]
        xh = xs[:, hs]
        hst = h_scr[h]
        y_intra = _dot(mmat.astype(BF16), xh.astype(BF16))
        y_inter = _dot_nt(cc[g], hst.astype(BF16)) * jnp.exp(col)
        wts = jnp.exp(last - col) * dt[:, h:h + 1]
        xw = (xh * wts).astype(BF16)
        h_scr[h] = hst * jnp.exp(last) + _dot_tn(xw, bc[g])
        y_scr[:, hs] = y_intra + y_inter + dskip_ref[h] * xh

    y = y_scr[0:l_valid, :]
    y_ref[0] = _rms(y * _silu(z_ref[0]), gn_ref[...]).astype(y_ref.dtype)

    @pl.when(c == nc - 1)
    def _():
        hout_ref[0] = h_scr[...]


def _ssd(big3, small3, small3_t, prefix, h0, conv_w, conv_b, dt_bias, a_log, d_skip, ssm_norm, *, l, l_valid):
    b, t, _ = big3.shape
    nc = t // l_valid
    body = functools.partial(_ssd_body, l=l, l_valid=l_valid)
    gw = N_SSM_GROUPS * SSM_STATE
    row = lambda v: v.reshape(1, -1)
    colv = lambda v: v.reshape(-1, 1)
    const2 = lambda shape: pl.BlockSpec(shape, lambda bi, c: (0, 0))
    return pl.pallas_call(
        body,
        grid=(b, nc),
        in_specs=[
            pl.BlockSpec((1, l_valid, SSM_WIDTH), lambda bi, c: (bi, c, COL_X // SSM_WIDTH)),
            pl.BlockSpec((1, l_valid, gw), lambda bi, c: (bi, c, COL_B // gw)),
            pl.BlockSpec((1, l_valid, gw), lambda bi, c: (bi, c, COL_C // gw)),
            pl.BlockSpec((1, l_valid, SSM_WIDTH), lambda bi, c: (bi, c, COL_Z // SSM_WIDTH)),
            pl.BlockSpec((1, l_valid, SMALL_WIDTH), lambda bi, c: (bi, c, 0)),
            pl.BlockSpec((1, SMALL_WIDTH, l_valid), lambda bi, c: (bi * nc + c, 0, 0)),
            pl.BlockSpec((1, CONV_W - 1, CONV_DIM), lambda bi, c: (bi, 0, 0)),
            pl.BlockSpec((1, N_SSM_HEADS, SSM_HEAD_DIM, SSM_STATE), lambda bi, c: (bi, 0, 0, 0)),
            const2((CONV_W, CONV_DIM)),
            const2((1, CONV_DIM)),
            const2((1, N_SSM_HEADS)),
            const2((N_SSM_HEADS, 1)),
            const2((1, N_SSM_HEADS)),
            const2((N_SSM_HEADS, 1)),
            pl.BlockSpec(memory_space=pltpu.SMEM),
            const2((1, SSM_WIDTH)),
        ],
        out_specs=[
            pl.BlockSpec((1, l_valid, SSM_WIDTH), lambda bi, c: (bi, c, 0)),
            pl.BlockSpec((1, CONV_W - 1, CONV_DIM), lambda bi, c: (bi, 0, 0)),
            pl.BlockSpec((1, N_SSM_HEADS, SSM_HEAD_DIM, SSM_STATE), lambda bi, c: (bi, 0, 0, 0)),
        ],
        out_shape=[
            jax.ShapeDtypeStruct((b, t, SSM_WIDTH), BF16),
            jax.ShapeDtypeStruct((b, CONV_W - 1, CONV_DIM), F32),
            jax.ShapeDtypeStruct((b, N_SSM_HEADS, SSM_HEAD_DIM, SSM_STATE), F32),
        ],
        scratch_shapes=[
            pltpu.VMEM((SUBLANES + l, CONV_DIM), F32),
            pltpu.VMEM((N_SSM_HEADS, SSM_HEAD_DIM, SSM_STATE), F32),
            pltpu.VMEM((l, SSM_WIDTH), F32),
        ],
        compiler_params=pltpu.CompilerParams(
            dimension_semantics=("parallel", "arbitrary"), vmem_limit_bytes=VMEM_LIMIT),
        name="ssd",
    )(big3, big3, big3, big3, small3, small3_t, prefix, h0,
      conv_w, row(conv_b), row(dt_bias), colv(dt_bias), row(a_log), colv(a_log), d_skip, row(ssm_norm))


def _ffn_body(x_ref, att_ref, y_ref, woa_ref, woy_ref, gf_ref, wg_ref, wu_ref, wd_ref, gl_ref,
              o_ref, f_scr):
    j = pl.program_id(1)

    @pl.when(j == 0)
    def _():
        h = x_ref[...] + _dot(att_ref[...], woa_ref[...]) + _dot(y_ref[...], woy_ref[...])
        o_ref[...] = h
        f_scr[...] = _rms(h, gf_ref[...]).astype(BF16)

    f = f_scr[...]
    act = _silu(_dot(f, wg_ref[...])) * _dot(f, wu_ref[...])
    o_ref[...] += _dot(act.astype(BF16), wd_ref[...])

    @pl.when(j == pl.num_programs(1) - 1)
    def _():
        o_ref[...] = _rms(o_ref[...], gl_ref[...])


def _out_ffn(x2, att2, y2, woa, woy, gf, wg, wu, wd, gl, *, tm, th):
    m, d = x2.shape
    hidden = wg.shape[1]
    resident = dict(pipeline_mode=pl.Buffered(1))
    return pl.pallas_call(
        _ffn_body,
        grid=(m // tm, hidden // th),
        in_specs=[
            pl.BlockSpec((tm, d), lambda i, j: (i, 0)),
            pl.BlockSpec((tm, ATT_WIDTH), lambda i, j: (i, 0)),
            pl.BlockSpec((tm, SSM_WIDTH), lambda i, j: (i, 0)),
            pl.BlockSpec((ATT_WIDTH, d), lambda i, j: (0, 0), **resident),
            pl.BlockSpec((SSM_WIDTH, d), lambda i, j: (0, 0), **resident),
            pl.BlockSpec((1, d), lambda i, j: (0, 0)),
            pl.BlockSpec((d, th), lambda i, j: (0, j)),
            pl.BlockSpec((d, th), lambda i, j: (0, j)),
            pl.BlockSpec((th, d), lambda i, j: (j, 0)),
            pl.BlockSpec((1, d), lambda i, j: (0, 0)),
        ],
        out_specs=pl.BlockSpec((tm, d), lambda i, j: (i, 0)),
        out_shape=jax.ShapeDtypeStruct((m, d), F32),
        scratch_shapes=[pltpu.VMEM((tm, d), BF16)],
        compiler_params=pltpu.CompilerParams(
            dimension_semantics=("parallel", "arbitrary"), vmem_limit_bytes=VMEM_LIMIT),
        name="out_ffn",
    )(x2, att2, y2, woa, woy, gf, wg, wu, wd, gl)


def _t5_bucket(rel):
    max_exact = N_BUCKETS // 2
    relf = jnp.maximum(rel, 1).astype(jnp.float32)
    large = max_exact + (jnp.log(relf / max_exact) / math.log(MAX_DISTANCE / max_exact)
                         * (N_BUCKETS - max_exact)).astype(jnp.int32)
    large = jnp.minimum(large, N_BUCKETS - 1)
    return jnp.where(rel < max_exact, rel, large)


def _bias_tiles(rel_bias, rows):
    assert MAX_DISTANCE <= LANES
    ii = jnp.arange(rows, dtype=jnp.int32)[:, None]
    jj = jnp.arange(LANES, dtype=jnp.int32)[None, :]
    tiles = []
    for dd in (2, 1, 0):
        rel = jnp.maximum(dd * LANES + ii - jj, 0)
        tiles.append(jnp.moveaxis(rel_bias[_t5_bucket(rel)], -1, 0))
    return jnp.stack(tiles)


def _layer(x, att_call, prefix, h0, l, l_valid, p, *, tm):
    b, t, d = x.shape
    m = b * t
    tm = min(tm, m)
    big, small, kin = _inproj(x.reshape(m, d), p["norm_attn"], p["wb"], p["ws"], p["idx_k_norm"], tm=tm, tn=p["tn"])
    big3 = big.reshape(b, t, BIG_WIDTH)
    small3 = small.reshape(b, t, SMALL_WIDTH)
    kin3 = kin.reshape(b, t, IDX_DIM)
    hd = (b, t, N_ATT_HEADS, ATT_HEAD_DIM)
    k4 = big3[:, :, COL_K:COL_K + ATT_WIDTH].reshape(hd)
    v4 = big3[:, :, COL_V:COL_V + ATT_WIDTH].reshape(hd)
    qi_t = jnp.transpose(big3[:, :, COL_QI:COL_QI + N_IDX_HEADS * IDX_DIM].reshape(b, t, N_IDX_HEADS, IDX_DIM),
                         (0, 2, 1, 3))
    nc = t // l_valid
    small3_t = jnp.transpose(small3.reshape(b * nc, l_valid, SMALL_WIDTH), (0, 2, 1))
    att = att_call(big3, qi_t, small3, small3_t, kin3, k4, v4)
    y, tail, h_fin = _ssd(big3, small3, small3_t, prefix, h0, p["conv_w"], p["conv_b"], p["dt_bias"],
                          p["a_log"], p["d_skip"], p["ssm_norm"], l=l, l_valid=l_valid)
    out = _out_ffn(x.reshape(m, d), att.reshape(m, ATT_WIDTH), y.reshape(m, SSM_WIDTH), p["woa"], p["woy"],
                   p["norm_ffn"], p["wg"], p["wu"], p["wd"], p["norm_final"], tm=tm, th=p["th"])
    return out.reshape(b, t, d), k4, v4, kin3, tail, h_fin


def kernel(x_prompt, x_sample, cache_k, cache_v, cache_kidx, state_conv, state_ssm, page_table, rel_bias,
           norm_attn, w_in, idx_k_norm, conv_w, conv_b, dt_bias, a_log, d_skip, ssm_norm, w_out, norm_ffn,
           w_gate, w_up, w_down, norm_final):
    depth = w_in.shape[0]
    assert depth == 1
    bp, s, d = x_prompt.shape
    bs, t, _ = x_sample.shape
    n_pages = page_table.shape[1]
    past = n_pages * PAGE_SIZE
    lyr = 0

    w = w_in[lyr]
    o_ki = 3 * ATT_WIDTH + N_IDX_HEADS * IDX_DIM
    o_z = o_ki + IDX_DIM + N_IDX_HEADS
    o_dt = o_z + SSM_WIDTH + CONV_DIM
    wb = jnp.concatenate([w[:, :o_ki], w[:, o_z:o_dt]], axis=1).astype(BF16)
    ws = jnp.concatenate([w[:, o_ki:o_z], w[:, o_dt:],
                          jnp.zeros((d, SMALL_WIDTH - (o_z - o_ki) - N_SSM_HEADS), F32)], axis=1).astype(BF16)
    row = lambda v: v.reshape(1, -1)
    p = dict(
        norm_attn=row(norm_attn[lyr]), wb=wb, ws=ws, idx_k_norm=row(idx_k_norm[lyr]),
        conv_w=conv_w[lyr], conv_b=conv_b[lyr], dt_bias=dt_bias[lyr], a_log=a_log[lyr], d_skip=d_skip[lyr],
        ssm_norm=ssm_norm[lyr],
        woa=w_out[lyr, :ATT_WIDTH].astype(BF16), woy=w_out[lyr, ATT_WIDTH:].astype(BF16),
        norm_ffn=row(norm_ffn[lyr]), wg=w_gate[lyr].astype(BF16), wu=w_up[lyr].astype(BF16),
        wd=w_down[lyr].astype(BF16), norm_final=row(norm_final),
        tn=BIG_WIDTH // 4, th=512,
    )

    topk_p = min(TOPK_MAX, s // 4)
    assert SSD_CHUNK == LANES and s % LANES == 0
    tiles_p = jnp.swapaxes(_bias_tiles(rel_bias, LANES), -1, -2)
    tiles_p = tiles_p - tiles_p[0:1]

    def att_prompt(big3, qi_t, small3, small3_t, kin3, k4, v4):
        k16 = big3[:, :, COL_K:COL_K + ATT_WIDTH].astype(BF16)
        vt16 = jnp.swapaxes(big3[:, :, COL_V:COL_V + ATT_WIDTH].astype(BF16), -1, -2)
        return _attn_prompt(big3, qi_t, small3_t, kin3, k16, vt16, tiles_p, topk=topk_p, group=4)

    zero_conv = jnp.zeros((bp, CONV_W - 1, CONV_DIM), F32)
    zero_ssm = jnp.zeros((bp, N_SSM_HEADS, SSM_HEAD_DIM, SSM_STATE), F32)
    yp, kp, vp, kip, cp, sp = _layer(x_prompt, att_prompt, zero_conv, zero_ssm, SSD_CHUNK, SSD_CHUNK, p, tm=512)

    topk_s = min(TOPK_MAX, (past + t) // 4)
    tiles_s = jnp.repeat(_bias_tiles(rel_bias, t).reshape(3, N_ATT_HEADS * t, LANES), N_ATT_HEADS, axis=-1)

    def att_sample(big3, qi_t, small3, small3_t, kin3, k4, v4):
        return _attn_sample(page_table, big3, k4, v4, qi_t, small3, kin3, tiles_s,
                            cache_kidx, cache_k, cache_v, topk=topk_s)

    ys, ks, vs, kis, cs, ss = _layer(x_sample, att_sample, state_conv[lyr], state_ssm[lyr], SSD_CHUNK, t, p, tm=512)

    st = lambda a: a[None]
    return (yp, ys, st(kp), st(vp), st(kip), st(cp), st(sp), st(ks), st(vs), st(kis), st(cs), st(ss))
```

```python
import functools
import math

import numpy as np
import jax
import jax.numpy as jnp
from jax import lax
from jax.experimental import pallas as pl
from jax.experimental.pallas import tpu as pltpu

N_ATT_HEADS = 8
ATT_HEAD_DIM = 128
ATT_WIDTH = N_ATT_HEADS * ATT_HEAD_DIM
N_IDX_HEADS = 16
IDX_DIM = 64
TOPK_MAX = 256
N_SSM_HEADS = 16
SSM_HEAD_DIM = 64
SSM_WIDTH = N_SSM_HEADS * SSM_HEAD_DIM
N_SSM_GROUPS = 2
SSM_STATE = 128
CONV_W = 4
CONV_DIM = SSM_WIDTH + 2 * N_SSM_GROUPS * SSM_STATE
SSD_CHUNK = 128
PAGE_SIZE = 128
N_BUCKETS = 32
MAX_DISTANCE = 128
NORM_EPS = 1e-6

LANES = 128
SUBLANES = 8
VMEM_LIMIT = 56 * 1024 * 1024

COL_Q, COL_K, COL_V, COL_QI, COL_Z, COL_X = (i * ATT_WIDTH for i in range(6))
COL_B = COL_X + SSM_WIDTH
COL_C = COL_B + N_SSM_GROUPS * SSM_STATE
BIG_WIDTH = COL_C + N_SSM_GROUPS * SSM_STATE
SM_KI, SM_WI, SM_DT = 0, IDX_DIM, IDX_DIM + N_IDX_HEADS
SMALL_WIDTH = LANES

F32 = jnp.float32
BF16 = jnp.bfloat16

_INT_MIN = -(2 ** 31)
_INT_MAX = 2 ** 31 - 1
_KEY_NEG_INF = int(np.array([-np.inf], np.float32).view(np.int32)[0]) ^ 0x7FFFFFFF
_NEG_BIG = -1e30


def _dot(a, b):
    return jnp.dot(a, b, preferred_element_type=F32)


def _dot_nt(a, b):
    return lax.dot_general(a, b, (((1,), (1,)), ((), ())), preferred_element_type=F32)


def _dot_tn(a, b):
    return lax.dot_general(a, b, (((0,), (0,)), ((), ())), preferred_element_type=F32)


def _rms(x, g):
    return x * lax.rsqrt(jnp.mean(x * x, axis=-1, keepdims=True) + NORM_EPS) * g


def _silu(x):
    return x * (1.0 / (1.0 + jnp.exp(-x)))


def _inproj_body(x_ref, g_ref, wb_ref, ws_ref, gk_ref, big_ref, small_ref, kin_ref, u_scr):
    @pl.when(pl.program_id(1) == 0)
    def _():
        ub = _rms(x_ref[...], g_ref[...]).astype(BF16)
        u_scr[...] = ub
        sm = _dot(ub, ws_ref[...])
        small_ref[...] = sm
        kin_ref[...] = _rms(sm[:, SM_KI:SM_KI + IDX_DIM], gk_ref[...])

    big_ref[...] = _dot(u_scr[...], wb_ref[...])


def _inproj(x2, g, wb, ws, gk, *, tm, tn):
    m, d = x2.shape
    return pl.pallas_call(
        _inproj_body,
        grid=(m // tm, BIG_WIDTH // tn),
        in_specs=[
            pl.BlockSpec((tm, d), lambda i, j: (i, 0)),
            pl.BlockSpec((1, d), lambda i, j: (0, 0)),
            pl.BlockSpec((d, tn), lambda i, j: (0, j)),
            pl.BlockSpec((d, SMALL_WIDTH), lambda i, j: (0, 0)),
            pl.BlockSpec((1, IDX_DIM), lambda i, j: (0, 0)),
        ],
        out_specs=[
            pl.BlockSpec((tm, tn), lambda i, j: (i, j)),
            pl.BlockSpec((tm, SMALL_WIDTH), lambda i, j: (i, 0)),
            pl.BlockSpec((tm, IDX_DIM), lambda i, j: (i, 0)),
        ],
        out_shape=[
            jax.ShapeDtypeStruct((m, BIG_WIDTH), F32),
            jax.ShapeDtypeStruct((m, SMALL_WIDTH), F32),
            jax.ShapeDtypeStruct((m, IDX_DIM), F32),
        ],
        scratch_shapes=[pltpu.VMEM((tm, d), BF16)],
        compiler_params=pltpu.CompilerParams(
            dimension_semantics=("parallel", "arbitrary"), vmem_limit_bytes=VMEM_LIMIT),
        name="inproj",
    )(x2, g, wb, ws, gk)


def _sortable_key(s):
    s = jnp.where(s == 0.0, 0.0, s)
    b = pltpu.bitcast(s, jnp.int32)
    return b ^ ((b >> 31) & 0x7FFFFFFF)


def _count_keys(key_ref, nblk, pred, key_axis):
    def body(kb, acc):
        return acc + jnp.where(pred(key_ref[kb], kb), 1.0, 0.0)

    acc = jnp.zeros(key_ref.shape[1:], F32)
    if isinstance(nblk, int):
        for kb in range(nblk):
            acc = body(kb, acc)
    else:
        acc = lax.fori_loop(0, nblk, body, acc)
    return jnp.sum(acc, axis=key_axis, keepdims=True)


def _topk_select_params(key_ref, nblk, topk, key_axis):
    shape = key_ref.shape[1:]
    qshape = tuple(1 if a == key_axis else n for a, n in enumerate(shape))
    blk = shape[key_axis]
    kf = float(topk)

    def count_ge(cand):
        cb = jnp.broadcast_to(cand, shape)
        return _count_keys(key_ref, nblk, lambda k, kb: k >= cb, key_axis)

    res = jnp.where(count_ge(jnp.zeros(qshape, jnp.int32)) >= kf, 0, _INT_MIN).astype(jnp.int32)

    def bit_body(it, res):
        cand = res | jnp.left_shift(jnp.int32(1), 30 - it)
        return jnp.where(count_ge(cand) >= kf, cand, res)

    thr = lax.fori_loop(0, 31, bit_body, res)
    thr_b = jnp.broadcast_to(thr, shape)
    n_gt = _count_keys(key_ref, nblk, lambda k, kb: k > thr_b, key_axis)
    n_ge = count_ge(thr)
    tied = jnp.logical_and(n_ge > kf, thr > _KEY_NEG_INF)
    any_tied = jnp.max(jnp.where(tied, 1.0, 0.0)) > 0.0
    need = kf - n_gt
    kpos = lax.broadcasted_iota(jnp.int32, shape, key_axis)

    def tie_break():
        def body(it, cut):
            cand = cut | jnp.left_shift(jnp.int32(1), 30 - it)
            cand_b = jnp.broadcast_to(cand, shape)
            n = _count_keys(key_ref, nblk,
                            lambda k, kb: jnp.logical_and(k == thr_b, kpos + kb * blk < cand_b), key_axis)
            return jnp.where(n <= need, cand, cut)

        return lax.fori_loop(0, 31, body, jnp.zeros(qshape, jnp.int32))

    cut = lax.cond(any_tied, tie_break, lambda: jnp.full(qshape, _INT_MAX, jnp.int32))
    return thr, cut


def _attn_prompt_body(q_ref, qi_ref, smt_ref, kin_ref, k_ref, vt_ref, bias_ref, o_ref,
                      key_scr, mask_scr, lg_scr, *, topk, group):
    i = pl.program_id(1)
    q0 = i * LANES
    blk = (LANES, LANES)
    scale = ATT_HEAD_DIM ** -0.5
    nq = key_scr.shape[0]

    w_t = smt_ref[0, SM_WI:SM_WI + N_IDX_HEADS, :]
    qi = qi_ref[0].reshape(N_IDX_HEADS * LANES, IDX_DIM).astype(BF16)
    kpos = lax.broadcasted_iota(jnp.int32, blk, 0)
    qpos = lax.broadcasted_iota(jnp.int32, blk, 1) + q0
    qb = q_ref[0].astype(BF16)

    def tile(nb, first_near):
        for kb in range(nb):
            kin = kin_ref[0, kb * LANES:(kb + 1) * LANES, :].astype(BF16)
            s = _dot_nt(kin, qi)
            score = jnp.zeros(blk, F32)
            for h in range(N_IDX_HEADS):
                score = score + jnp.maximum(s[:, h * LANES:(h + 1) * LANES], 0.0) * w_t[h:h + 1, :]
            if kb >= first_near:
                score = jnp.where(kpos + kb * LANES <= qpos, score, -jnp.inf)
            key_scr[kb] = _sortable_key(score)

        thr, cut = _topk_select_params(key_scr, nb, topk, key_axis=0)
        thr_b = jnp.broadcast_to(thr, blk)
        cut_b = jnp.broadcast_to(cut, blk)
        for kb in range(nb):
            kp = kpos + kb * LANES
            key = key_scr[kb]
            valid = jnp.logical_or(key > thr_b, jnp.logical_and(key == thr_b, kp < cut_b))
            if kb >= first_near:
                valid = jnp.logical_and(valid, kp <= qpos)
            mask_scr[kb * LANES:(kb + 1) * LANES, :] = jnp.where(valid, 0.0, -jnp.inf)

        nk = nb * LANES
        for h in range(N_ATT_HEADS):
            hs = slice(h * ATT_HEAD_DIM, (h + 1) * ATT_HEAD_DIM)
            lg = _dot_nt(k_ref[0, 0:nk, hs], qb[:, hs]) * scale + mask_scr[0:nk, :]
            for kb in range(first_near, nb):
                tix = 2 - jnp.clip(i - kb, 0, 2)
                lg_scr[kb * LANES:(kb + 1) * LANES, :] = lg[kb * LANES:(kb + 1) * LANES, :] + bias_ref[tix, h]
            if first_near > 0:
                lg_scr[0:first_near * LANES, :] = lg[0:first_near * LANES, :]
            lg = lg_scr[0:nk, :]
            m = jnp.max(lg, axis=0, keepdims=True)
            p = jnp.exp(lg - m)
            l = jnp.sum(p, axis=0, keepdims=True)
            out_t = _dot(vt_ref[0, hs, 0:nk], p.astype(BF16)) / l
            o_ref[0, :, hs] = out_t.T.astype(o_ref.dtype)

    ngroups = -(-nq // group)
    for c in range(ngroups):
        nb = min((c + 1) * group, nq)

        @pl.when(i // group == c)
        def _(nb=nb, c=c):
            tile(nb, max(c * group - 1, 0))


def _attn_prompt(big3, qi_t, small3_t, kin3, k16, vt16, bias_tiles_t, *, topk, group):
    b, s, _ = big3.shape
    nq = s // LANES
    body = functools.partial(_attn_prompt_body, topk=topk, group=group)
    return pl.pallas_call(
        body,
        grid=(b, nq),
        in_specs=[
            pl.BlockSpec((1, LANES, ATT_WIDTH), lambda bi, i: (bi, i, COL_Q // ATT_WIDTH)),
            pl.BlockSpec((1, N_IDX_HEADS, LANES, IDX_DIM), lambda bi, i: (bi, 0, i, 0)),
            pl.BlockSpec((1, SMALL_WIDTH, LANES), lambda bi, i: (bi * nq + i, 0, 0)),
            pl.BlockSpec((1, s, IDX_DIM), lambda bi, i: (bi, 0, 0)),
            pl.BlockSpec((1, s, ATT_WIDTH), lambda bi, i: (bi, 0, 0)),
            pl.BlockSpec((1, ATT_WIDTH, s), lambda bi, i: (bi, 0, 0)),
            pl.BlockSpec(bias_tiles_t.shape, lambda bi, i: (0, 0, 0, 0)),
        ],
        out_specs=pl.BlockSpec((1, LANES, ATT_WIDTH), lambda bi, i: (bi, i, 0)),
        out_shape=jax.ShapeDtypeStruct((b, s, ATT_WIDTH), BF16),
        scratch_shapes=[
            pltpu.VMEM((nq, LANES, LANES), jnp.int32),
            pltpu.VMEM((s, LANES), F32),
            pltpu.VMEM((s, LANES), F32),
        ],
        compiler_params=pltpu.CompilerParams(
            dimension_semantics=("parallel", "arbitrary"), vmem_limit_bytes=VMEM_LIMIT),
        name="attn_prompt",
    )(big3, qi_t, small3_t, kin3, k16, vt16, bias_tiles_t)


def _select_sample_body(pt_ref, qi_ref, sm_ref, kinew_ref, *rest, n_pages, t, topk):
    kidx_refs = rest[:n_pages]
    sel_ref = rest[n_pages]
    key_scr, wb_scr = rest[n_pages + 1:]
    nblk = n_pages + 1
    j = pl.program_id(1)
    r0 = pl.multiple_of(j * t, t)

    wi = sm_ref[0, :, SM_WI:SM_WI + N_IDX_HEADS]
    for h in range(N_IDX_HEADS):
        wb_scr[h] = jnp.broadcast_to(wi[:, h:h + 1], (t, LANES))
    qi = qi_ref[0].reshape(N_IDX_HEADS * t, IDX_DIM).astype(BF16)

    def scores(s):
        return jnp.sum(jnp.maximum(s.reshape(N_IDX_HEADS, t, LANES), 0.0) * wb_scr[...], axis=0)

    for p in range(n_pages):
        key_scr[p, pl.ds(r0, t), :] = _sortable_key(scores(_dot(qi, kidx_refs[p][0, 0].astype(BF16))))
    kin_new = jnp.concatenate([kinew_ref[0], jnp.zeros((LANES - t, IDX_DIM), F32)], axis=0)
    trow = lax.broadcasted_iota(jnp.int32, (t, LANES), 0)
    lane_t = lax.broadcasted_iota(jnp.int32, (t, LANES), 1)
    s_new = jnp.where(lane_t <= trow, scores(_dot_nt(qi, kin_new.astype(BF16))), -jnp.inf)
    key_scr[n_pages, pl.ds(r0, t), :] = _sortable_key(s_new)

    @pl.when(j == pl.num_programs(1) - 1)
    def _():
        rows = key_scr.shape[1]
        thr, cut = _topk_select_params(key_scr, nblk, topk, key_axis=1)
        thr_b = jnp.broadcast_to(thr, (rows, LANES))
        cut_b = jnp.broadcast_to(cut, (rows, LANES))
        lane = lax.broadcasted_iota(jnp.int32, (rows, LANES), 1)
        tq = lax.broadcasted_iota(jnp.int32, (rows, LANES), 0) & (t - 1)
        for p in range(nblk):
            key = key_scr[p]
            sel = jnp.logical_or(key > thr_b, jnp.logical_and(key == thr_b, lane + p * LANES < cut_b))
            if p == n_pages:
                sel = jnp.logical_and(sel, lane <= tq)
            sel_ref[0, p] = jnp.where(sel, 1.0, 0.0)


def _select_sample(page_table, qi_t, small3, kin3, cache_kidx_t, *, topk, gs):
    b, t, _ = small3.shape
    n_pages = page_table.shape[1]
    assert b % gs == 0 and t & (t - 1) == 0
    body = functools.partial(_select_sample_body, n_pages=n_pages, t=t, topk=topk)
    seq = lambda g, j: g * gs + j
    in_specs = [
        pl.BlockSpec((1, N_IDX_HEADS, t, IDX_DIM), lambda g, j, pt: (seq(g, j), 0, 0, 0)),
        pl.BlockSpec((1, t, SMALL_WIDTH), lambda g, j, pt: (seq(g, j), 0, 0)),
        pl.BlockSpec((1, t, IDX_DIM), lambda g, j, pt: (seq(g, j), 0, 0)),
    ]
    in_specs += [pl.BlockSpec((1, 1, IDX_DIM, PAGE_SIZE), lambda g, j, pt, p=p: (0, pt[seq(g, j), p], 0, 0))
                 for p in range(n_pages)]
    sel_shape = (b // gs, n_pages + 1, gs * t, LANES)
    return pl.pallas_call(
        body,
        grid_spec=pltpu.PrefetchScalarGridSpec(
            num_scalar_prefetch=1,
            grid=(b // gs, gs),
            in_specs=in_specs,
            out_specs=pl.BlockSpec((1,) + sel_shape[1:], lambda g, j, pt: (g, 0, 0, 0)),
            scratch_shapes=[
                pltpu.VMEM(sel_shape[1:], jnp.int32),
                pltpu.VMEM((N_IDX_HEADS, t, LANES), F32),
            ],
        ),
        out_shape=jax.ShapeDtypeStruct(sel_shape, F32),
        compiler_params=pltpu.CompilerParams(
            dimension_semantics=("parallel", "arbitrary"), vmem_limit_bytes=VMEM_LIMIT),
        name="select_sample",
    )(page_table, qi_t, small3, kin3, *([cache_kidx_t] * n_pages))


def _attn_sample_body(pt_ref, q_ref, knew_ref, vnew_ref, sel_ref, bias_ref, *rest, n_pages, t):
    k_refs = rest[:n_pages]
    v_refs = rest[n_pages:2 * n_pages]
    o_ref = rest[2 * n_pages]
    (lg_scr,) = rest[2 * n_pages + 1:]
    nblk = n_pages + 1
    rows = N_ATT_HEADS * t
    flat = PAGE_SIZE * N_ATT_HEADS
    scale = ATT_HEAD_DIM ** -0.5

    q = q_ref[0]
    q64 = jnp.concatenate([q[:, h * ATT_HEAD_DIM:(h + 1) * ATT_HEAD_DIM] for h in range(N_ATT_HEADS)],
                          axis=0).astype(BF16)
    rr = lax.broadcasted_iota(jnp.int32, (rows, flat), 0)
    cc = lax.broadcasted_iota(jnp.int32, (rows, flat), 1)
    head_match = (cc & (N_ATT_HEADS - 1)) == (rr >> (t.bit_length() - 1))
    er = lax.broadcasted_iota(jnp.int32, (LANES, flat), 0)
    ec = lax.broadcasted_iota(jnp.int32, (LANES, flat), 1)
    expand = jnp.where((ec >> (N_ATT_HEADS.bit_length() - 1)) == er, 1.0, 0.0).astype(BF16)

    def flat_kv(refs, p):
        if p < n_pages:
            return refs[p][0, 0].reshape(flat, ATT_HEAD_DIM).astype(BF16), flat
        new = refs[p][0].reshape(t * N_ATT_HEADS, ATT_HEAD_DIM)
        pad = jnp.zeros((LANES - t * N_ATT_HEADS, ATT_HEAD_DIM), F32)
        return jnp.concatenate([new, pad], axis=0).astype(BF16), LANES

    k_all = list(k_refs) + [knew_ref]
    v_all = list(v_refs) + [vnew_ref]
    m = jnp.full((rows, 1), _NEG_BIG, F32)
    for p in range(nblk):
        kflat, width = flat_kv(k_all, p)
        selx = _dot(sel_ref[0, p].astype(BF16), expand[:, :width])
        valid = jnp.logical_and(head_match[:, :width], jnp.concatenate([selx] * N_ATT_HEADS, axis=0) > 0.5)
        tix = 2 - min(n_pages - p, 2)
        lg = _dot_nt(q64, kflat) * scale + bias_ref[tix, :, :width]
        lg = jnp.where(valid, lg, _NEG_BIG)
        lg_scr[p, :, :width] = lg
        m = jnp.maximum(m, jnp.max(lg, axis=-1, keepdims=True))

    l = jnp.zeros((rows, 1), F32)
    acc = jnp.zeros((rows, ATT_HEAD_DIM), F32)
    for p in range(nblk):
        vflat, width = flat_kv(v_all, p)
        pr = jnp.exp(lg_scr[p, :, :width] - m)
        l = l + jnp.sum(pr, axis=-1, keepdims=True)
        acc = acc + _dot(pr.astype(BF16), vflat)
    out = acc / l
    for h in range(N_ATT_HEADS):
        o_ref[0, :, h * ATT_HEAD_DIM:(h + 1) * ATT_HEAD_DIM] = out[h * t:(h + 1) * t, :].astype(o_ref.dtype)


def _attn_sample(page_table, big3, k_new, v_new, sel, bias_rows, cache_k, cache_v, *, gs):
    b, t, _ = big3.shape
    n_pages = page_table.shape[1]
    assert t == SUBLANES and N_ATT_HEADS & (N_ATT_HEADS - 1) == 0 and t * N_ATT_HEADS <= LANES
    body = functools.partial(_attn_sample_body, n_pages=n_pages, t=t)
    kv_block = (1, 1, PAGE_SIZE, N_ATT_HEADS, ATT_HEAD_DIM)
    new_block = (1, t, N_ATT_HEADS, ATT_HEAD_DIM)

    in_specs = [
        pl.BlockSpec((1, t, ATT_WIDTH), lambda bi, pt: (bi, 0, COL_Q // ATT_WIDTH)),
        pl.BlockSpec(new_block, lambda bi, pt: (bi, 0, 0, 0)),
        pl.BlockSpec(new_block, lambda bi, pt: (bi, 0, 0, 0)),
        pl.BlockSpec((1, n_pages + 1, t, LANES), lambda bi, pt: (bi // gs, 0, bi % gs, 0)),
        pl.BlockSpec(bias_rows.shape, lambda bi, pt: (0, 0, 0)),
    ]
    in_specs += [pl.BlockSpec(kv_block, lambda bi, pt, p=p: (0, pt[bi, p], 0, 0, 0)) for p in range(n_pages)] * 2
    rows = N_ATT_HEADS * t
    return pl.pallas_call(
        body,
        grid_spec=pltpu.PrefetchScalarGridSpec(
            num_scalar_prefetch=1,
            grid=(b,),
            in_specs=in_specs,
            out_specs=pl.BlockSpec((1, t, ATT_WIDTH), lambda bi, pt: (bi, 0, 0)),
            scratch_shapes=[pltpu.VMEM((n_pages + 1, rows, PAGE_SIZE * N_ATT_HEADS), F32)],
        ),
        out_shape=jax.ShapeDtypeStruct((b, t, ATT_WIDTH), BF16),
        compiler_params=pltpu.CompilerParams(
            dimension_semantics=("arbitrary",), vmem_limit_bytes=VMEM_LIMIT),
        name="attn_sample",
    )(page_table, big3, k_new, v_new, sel, bias_rows, *([cache_k] * n_pages), *([cache_v] * n_pages))


def _split3(a):
    a1 = a.astype(BF16)
    r1 = a - a1.astype(F32)
    a2 = r1.astype(BF16)
    a3 = (r1 - a2.astype(F32)).astype(BF16)
    return a1, a2, a3


def _ssd_body(x_ref, bm_ref, cm_ref, z_ref, sm_ref, smt_ref, pre_ref, h0_ref,
              cw_ref, cb_ref, dtb_ref, dtbt_ref, alog_ref, alogt_ref, dskip_ref, gn_ref,
              y_ref, tail_ref, hout_ref, xp_scr, h_scr, y_scr, *, l, l_valid):
    c = pl.program_id(1)
    nc = pl.num_programs(1)
    off = SUBLANES

    @pl.when(c == 0)
    def _():
        xp_scr[off - (CONV_W - 1):off, :] = pre_ref[0]
        h_scr[...] = h0_ref[0]

    if l_valid < l:
        xp_scr[off + l_valid:off + l, :] = jnp.zeros((l - l_valid, CONV_DIM), F32)
    xp_scr[off:off + l_valid, 0:SSM_WIDTH] = x_ref[0]
    xp_scr[off:off + l_valid, SSM_WIDTH:COL_C - COL_X] = bm_ref[0]
    xp_scr[off:off + l_valid, COL_C - COL_X:CONV_DIM] = cm_ref[0]

    conv = cb_ref[...]
    for j in range(CONV_W):
        s0 = off - (CONV_W - 1) + j
        conv = conv + xp_scr[s0:s0 + l, :] * cw_ref[j:j + 1, :]
    xc = _silu(conv)
    tail = xp_scr[off + l_valid - (CONV_W - 1):off + l_valid, :]
    tail_ref[0] = tail
    xp_scr[off - (CONV_W - 1):off, :] = tail

    xs = xc[:, :SSM_WIDTH]
    bc = [xc[:, SSM_WIDTH + g * SSM_STATE:SSM_WIDTH + (g + 1) * SSM_STATE].astype(BF16)
          for g in range(N_SSM_GROUPS)]
    cc = [xc[:, SSM_WIDTH + (N_SSM_GROUPS + g) * SSM_STATE:SSM_WIDTH + (N_SSM_GROUPS + g + 1) * SSM_STATE]
          .astype(BF16) for g in range(N_SSM_GROUPS)]

    def softplus(v):
        return jnp.maximum(v, 0.0) + jnp.log1p(jnp.exp(-jnp.abs(v)))

    dt_raw = sm_ref[0, :, SM_DT:SM_DT + N_SSM_HEADS]
    dtt_raw = smt_ref[0, SM_DT:SM_DT + N_SSM_HEADS, :]
    if l_valid < l:
        dt_raw = jnp.concatenate([dt_raw, jnp.zeros((l - l_valid, N_SSM_HEADS), F32)], axis=0)
        dtt_raw = jnp.concatenate([dtt_raw, jnp.zeros((N_SSM_HEADS, l - l_valid), F32)], axis=1)
    dt = softplus(dt_raw + dtb_ref[...])
    dtt = softplus(dtt_raw + dtbt_ref[...])
    if l_valid < l:
        dt = jnp.where(lax.broadcasted_iota(jnp.int32, dt.shape, 0) < l_valid, dt, 0.0)
        dtt = jnp.where(lax.broadcasted_iota(jnp.int32, dtt.shape, 1) < l_valid, dtt, 0.0)
    a = dt * (-jnp.exp(alog_ref[...]))
    at = dtt * (-jnp.exp(alogt_ref[...]))

    ri = lax.broadcasted_iota(jnp.int32, (l, l), 0)
    ci = lax.broadcasted_iota(jnp.int32, (l, l), 1)
    causal = ri >= ci
    tri = jnp.where(causal, 1.0, 0.0).astype(BF16)
    tri_u = jnp.where(ri <= ci, 1.0, 0.0).astype(BF16)
    acum = sum(_dot(tri, piece) for piece in _split3(a))
    acum_t = sum(_dot(piece, tri_u) for piece in _split3(at))

    cb = [_dot_nt(cc[g], bc[g]) for g in range(N_SSM_GROUPS)]
    heads_per_group = N_SSM_HEADS // N_SSM_GROUPS
    for h in range(N_SSM_HEADS):
        g = h // heads_per_group
        hs = slice(h * SSM_HEAD_DIM, (h + 1) * SSM_HEAD_DIM)
        col = acum[:, h:h + 1]
        rowv = acum_t[h:h + 1, :]
        last = acum[l - 1:l, h:h + 1]
        decay = jnp.exp(jnp.where(causal, col - rowv, -jnp.inf))
        mmat = cb[g] * decay * dtt[h:h + 1, :]
        xh = xs[:, hs]
        hst = h_scr[h]
        y_intra = _dot(mmat.astype(BF16), xh.astype(BF16))
        y_inter = _dot_nt(cc[g], hst.astype(BF16)) * jnp.exp(col)
        wts = jnp.exp(last - col) * dt[:, h:h + 1]
        xw = (xh * wts).astype(BF16)
        h_scr[h] = hst * jnp.exp(last) + _dot_tn(xw, bc[g])
        y_scr[:, hs] = y_intra + y_inter + dskip_ref[h] * xh

    y = y_scr[0:l_valid, :]
    y_ref[0] = _rms(y * _silu(z_ref[0]), gn_ref[...]).astype(y_ref.dtype)

    @pl.when(c == nc - 1)
    def _():
        hout_ref[0] = h_scr[...]


def _ssd(big3, small3, small3_t, prefix, h0, conv_w, conv_b, dt_bias, a_log, d_skip, ssm_norm, *, l, l_valid):
    b, t, _ = big3.shape
    nc = t // l_valid
    body = functools.partial(_ssd_body, l=l, l_valid=l_valid)
    gw = N_SSM_GROUPS * SSM_STATE
    row = lambda v: v.reshape(1, -1)
    colv = lambda v: v.reshape(-1, 1)
    const2 = lambda shape: pl.BlockSpec(shape, lambda bi, c: (0, 0))
    return pl.pallas_call(
        body,
        grid=(b, nc),
        in_specs=[
            pl.BlockSpec((1, l_valid, SSM_WIDTH), lambda bi, c: (bi, c, COL_X // SSM_WIDTH)),
            pl.BlockSpec((1, l_valid, gw), lambda bi, c: (bi, c, COL_B // gw)),
            pl.BlockSpec((1, l_valid, gw), lambda bi, c: (bi, c, COL_C // gw)),
            pl.BlockSpec((1, l_valid, SSM_WIDTH), lambda bi, c: (bi, c, COL_Z // SSM_WIDTH)),
            pl.BlockSpec((1, l_valid, SMALL_WIDTH), lambda bi, c: (bi, c, 0)),
            pl.BlockSpec((1, SMALL_WIDTH, l_valid), lambda bi, c: (bi * nc + c, 0, 0)),
            pl.BlockSpec((1, CONV_W - 1, CONV_DIM), lambda bi, c: (bi, 0, 0)),
            pl.BlockSpec((1, N_SSM_HEADS, SSM_HEAD_DIM, SSM_STATE), lambda bi, c: (bi, 0, 0, 0)),
            const2((CONV_W, CONV_DIM)),
            const2((1, CONV_DIM)),
            const2((1, N_SSM_HEADS)),
            const2((N_SSM_HEADS, 1)),
            const2((1, N_SSM_HEADS)),
            const2((N_SSM_HEADS, 1)),
            pl.BlockSpec(memory_space=pltpu.SMEM),
            const2((1, SSM_WIDTH)),
        ],
        out_specs=[
            pl.BlockSpec((1, l_valid, SSM_WIDTH), lambda bi, c: (bi, c, 0)),
            pl.BlockSpec((1, CONV_W - 1, CONV_DIM), lambda bi, c: (bi, 0, 0)),
            pl.BlockSpec((1, N_SSM_HEADS, SSM_HEAD_DIM, SSM_STATE), lambda bi, c: (bi, 0, 0, 0)),
        ],
        out_shape=[
            jax.ShapeDtypeStruct((b, t, SSM_WIDTH), BF16),
            jax.ShapeDtypeStruct((b, CONV_W - 1, CONV_DIM), F32),
            jax.ShapeDtypeStruct((b, N_SSM_HEADS, SSM_HEAD_DIM, SSM_STATE), F32),
        ],
        scratch_shapes=[
            pltpu.VMEM((SUBLANES + l, CONV_DIM), F32),
            pltpu.VMEM((N_SSM_HEADS, SSM_HEAD_DIM, SSM_STATE), F32),
            pltpu.VMEM((l, SSM_WIDTH), F32),
        ],
        compiler_params=pltpu.CompilerParams(
            dimension_semantics=("parallel", "arbitrary"), vmem_limit_bytes=VMEM_LIMIT),
        name="ssd",
    )(big3, big3, big3, big3, small3, small3_t, prefix, h0,
      conv_w, row(conv_b), row(dt_bias), colv(dt_bias), row(a_log), colv(a_log), d_skip, row(ssm_norm))


def _ssd_sample_body(x_ref, bm_ref, cm_ref, z_ref, sm_ref, smt_ref, pre_ref, h0_ref,
                     cw_ref, cb_ref, dtb_ref, dtbt_ref, alog_ref, alogt_ref, dskip_ref, gn_ref,
                     y_ref, tail_ref, hout_ref, xp_scr, xc_scr, yi_scr, xw_scr, y_scr, *, gs, t):
    rows = gs * t
    off = SUBLANES
    tshift = t.bit_length() - 1
    gw = N_SSM_GROUPS * SSM_STATE
    heads_per_group = N_SSM_HEADS // N_SSM_GROUPS
    gcols = heads_per_group * SSM_HEAD_DIM

    for s in range(gs):
        xp_scr[s, off - (CONV_W - 1):off, :] = pre_ref[s]
        xp_scr[s, off:off + t, 0:SSM_WIDTH] = x_ref[s]
        xp_scr[s, off:off + t, SSM_WIDTH:SSM_WIDTH + gw] = bm_ref[s]
        xp_scr[s, off:off + t, SSM_WIDTH + gw:CONV_DIM] = cm_ref[s]
        conv = cb_ref[...]
        for j in range(CONV_W):
            s0 = off - (CONV_W - 1) + j
            conv = conv + xp_scr[s, s0:s0 + t, :] * cw_ref[j:j + 1, :]
        xc_scr[s * t:(s + 1) * t, :] = _silu(conv)
        tail_ref[s] = xp_scr[s, off + t - (CONV_W - 1):off + t, :]

    xc = xc_scr[...]
    xs = xc[:, :SSM_WIDTH]
    bcf = [xc[:, SSM_WIDTH + g * SSM_STATE:SSM_WIDTH + (g + 1) * SSM_STATE] for g in range(N_SSM_GROUPS)]
    ccf = [xc[:, SSM_WIDTH + gw + g * SSM_STATE:SSM_WIDTH + gw + (g + 1) * SSM_STATE] for g in range(N_SSM_GROUPS)]

    def softplus(v):
        return jnp.maximum(v, 0.0) + jnp.log1p(jnp.exp(-jnp.abs(v)))

    dt = softplus(sm_ref[...].reshape(rows, SMALL_WIDTH)[:, SM_DT:SM_DT + N_SSM_HEADS] + dtb_ref[...])
    dtt = softplus(smt_ref[0, SM_DT:SM_DT + N_SSM_HEADS, :] + dtbt_ref[...])
    a = dt * (-jnp.exp(alog_ref[...]))
    at = dtt * (-jnp.exp(alogt_ref[...]))

    ri = lax.broadcasted_iota(jnp.int32, (rows, rows), 0)
    ci = lax.broadcasted_iota(jnp.int32, (rows, rows), 1)
    same = (ri >> tshift) == (ci >> tshift)
    causal = jnp.logical_and(same, ri >= ci)
    tri = jnp.where(causal, 1.0, 0.0).astype(BF16)
    tri_u = jnp.where(jnp.logical_and(same, ri <= ci), 1.0, 0.0).astype(BF16)
    pick_last = jnp.where(ci == (((ri >> tshift) << tshift) + (t - 1)), 1.0, 0.0).astype(BF16)
    acum = sum(_dot(tri, piece) for piece in _split3(a))
    acum_t = sum(_dot(piece, tri_u) for piece in _split3(at))
    last = sum(_dot(pick_last, piece) for piece in _split3(acum))
    wts = jnp.exp(last - acum) * dt
    exp_last = jnp.exp(last)

    for s in range(gs):
        rs = slice(s * t, (s + 1) * t)
        for g in range(N_SSM_GROUPS):
            h0g = h0_ref[s, g * heads_per_group:(g + 1) * heads_per_group].reshape(gcols, SSM_STATE)
            yi_scr[rs, g * gcols:(g + 1) * gcols] = _dot_nt(ccf[g][rs, :], h0g)

    cb = [_dot_nt(ccf[g].astype(BF16), bcf[g].astype(BF16)) for g in range(N_SSM_GROUPS)]
    for h in range(N_SSM_HEADS):
        g = h // heads_per_group
        hs = slice(h * SSM_HEAD_DIM, (h + 1) * SSM_HEAD_DIM)
        col = acum[:, h:h + 1]
        decay = jnp.exp(jnp.where(causal, col - acum_t[h:h + 1, :], -jnp.inf))
        mmat = cb[g] * decay * dtt[h:h + 1, :]
        xh = xs[:, hs]
        y_intra = _dot(mmat.astype(BF16), xh.astype(BF16))
        y_scr[:, hs] = y_intra + yi_scr[:, hs] * jnp.exp(col) + dskip_ref[h] * xh
        xw_scr[:, hs] = xh * wts[:, h:h + 1]

    out = _rms(y_scr[...] * _silu(z_ref[...].reshape(rows, SSM_WIDTH)), gn_ref[...])
    for s in range(gs):
        y_ref[s] = out[s * t:(s + 1) * t, :].astype(y_ref.dtype)

    for s in range(gs):
        rs = slice(s * t, (s + 1) * t)
        for g in range(N_SSM_GROUPS):
            upd = _dot_tn(xw_scr[rs, g * gcols:(g + 1) * gcols], bcf[g][rs, :])
            for hh in range(heads_per_group):
                h = g * heads_per_group + hh
                scale = jnp.broadcast_to(exp_last[s * t:s * t + 1, h:h + 1], (SSM_HEAD_DIM, SSM_STATE))
                hout_ref[s, h] = h0_ref[s, h] * scale + upd[hh * SSM_HEAD_DIM:(hh + 1) * SSM_HEAD_DIM, :]


def _ssd_sample(big3, small3, small3_t, prefix, h0, conv_w, conv_b, dt_bias, a_log, d_skip, ssm_norm, *, gs):
    b, t, _ = big3.shape
    assert b % gs == 0 and t & (t - 1) == 0 and t == SUBLANES
    rows = gs * t
    body = functools.partial(_ssd_sample_body, gs=gs, t=t)
    gw = N_SSM_GROUPS * SSM_STATE
    row = lambda v: v.reshape(1, -1)
    colv = lambda v: v.reshape(-1, 1)
    const2 = lambda shape: pl.BlockSpec(shape, lambda gi: (0, 0))
    state_block = (gs, N_SSM_HEADS, SSM_HEAD_DIM, SSM_STATE)
    return pl.pallas_call(
        body,
        grid=(b // gs,),
        in_specs=[
            pl.BlockSpec((gs, t, SSM_WIDTH), lambda gi: (gi, 0, COL_X // SSM_WIDTH)),
            pl.BlockSpec((gs, t, gw), lambda gi: (gi, 0, COL_B // gw)),
            pl.BlockSpec((gs, t, gw), lambda gi: (gi, 0, COL_C // gw)),
            pl.BlockSpec((gs, t, SSM_WIDTH), lambda gi: (gi, 0, COL_Z // SSM_WIDTH)),
            pl.BlockSpec((gs, t, SMALL_WIDTH), lambda gi: (gi, 0, 0)),
            pl.BlockSpec((1, SMALL_WIDTH, rows), lambda gi: (gi, 0, 0)),
            pl.BlockSpec((gs, CONV_W - 1, CONV_DIM), lambda gi: (gi, 0, 0)),
            pl.BlockSpec(state_block, lambda gi: (gi, 0, 0, 0)),
            const2((CONV_W, CONV_DIM)),
            const2((1, CONV_DIM)),
            const2((1, N_SSM_HEADS)),
            const2((N_SSM_HEADS, 1)),
            const2((1, N_SSM_HEADS)),
            const2((N_SSM_HEADS, 1)),
            pl.BlockSpec(memory_space=pltpu.SMEM),
            const2((1, SSM_WIDTH)),
        ],
        out_specs=[
            pl.BlockSpec((gs, t, SSM_WIDTH), lambda gi: (gi, 0, 0)),
            pl.BlockSpec((gs, CONV_W - 1, CONV_DIM), lambda gi: (gi, 0, 0)),
            pl.BlockSpec(state_block, lambda gi: (gi, 0, 0, 0)),
        ],
        out_shape=[
            jax.ShapeDtypeStruct((b, t, SSM_WIDTH), BF16),
            jax.ShapeDtypeStruct((b, CONV_W - 1, CONV_DIM), F32),
            jax.ShapeDtypeStruct((b, N_SSM_HEADS, SSM_HEAD_DIM, SSM_STATE), F32),
        ],
        scratch_shapes=[
            pltpu.VMEM((gs, SUBLANES + t, CONV_DIM), F32),
            pltpu.VMEM((rows, CONV_DIM), F32),
            pltpu.VMEM((rows, SSM_WIDTH), F32),
            pltpu.VMEM((rows, SSM_WIDTH), F32),
            pltpu.VMEM((rows, SSM_WIDTH), F32),
        ],
        compiler_params=pltpu.CompilerParams(
            dimension_semantics=("parallel",), vmem_limit_bytes=VMEM_LIMIT),
        name="ssd_sample",
    )(big3, big3, big3, big3, small3, small3_t, prefix, h0,
      conv_w, row(conv_b), row(dt_bias), colv(dt_bias), row(a_log), colv(a_log), d_skip, row(ssm_norm))


def _ffn_body(x_ref, att_ref, y_ref, woa_ref, woy_ref, gf_ref, wg_ref, wu_ref, wd_ref, gl_ref,
              o_ref, f_scr):
    j = pl.program_id(1)

    @pl.when(j == 0)
    def _():
        h = x_ref[...] + _dot(att_ref[...], woa_ref[...]) + _dot(y_ref[...], woy_ref[...])
        o_ref[...] = h
        f_scr[...] = _rms(h, gf_ref[...]).astype(BF16)

    f = f_scr[...]
    act = _silu(_dot(f, wg_ref[...])) * _dot(f, wu_ref[...])
    o_ref[...] += _dot(act.astype(BF16), wd_ref[...])

    @pl.when(j == pl.num_programs(1) - 1)
    def _():
        o_ref[...] = _rms(o_ref[...], gl_ref[...])


def _out_ffn(x2, att2, y2, woa, woy, gf, wg, wu, wd, gl, *, tm, th):
    m, d = x2.shape
    hidden = wg.shape[1]
    resident = dict(pipeline_mode=pl.Buffered(1))
    return pl.pallas_call(
        _ffn_body,
        grid=(m // tm, hidden // th),
        in_specs=[
            pl.BlockSpec((tm, d), lambda i, j: (i, 0)),
            pl.BlockSpec((tm, ATT_WIDTH), lambda i, j: (i, 0)),
            pl.BlockSpec((tm, SSM_WIDTH), lambda i, j: (i, 0)),
            pl.BlockSpec((ATT_WIDTH, d), lambda i, j: (0, 0), **resident),
            pl.BlockSpec((SSM_WIDTH, d), lambda i, j: (0, 0), **resident),
            pl.BlockSpec((1, d), lambda i, j: (0, 0)),
            pl.BlockSpec((d, th), lambda i, j: (0, j)),
            pl.BlockSpec((d, th), lambda i, j: (0, j)),
            pl.BlockSpec((th, d), lambda i, j: (j, 0)),
            pl.BlockSpec((1, d), lambda i, j: (0, 0)),
        ],
        out_specs=pl.BlockSpec((tm, d), lambda i, j: (i, 0)),
        out_shape=jax.ShapeDtypeStruct((m, d), F32),
        scratch_shapes=[pltpu.VMEM((tm, d), BF16)],
        compiler_params=pltpu.CompilerParams(
            dimension_semantics=("parallel", "arbitrary"), vmem_limit_bytes=VMEM_LIMIT),
        name="out_ffn",
    )(x2, att2, y2, woa, woy, gf, wg, wu, wd, gl)


def _t5_bucket(rel):
    max_exact = N_BUCKETS // 2
    relf = jnp.maximum(rel, 1).astype(jnp.float32)
    large = max_exact + (jnp.log(relf / max_exact) / math.log(MAX_DISTANCE / max_exact)
                         * (N_BUCKETS - max_exact)).astype(jnp.int32)
    large = jnp.minimum(large, N_BUCKETS - 1)
    return jnp.where(rel < max_exact, rel, large)


def _toeplitz(u, rows, cols):
    nh, period = u.shape
    assert cols < period
    flat = jnp.tile(u, (1, rows))
    return flat[:, :rows * (period - 1)].reshape(nh, rows, period - 1)[:, :, :cols]


def _bias_tiles(rel_bias, rows, key_major):
    assert MAX_DISTANCE <= LANES and rows <= LANES
    table = rel_bias[_t5_bucket(jnp.arange(3 * LANES, dtype=jnp.int32))].T
    tiles = []
    for dd in (2, 1, 0):
        neg = table[:, (dd - 1) * LANES:dd * LANES] if dd >= 1 else jnp.tile(table[:, 0:1], (1, LANES))
        u = jnp.concatenate([table[:, dd * LANES:(dd + 1) * LANES], neg], axis=1)
        if key_major:
            tiles.append(_toeplitz(u, LANES, rows))
        else:
            w = jnp.concatenate([u[:, 0:1], jnp.flip(u[:, 1:], axis=1)], axis=1)
            tiles.append(_toeplitz(w, rows, LANES))
    return jnp.stack(tiles)


def _layer(x, att_call, ssd_call, blk_rows, p, *, tm):
    b, t, d = x.shape
    m = b * t
    tm = min(tm, m)
    big, small, kin = _inproj(x.reshape(m, d), p["norm_attn"], p["wb"], p["ws"], p["idx_k_norm"], tm=tm, tn=p["tn"])
    big3 = big.reshape(b, t, BIG_WIDTH)
    small3 = small.reshape(b, t, SMALL_WIDTH)
    kin3 = kin.reshape(b, t, IDX_DIM)
    hd = (b, t, N_ATT_HEADS, ATT_HEAD_DIM)
    k4 = big3[:, :, COL_K:COL_K + ATT_WIDTH].reshape(hd)
    v4 = big3[:, :, COL_V:COL_V + ATT_WIDTH].reshape(hd)
    qi_t = jnp.transpose(big3[:, :, COL_QI:COL_QI + N_IDX_HEADS * IDX_DIM].reshape(b, t, N_IDX_HEADS, IDX_DIM),
                         (0, 2, 1, 3))
    small3_t = jnp.transpose(small.reshape(m // blk_rows, blk_rows, SMALL_WIDTH), (0, 2, 1))
    att = att_call(big3, qi_t, small3, small3_t, kin3, k4, v4)
    y, tail, h_fin = ssd_call(big3, small3, small3_t)
    out = _out_ffn(x.reshape(m, d), att.reshape(m, ATT_WIDTH), y.reshape(m, SSM_WIDTH), p["woa"], p["woy"],
                   p["norm_ffn"], p["wg"], p["wu"], p["wd"], p["norm_final"], tm=tm, th=p["th"])
    return out.reshape(b, t, d), k4, v4, kin3, tail, h_fin


def kernel(x_prompt, x_sample, cache_k, cache_v, cache_kidx, state_conv, state_ssm, page_table, rel_bias,
           norm_attn, w_in, idx_k_norm, conv_w, conv_b, dt_bias, a_log, d_skip, ssm_norm, w_out, norm_ffn,
           w_gate, w_up, w_down, norm_final):
    depth = w_in.shape[0]
    assert depth == 1
    bp, s, d = x_prompt.shape
    bs, t, _ = x_sample.shape
    n_pages = page_table.shape[1]
    past = n_pages * PAGE_SIZE
    lyr = 0

    w = w_in[lyr]
    o_ki = 3 * ATT_WIDTH + N_IDX_HEADS * IDX_DIM
    o_z = o_ki + IDX_DIM + N_IDX_HEADS
    o_dt = o_z + SSM_WIDTH + CONV_DIM
    wb = jnp.concatenate([w[:, :o_ki], w[:, o_z:o_dt]], axis=1).astype(BF16)
    ws = jnp.concatenate([w[:, o_ki:o_z], w[:, o_dt:],
                          jnp.zeros((d, SMALL_WIDTH - (o_z - o_ki) - N_SSM_HEADS), F32)], axis=1).astype(BF16)
    row = lambda v: v.reshape(1, -1)
    p = dict(
        norm_attn=row(norm_attn[lyr]), wb=wb, ws=ws, idx_k_norm=row(idx_k_norm[lyr]),
        conv_w=conv_w[lyr], conv_b=conv_b[lyr], dt_bias=dt_bias[lyr], a_log=a_log[lyr], d_skip=d_skip[lyr],
        ssm_norm=ssm_norm[lyr],
        woa=w_out[lyr, :ATT_WIDTH].astype(BF16), woy=w_out[lyr, ATT_WIDTH:].astype(BF16),
        norm_ffn=row(norm_ffn[lyr]), wg=w_gate[lyr].astype(BF16), wu=w_up[lyr].astype(BF16),
        wd=w_down[lyr].astype(BF16), norm_final=row(norm_final),
        tn=BIG_WIDTH // 4, th=512,
    )

    topk_p = min(TOPK_MAX, s // 4)
    assert SSD_CHUNK == LANES and s % LANES == 0
    tiles_p = _bias_tiles(rel_bias, LANES, key_major=True)
    tiles_p = tiles_p - tiles_p[0:1]

    def att_prompt(big3, qi_t, small3, small3_t, kin3, k4, v4):
        k16 = big3[:, :, COL_K:COL_K + ATT_WIDTH].astype(BF16)
        vt16 = jnp.swapaxes(big3[:, :, COL_V:COL_V + ATT_WIDTH].astype(BF16), -1, -2)
        return _attn_prompt(big3, qi_t, small3_t, kin3, k16, vt16, tiles_p, topk=topk_p, group=4)

    zero_conv = jnp.zeros((bp, CONV_W - 1, CONV_DIM), F32)
    zero_ssm = jnp.zeros((bp, N_SSM_HEADS, SSM_HEAD_DIM, SSM_STATE), F32)
    ssm_params = (p["conv_w"], p["conv_b"], p["dt_bias"], p["a_log"], p["d_skip"], p["ssm_norm"])

    def ssd_prompt(big3, small3, small3_t):
        return _ssd(big3, small3, small3_t, zero_conv, zero_ssm, *ssm_params, l=SSD_CHUNK, l_valid=SSD_CHUNK)

    yp, kp, vp, kip, cp, sp = _layer(x_prompt, att_prompt, ssd_prompt, SSD_CHUNK, p, tm=512)

    topk_s = min(TOPK_MAX, (past + t) // 4)
    tiles_s = jnp.repeat(_bias_tiles(rel_bias, t, key_major=False).reshape(3, N_ATT_HEADS * t, LANES),
                         N_ATT_HEADS, axis=-1)

    cache_kidx_t = jnp.swapaxes(cache_kidx, -1, -2)

    gs = math.gcd(bs, LANES // t)

    def att_sample(big3, qi_t, small3, small3_t, kin3, k4, v4):
        sel = _select_sample(page_table, qi_t, small3, kin3, cache_kidx_t, topk=topk_s, gs=gs)
        return _attn_sample(page_table, big3, k4, v4, sel, tiles_s, cache_k, cache_v, gs=gs)

    def ssd_sample(big3, small3, small3_t):
        return _ssd_sample(big3, small3, small3_t, state_conv[lyr], state_ssm[lyr], *ssm_params, gs=gs)

    ys, ks, vs, kis, cs, ss = _layer(x_sample, att_sample, ssd_sample, gs * t, p, tm=512)

    st = lambda a: a[None]
    return (yp, ys, st(kp), st(vp), st(kip), st(cp), st(sp), st(ks), st(vs), st(kis), st(cs), st(ss))
```

```python
import functools
import math

import numpy as np
import jax
import jax.numpy as jnp
from jax import lax
from jax.experimental import pallas as pl
from jax.experimental.pallas import tpu as pltpu

N_ATT_HEADS = 8
ATT_HEAD_DIM = 128
ATT_WIDTH = N_ATT_HEADS * ATT_HEAD_DIM
N_IDX_HEADS = 16
IDX_DIM = 64
TOPK_MAX = 256
N_SSM_HEADS = 16
SSM_HEAD_DIM = 64
SSM_WIDTH = N_SSM_HEADS * SSM_HEAD_DIM
N_SSM_GROUPS = 2
SSM_STATE = 128
CONV_W = 4
CONV_DIM = SSM_WIDTH + 2 * N_SSM_GROUPS * SSM_STATE
SSD_CHUNK = 128
PAGE_SIZE = 128
N_BUCKETS = 32
MAX_DISTANCE = 128
NORM_EPS = 1e-6

LANES = 128
SUBLANES = 8
VMEM_LIMIT = 56 * 1024 * 1024

COL_Q, COL_K, COL_V, COL_QI, COL_Z, COL_X = (i * ATT_WIDTH for i in range(6))
BIG_WIDTH = 6 * ATT_WIDTH
GROUP_WIDTH = N_SSM_GROUPS * SSM_STATE
BC_WIDTH = 2 * GROUP_WIDTH
SM_KI, SM_WI, SM_DT = 0, IDX_DIM, IDX_DIM + N_IDX_HEADS
SMALL_WIDTH = LANES
INPROJ_TN = 2 * ATT_WIDTH

F32 = jnp.float32
BF16 = jnp.bfloat16

_INT_MIN = -(2 ** 31)
_INT_MAX = 2 ** 31 - 1
_KEY_NEG_INF = int(np.array([-np.inf], np.float32).view(np.int32)[0]) ^ 0x7FFFFFFF
_NEG_BIG = -1e30


def _dot(a, b):
    return jnp.dot(a, b, preferred_element_type=F32)


def _dot_nt(a, b):
    return lax.dot_general(a, b, (((1,), (1,)), ((), ())), preferred_element_type=F32)


def _dot_tn(a, b):
    return lax.dot_general(a, b, (((0,), (0,)), ((), ())), preferred_element_type=F32)


def _rms(x, g):
    return x * lax.rsqrt(jnp.mean(x * x, axis=-1, keepdims=True) + NORM_EPS) * g


def _silu(x):
    return x * (1.0 / (1.0 + jnp.exp(-x)))


def _inproj_body(x_ref, g_ref, wb_ref, ws_ref, gk_ref, big_ref, bc_ref, small_ref, kin_ref, *rest, kv_bf16):
    u_scr = rest[-1]
    n = pl.program_id(1)

    @pl.when(n == 0)
    def _():
        ub = _rms(x_ref[...], g_ref[...]).astype(BF16)
        u_scr[...] = ub
        nar = _dot(ub, ws_ref[...])
        bc_ref[...] = nar[:, :BC_WIDTH]
        sm = nar[:, BC_WIDTH:]
        small_ref[...] = sm
        kin_ref[...] = _rms(sm[:, SM_KI:SM_KI + IDX_DIM], gk_ref[...])

    tile = _dot(u_scr[...], wb_ref[...])
    big_ref[...] = tile
    if kv_bf16:
        k16_ref, vt16_ref = rest[:2]

        @pl.when(n == COL_K // INPROJ_TN)
        def _():
            k16_ref[...] = tile[:, COL_K % INPROJ_TN:COL_K % INPROJ_TN + ATT_WIDTH].astype(BF16)

        @pl.when(n == COL_V // INPROJ_TN)
        def _():
            vt16_ref[0] = tile[:, COL_V % INPROJ_TN:COL_V % INPROJ_TN + ATT_WIDTH].T.astype(BF16)


def _inproj(x2, g, wb, ws, gk, *, tm, seq_len=None):
    m, d = x2.shape
    kv_bf16 = seq_len is not None
    out_specs = [
        pl.BlockSpec((tm, INPROJ_TN), lambda i, j: (i, j)),
        pl.BlockSpec((tm, BC_WIDTH), lambda i, j: (i, 0)),
        pl.BlockSpec((tm, SMALL_WIDTH), lambda i, j: (i, 0)),
        pl.BlockSpec((tm, IDX_DIM), lambda i, j: (i, 0)),
    ]
    out_shape = [
        jax.ShapeDtypeStruct((m, BIG_WIDTH), F32),
        jax.ShapeDtypeStruct((m, BC_WIDTH), F32),
        jax.ShapeDtypeStruct((m, SMALL_WIDTH), F32),
        jax.ShapeDtypeStruct((m, IDX_DIM), F32),
    ]
    if kv_bf16:
        assert seq_len % tm == 0
        per_seq = seq_len // tm
        out_specs += [
            pl.BlockSpec((tm, ATT_WIDTH), lambda i, j: (i, 0)),
            pl.BlockSpec((1, ATT_WIDTH, tm), lambda i, j: (i // per_seq, 0, i % per_seq)),
        ]
        out_shape += [
            jax.ShapeDtypeStruct((m, ATT_WIDTH), BF16),
            jax.ShapeDtypeStruct((m // seq_len, ATT_WIDTH, seq_len), BF16),
        ]
    return pl.pallas_call(
        functools.partial(_inproj_body, kv_bf16=kv_bf16),
        grid=(m // tm, BIG_WIDTH // INPROJ_TN),
        in_specs=[
            pl.BlockSpec((tm, d), lambda i, j: (i, 0)),
            pl.BlockSpec((1, d), lambda i, j: (0, 0)),
            pl.BlockSpec((d, INPROJ_TN), lambda i, j: (0, j)),
            pl.BlockSpec((d, BC_WIDTH + SMALL_WIDTH), lambda i, j: (0, 0), pipeline_mode=pl.Buffered(1)),
            pl.BlockSpec((1, IDX_DIM), lambda i, j: (0, 0)),
        ],
        out_specs=out_specs,
        out_shape=out_shape,
        scratch_shapes=[pltpu.VMEM((tm, d), BF16)],
        compiler_params=pltpu.CompilerParams(
            dimension_semantics=("parallel", "arbitrary"), vmem_limit_bytes=VMEM_LIMIT),
        name="inproj",
    )(x2, g, wb, ws, gk)


def _sortable_key(s):
    s = jnp.where(s == 0.0, 0.0, s)
    b = pltpu.bitcast(s, jnp.int32)
    return b ^ ((b >> 31) & 0x7FFFFFFF)


def _count_keys(key_ref, nblk, pred, key_axis):
    def body(kb, acc):
        return acc + jnp.where(pred(key_ref[kb], kb), 1.0, 0.0)

    acc = jnp.zeros(key_ref.shape[1:], F32)
    if isinstance(nblk, int):
        for kb in range(nblk):
            acc = body(kb, acc)
    else:
        acc = lax.fori_loop(0, nblk, body, acc)
    return jnp.sum(acc, axis=key_axis, keepdims=True)


def _topk_select_params(key_ref, nblk, topk, key_axis):
    shape = key_ref.shape[1:]
    qshape = tuple(1 if a == key_axis else n for a, n in enumerate(shape))
    blk = shape[key_axis]
    kf = float(topk)

    def count_ge(cand):
        cb = jnp.broadcast_to(cand, shape)
        return _count_keys(key_ref, nblk, lambda k, kb: k >= cb, key_axis)

    res = jnp.where(count_ge(jnp.zeros(qshape, jnp.int32)) >= kf, 0, _INT_MIN).astype(jnp.int32)

    def bit_body(it, res):
        cand = res | jnp.left_shift(jnp.int32(1), 30 - it)
        return jnp.where(count_ge(cand) >= kf, cand, res)

    thr = lax.fori_loop(0, 31, bit_body, res)
    thr_b = jnp.broadcast_to(thr, shape)
    n_gt = _count_keys(key_ref, nblk, lambda k, kb: k > thr_b, key_axis)
    n_ge = count_ge(thr)
    tied = jnp.logical_and(n_ge > kf, thr > _KEY_NEG_INF)
    any_tied = jnp.max(jnp.where(tied, 1.0, 0.0)) > 0.0
    need = kf - n_gt
    kpos = lax.broadcasted_iota(jnp.int32, shape, key_axis)

    def tie_break():
        def body(it, cut):
            cand = cut | jnp.left_shift(jnp.int32(1), 30 - it)
            cand_b = jnp.broadcast_to(cand, shape)
            n = _count_keys(key_ref, nblk,
                            lambda k, kb: jnp.logical_and(k == thr_b, kpos + kb * blk < cand_b), key_axis)
            return jnp.where(n <= need, cand, cut)

        return lax.fori_loop(0, 31, body, jnp.zeros(qshape, jnp.int32))

    cut = lax.cond(any_tied, tie_break, lambda: jnp.full(qshape, _INT_MAX, jnp.int32))
    return thr, cut


def _attn_prompt_body(q_ref, qi_ref, smt_ref, kin_ref, k_ref, vt_ref, bias_ref, o_ref,
                      key_scr, mask_scr, lg_scr, qi_scr, *, topk, group):
    i = pl.program_id(1)
    q0 = i * LANES
    blk = (LANES, LANES)
    scale = ATT_HEAD_DIM ** -0.5
    nq = key_scr.shape[0]

    w_t = smt_ref[0, SM_WI:SM_WI + N_IDX_HEADS, :]
    qi_all = qi_ref[0]
    for h in range(N_IDX_HEADS):
        qi_scr[h * LANES:(h + 1) * LANES, :] = qi_all[:, h * IDX_DIM:(h + 1) * IDX_DIM].astype(BF16)
    qi = qi_scr[...]
    kpos = lax.broadcasted_iota(jnp.int32, blk, 0)
    qpos = lax.broadcasted_iota(jnp.int32, blk, 1) + q0
    qb = q_ref[0].astype(BF16)

    def tile(nb, first_near):
        for kb in range(nb):
            kin = kin_ref[0, kb * LANES:(kb + 1) * LANES, :].astype(BF16)
            s = _dot_nt(kin, qi)
            score = jnp.zeros(blk, F32)
            for h in range(N_IDX_HEADS):
                score = score + jnp.maximum(s[:, h * LANES:(h + 1) * LANES], 0.0) * w_t[h:h + 1, :]
            if kb >= first_near:
                score = jnp.where(kpos + kb * LANES <= qpos, score, -jnp.inf)
            key_scr[kb] = _sortable_key(score)

        thr, cut = _topk_select_params(key_scr, nb, topk, key_axis=0)
        thr_b = jnp.broadcast_to(thr, blk)
        cut_b = jnp.broadcast_to(cut, blk)
        for kb in range(nb):
            kp = kpos + kb * LANES
            key = key_scr[kb]
            valid = jnp.logical_or(key > thr_b, jnp.logical_and(key == thr_b, kp < cut_b))
            if kb >= first_near:
                valid = jnp.logical_and(valid, kp <= qpos)
            mask_scr[kb * LANES:(kb + 1) * LANES, :] = jnp.where(valid, 0.0, -jnp.inf)

        nk = nb * LANES
        for h in range(N_ATT_HEADS):
            hs = slice(h * ATT_HEAD_DIM, (h + 1) * ATT_HEAD_DIM)
            lg = _dot_nt(k_ref[0, 0:nk, hs], qb[:, hs]) * scale + mask_scr[0:nk, :]
            for kb in range(first_near, nb):
                tix = 2 - jnp.clip(i - kb, 0, 2)
                lg_scr[kb * LANES:(kb + 1) * LANES, :] = lg[kb * LANES:(kb + 1) * LANES, :] + bias_ref[tix, h]
            if first_near > 0:
                lg_scr[0:first_near * LANES, :] = lg[0:first_near * LANES, :]
            lg = lg_scr[0:nk, :]
            m = jnp.max(lg, axis=0, keepdims=True)
            p = jnp.exp(lg - m)
            l = jnp.sum(p, axis=0, keepdims=True)
            out_t = _dot(vt_ref[0, hs, 0:nk], p.astype(BF16)) / l
            o_ref[0, :, hs] = out_t.T.astype(o_ref.dtype)

    ngroups = -(-nq // group)
    for c in range(ngroups):
        nb = min((c + 1) * group, nq)

        @pl.when(i // group == c)
        def _(nb=nb, c=c):
            tile(nb, max(c * group - 1, 0))


def _attn_prompt(big3, small3_t, kin3, k16, vt16, bias_tiles_t, *, topk, group):
    b, s, _ = big3.shape
    nq = s // LANES
    body = functools.partial(_attn_prompt_body, topk=topk, group=group)
    return pl.pallas_call(
        body,
        grid=(b, nq),
        in_specs=[
            pl.BlockSpec((1, LANES, ATT_WIDTH), lambda bi, i: (bi, i, COL_Q // ATT_WIDTH)),
            pl.BlockSpec((1, LANES, N_IDX_HEADS * IDX_DIM), lambda bi, i: (bi, i, COL_QI // (N_IDX_HEADS * IDX_DIM))),
            pl.BlockSpec((1, SMALL_WIDTH, LANES), lambda bi, i: (bi * nq + i, 0, 0)),
            pl.BlockSpec((1, s, IDX_DIM), lambda bi, i: (bi, 0, 0)),
            pl.BlockSpec((1, s, ATT_WIDTH), lambda bi, i: (bi, 0, 0)),
            pl.BlockSpec((1, ATT_WIDTH, s), lambda bi, i: (bi, 0, 0)),
            pl.BlockSpec(bias_tiles_t.shape, lambda bi, i: (0, 0, 0, 0)),
        ],
        out_specs=pl.BlockSpec((1, LANES, ATT_WIDTH), lambda bi, i: (bi, i, 0)),
        out_shape=jax.ShapeDtypeStruct((b, s, ATT_WIDTH), BF16),
        scratch_shapes=[
            pltpu.VMEM((nq, LANES, LANES), jnp.int32),
            pltpu.VMEM((s, LANES), F32),
            pltpu.VMEM((s, LANES), F32),
            pltpu.VMEM((N_IDX_HEADS * LANES, IDX_DIM), BF16),
        ],
        compiler_params=pltpu.CompilerParams(
            dimension_semantics=("parallel", "arbitrary"), vmem_limit_bytes=VMEM_LIMIT),
        name="attn_prompt",
    )(big3, big3, small3_t, kin3, k16, vt16, bias_tiles_t)


def _select_sample_body(pt_ref, qi_ref, sm_ref, kinew_ref, *rest, n_pages, t, topk):
    kidx_refs = rest[:n_pages]
    sel_ref = rest[n_pages]
    key_scr, wb_scr = rest[n_pages + 1:]
    nblk = n_pages + 1
    j = pl.program_id(1)
    r0 = pl.multiple_of(j * t, t)

    wi = sm_ref[0, :, SM_WI:SM_WI + N_IDX_HEADS]
    for h in range(N_IDX_HEADS):
        wb_scr[h] = jnp.broadcast_to(wi[:, h:h + 1], (t, LANES))
    qi_all = qi_ref[0]
    qi = jnp.concatenate([qi_all[:, h * IDX_DIM:(h + 1) * IDX_DIM] for h in range(N_IDX_HEADS)],
                         axis=0).astype(BF16)

    def scores(s):
        return jnp.sum(jnp.maximum(s.reshape(N_IDX_HEADS, t, LANES), 0.0) * wb_scr[...], axis=0)

    for p in range(n_pages):
        key_scr[p, pl.ds(r0, t), :] = _sortable_key(scores(_dot(qi, kidx_refs[p][0, 0].astype(BF16))))
    kin_new = jnp.concatenate([kinew_ref[0], jnp.zeros((LANES - t, IDX_DIM), F32)], axis=0)
    trow = lax.broadcasted_iota(jnp.int32, (t, LANES), 0)
    lane_t = lax.broadcasted_iota(jnp.int32, (t, LANES), 1)
    s_new = jnp.where(lane_t <= trow, scores(_dot_nt(qi, kin_new.astype(BF16))), -jnp.inf)
    key_scr[n_pages, pl.ds(r0, t), :] = _sortable_key(s_new)

    @pl.when(j == pl.num_programs(1) - 1)
    def _():
        rows = key_scr.shape[1]
        thr, cut = _topk_select_params(key_scr, nblk, topk, key_axis=1)
        thr_b = jnp.broadcast_to(thr, (rows, LANES))
        cut_b = jnp.broadcast_to(cut, (rows, LANES))
        lane = lax.broadcasted_iota(jnp.int32, (rows, LANES), 1)
        tq = lax.broadcasted_iota(jnp.int32, (rows, LANES), 0) & (t - 1)
        for p in range(nblk):
            key = key_scr[p]
            sel = jnp.logical_or(key > thr_b, jnp.logical_and(key == thr_b, lane + p * LANES < cut_b))
            if p == n_pages:
                sel = jnp.logical_and(sel, lane <= tq)
            sel_ref[0, p] = jnp.where(sel, 1.0, 0.0)


def _select_sample(page_table, big3, small3, kin3, cache_kidx_t, *, topk, gs):
    b, t, _ = small3.shape
    n_pages = page_table.shape[1]
    assert b % gs == 0 and t & (t - 1) == 0
    body = functools.partial(_select_sample_body, n_pages=n_pages, t=t, topk=topk)
    seq = lambda g, j: g * gs + j
    qi_width = N_IDX_HEADS * IDX_DIM
    in_specs = [
        pl.BlockSpec((1, t, qi_width), lambda g, j, pt: (seq(g, j), 0, COL_QI // qi_width)),
        pl.BlockSpec((1, t, SMALL_WIDTH), lambda g, j, pt: (seq(g, j), 0, 0)),
        pl.BlockSpec((1, t, IDX_DIM), lambda g, j, pt: (seq(g, j), 0, 0)),
    ]
    in_specs += [pl.BlockSpec((1, 1, IDX_DIM, PAGE_SIZE), lambda g, j, pt, p=p: (0, pt[seq(g, j), p], 0, 0))
                 for p in range(n_pages)]
    sel_shape = (b // gs, n_pages + 1, gs * t, LANES)
    return pl.pallas_call(
        body,
        grid_spec=pltpu.PrefetchScalarGridSpec(
            num_scalar_prefetch=1,
            grid=(b // gs, gs),
            in_specs=in_specs,
            out_specs=pl.BlockSpec((1,) + sel_shape[1:], lambda g, j, pt: (g, 0, 0, 0)),
            scratch_shapes=[
                pltpu.VMEM(sel_shape[1:], jnp.int32),
                pltpu.VMEM((N_IDX_HEADS, t, LANES), F32),
            ],
        ),
        out_shape=jax.ShapeDtypeStruct(sel_shape, F32),
        compiler_params=pltpu.CompilerParams(
            dimension_semantics=("parallel", "arbitrary"), vmem_limit_bytes=VMEM_LIMIT),
        name="select_sample",
    )(page_table, big3, small3, kin3, *([cache_kidx_t] * n_pages))


def _attn_sample_body(pt_ref, q_ref, knew_ref, vnew_ref, sel_ref, bias_ref, *rest, n_pages, t):
    k_refs = rest[:n_pages]
    v_refs = rest[n_pages:2 * n_pages]
    o_ref = rest[2 * n_pages]
    (lg_scr,) = rest[2 * n_pages + 1:]
    nblk = n_pages + 1
    rows = N_ATT_HEADS * t
    flat = PAGE_SIZE * N_ATT_HEADS
    scale = ATT_HEAD_DIM ** -0.5

    q = q_ref[0]
    q64 = jnp.concatenate([q[:, h * ATT_HEAD_DIM:(h + 1) * ATT_HEAD_DIM] for h in range(N_ATT_HEADS)],
                          axis=0).astype(BF16)
    rr = lax.broadcasted_iota(jnp.int32, (rows, flat), 0)
    cc = lax.broadcasted_iota(jnp.int32, (rows, flat), 1)
    head_match = (cc & (N_ATT_HEADS - 1)) == (rr >> (t.bit_length() - 1))
    er = lax.broadcasted_iota(jnp.int32, (LANES, flat), 0)
    ec = lax.broadcasted_iota(jnp.int32, (LANES, flat), 1)
    expand = jnp.where((ec >> (N_ATT_HEADS.bit_length() - 1)) == er, 1.0, 0.0).astype(BF16)

    def flat_kv(refs, p):
        if p < n_pages:
            return refs[p][0, 0].reshape(flat, ATT_HEAD_DIM).astype(BF16), flat
        new = refs[p][0].reshape(t * N_ATT_HEADS, ATT_HEAD_DIM)
        pad = jnp.zeros((LANES - t * N_ATT_HEADS, ATT_HEAD_DIM), F32)
        return jnp.concatenate([new, pad], axis=0).astype(BF16), LANES

    k_all = list(k_refs) + [knew_ref]
    v_all = list(v_refs) + [vnew_ref]
    m = jnp.full((rows, 1), _NEG_BIG, F32)
    for p in range(nblk):
        kflat, width = flat_kv(k_all, p)
        selx = _dot(sel_ref[0, p].astype(BF16), expand[:, :width])
        valid = jnp.logical_and(head_match[:, :width], jnp.concatenate([selx] * N_ATT_HEADS, axis=0) > 0.5)
        tix = 2 - min(n_pages - p, 2)
        lg = _dot_nt(q64, kflat) * scale + bias_ref[tix, :, :width]
        lg = jnp.where(valid, lg, _NEG_BIG)
        lg_scr[p, :, :width] = lg
        m = jnp.maximum(m, jnp.max(lg, axis=-1, keepdims=True))

    l = jnp.zeros((rows, 1), F32)
    acc = jnp.zeros((rows, ATT_HEAD_DIM), F32)
    for p in range(nblk):
        vflat, width = flat_kv(v_all, p)
        pr = jnp.exp(lg_scr[p, :, :width] - m)
        l = l + jnp.sum(pr, axis=-1, keepdims=True)
        acc = acc + _dot(pr.astype(BF16), vflat)
    out = acc / l
    for h in range(N_ATT_HEADS):
        o_ref[0, :, h * ATT_HEAD_DIM:(h + 1) * ATT_HEAD_DIM] = out[h * t:(h + 1) * t, :].astype(o_ref.dtype)


def _attn_sample(page_table, big3, k_new, v_new, sel, bias_rows, cache_k, cache_v, *, gs):
    b, t, _ = big3.shape
    n_pages = page_table.shape[1]
    assert t == SUBLANES and N_ATT_HEADS & (N_ATT_HEADS - 1) == 0 and t * N_ATT_HEADS <= LANES
    body = functools.partial(_attn_sample_body, n_pages=n_pages, t=t)
    kv_block = (1, 1, PAGE_SIZE, N_ATT_HEADS, ATT_HEAD_DIM)
    new_block = (1, t, N_ATT_HEADS, ATT_HEAD_DIM)

    in_specs = [
        pl.BlockSpec((1, t, ATT_WIDTH), lambda bi, pt: (bi, 0, COL_Q // ATT_WIDTH)),
        pl.BlockSpec(new_block, lambda bi, pt: (bi, 0, 0, 0)),
        pl.BlockSpec(new_block, lambda bi, pt: (bi, 0, 0, 0)),
        pl.BlockSpec((1, n_pages + 1, t, LANES), lambda bi, pt: (bi // gs, 0, bi % gs, 0)),
        pl.BlockSpec(bias_rows.shape, lambda bi, pt: (0, 0, 0)),
    ]
    in_specs += [pl.BlockSpec(kv_block, lambda bi, pt, p=p: (0, pt[bi, p], 0, 0, 0)) for p in range(n_pages)] * 2
    rows = N_ATT_HEADS * t
    return pl.pallas_call(
        body,
        grid_spec=pltpu.PrefetchScalarGridSpec(
            num_scalar_prefetch=1,
            grid=(b,),
            in_specs=in_specs,
            out_specs=pl.BlockSpec((1, t, ATT_WIDTH), lambda bi, pt: (bi, 0, 0)),
            scratch_shapes=[pltpu.VMEM((n_pages + 1, rows, PAGE_SIZE * N_ATT_HEADS), F32)],
        ),
        out_shape=jax.ShapeDtypeStruct((b, t, ATT_WIDTH), BF16),
        compiler_params=pltpu.CompilerParams(
            dimension_semantics=("arbitrary",), vmem_limit_bytes=VMEM_LIMIT),
        name="attn_sample",
    )(page_table, big3, k_new, v_new, sel, bias_rows, *([cache_k] * n_pages), *([cache_v] * n_pages))


def _split3(a):
    a1 = a.astype(BF16)
    r1 = a - a1.astype(F32)
    a2 = r1.astype(BF16)
    a3 = (r1 - a2.astype(F32)).astype(BF16)
    return a1, a2, a3


def _softplus(v):
    return jnp.maximum(v, 0.0) + jnp.log1p(jnp.exp(-jnp.abs(v)))


def _head_expander():
    r = lax.broadcasted_iota(jnp.int32, (SMALL_WIDTH, SSM_WIDTH), 0)
    c = lax.broadcasted_iota(jnp.int32, (SMALL_WIDTH, SSM_WIDTH), 1)
    return jnp.where(r - SM_DT == c // SSM_HEAD_DIM, 1.0, 0.0).astype(BF16)


def _ssd_body(x_ref, bm_ref, cm_ref, z_ref, sm_ref, smt_ref, pre_ref, h0_ref,
              cw_ref, cb_ref, dtb_ref, dtbt_ref, alog_ref, alogt_ref, dskipx_ref, gn_ref,
              y_ref, tail_ref, hout_ref, xp_scr, ht_scr, y_scr, *, l):
    c = pl.program_id(1)
    nc = pl.num_programs(1)
    off = SUBLANES
    heads_per_group = N_SSM_HEADS // N_SSM_GROUPS
    gcols = heads_per_group * SSM_HEAD_DIM

    @pl.when(c == 0)
    def _():
        xp_scr[off - (CONV_W - 1):off, :] = pre_ref[0]
        for h in range(N_SSM_HEADS):
            g, hh = divmod(h, heads_per_group)
            ht_scr[g, :, hh * SSM_HEAD_DIM:(hh + 1) * SSM_HEAD_DIM] = h0_ref[0, h].T

    xp_scr[off:off + l, 0:SSM_WIDTH] = x_ref[0]
    xp_scr[off:off + l, SSM_WIDTH:SSM_WIDTH + GROUP_WIDTH] = bm_ref[0]
    xp_scr[off:off + l, SSM_WIDTH + GROUP_WIDTH:CONV_DIM] = cm_ref[0]

    conv = cb_ref[...]
    for j in range(CONV_W):
        s0 = off - (CONV_W - 1) + j
        conv = conv + xp_scr[s0:s0 + l, :] * cw_ref[j:j + 1, :]
    xc = _silu(conv)
    tail = xp_scr[off + l - (CONV_W - 1):off + l, :]
    tail_ref[0] = tail
    xp_scr[off - (CONV_W - 1):off, :] = tail

    xs = xc[:, :SSM_WIDTH]
    bc = [xc[:, SSM_WIDTH + g * SSM_STATE:SSM_WIDTH + (g + 1) * SSM_STATE].astype(BF16)
          for g in range(N_SSM_GROUPS)]
    cc = [xc[:, SSM_WIDTH + GROUP_WIDTH + g * SSM_STATE:SSM_WIDTH + GROUP_WIDTH + (g + 1) * SSM_STATE]
          .astype(BF16) for g in range(N_SSM_GROUPS)]

    dt = _softplus(sm_ref[0] + dtb_ref[...])
    a = dt * (-jnp.exp(alog_ref[...]))
    dtt = _softplus(smt_ref[0, SM_DT:SM_DT + N_SSM_HEADS, :] + dtbt_ref[...])
    at = dtt * (-jnp.exp(alogt_ref[...]))

    ri = lax.broadcasted_iota(jnp.int32, (l, l), 0)
    ci = lax.broadcasted_iota(jnp.int32, (l, l), 1)
    causal = ri >= ci
    tri = jnp.where(causal, 1.0, 0.0).astype(BF16)
    tri_u = jnp.where(ri <= ci, 1.0, 0.0).astype(BF16)
    acum = sum(_dot(tri, piece) for piece in _split3(a))
    acum_t = sum(_dot(piece, tri_u) for piece in _split3(at))
    last = acum[l - 1:l, :]
    wts = jnp.exp(last - acum) * dt
    expand = _head_expander()
    acum_x = sum(_dot(piece, expand) for piece in _split3(acum))
    wts_x = sum(_dot(piece, expand) for piece in _split3(wts))
    exp_last_x = jnp.exp(acum_x[l - 1:l, :])

    cb = [_dot_nt(cc[g], bc[g]) for g in range(N_SSM_GROUPS)]
    for h in range(N_SSM_HEADS):
        g = h // heads_per_group
        hs = slice(h * SSM_HEAD_DIM, (h + 1) * SSM_HEAD_DIM)
        col = acum[:, SM_DT + h:SM_DT + h + 1]
        decay = jnp.exp(jnp.where(causal, col - acum_t[h:h + 1, :], -jnp.inf))
        mmat = cb[g] * decay * dtt[h:h + 1, :]
        y_scr[:, hs] = _dot(mmat.astype(BF16), xs[:, hs].astype(BF16))

    xw = (xs * wts_x).astype(BF16)
    y_inter = []
    for g in range(N_SSM_GROUPS):
        gsl = slice(g * gcols, (g + 1) * gcols)
        ht = ht_scr[g]
        y_inter.append(_dot(cc[g], ht.astype(BF16)))
        ht_scr[g] = ht * exp_last_x[:, gsl] + _dot_tn(bc[g], xw[:, gsl])
    y = y_scr[...] + jnp.concatenate(y_inter, axis=1) * jnp.exp(acum_x) + dskipx_ref[...] * xs
    y_ref[0] = _rms(y * _silu(z_ref[0]), gn_ref[...]).astype(y_ref.dtype)

    @pl.when(c == nc - 1)
    def _():
        for h in range(N_SSM_HEADS):
            g, hh = divmod(h, heads_per_group)
            hout_ref[0, h] = ht_scr[g, :, hh * SSM_HEAD_DIM:(hh + 1) * SSM_HEAD_DIM].T


def _pad_lanes(v):
    return jnp.zeros((1, SMALL_WIDTH), F32).at[0, SM_DT:SM_DT + N_SSM_HEADS].set(v)


def _ssd(big3, bc3, small3, small3_t, prefix, h0, conv_w, conv_b, dt_bias, a_log, d_skip, ssm_norm, *, l):
    b, t, _ = big3.shape
    nc = t // l
    body = functools.partial(_ssd_body, l=l)
    row = lambda v: v.reshape(1, -1)
    colv = lambda v: v.reshape(-1, 1)
    const2 = lambda shape: pl.BlockSpec(shape, lambda bi, c: (0, 0))
    return pl.pallas_call(
        body,
        grid=(b, nc),
        in_specs=[
            pl.BlockSpec((1, l, SSM_WIDTH), lambda bi, c: (bi, c, COL_X // SSM_WIDTH)),
            pl.BlockSpec((1, l, GROUP_WIDTH), lambda bi, c: (bi, c, 0)),
            pl.BlockSpec((1, l, GROUP_WIDTH), lambda bi, c: (bi, c, 1)),
            pl.BlockSpec((1, l, SSM_WIDTH), lambda bi, c: (bi, c, COL_Z // SSM_WIDTH)),
            pl.BlockSpec((1, l, SMALL_WIDTH), lambda bi, c: (bi, c, 0)),
            pl.BlockSpec((1, SMALL_WIDTH, l), lambda bi, c: (bi * nc + c, 0, 0)),
            pl.BlockSpec((1, CONV_W - 1, CONV_DIM), lambda bi, c: (bi, 0, 0)),
            pl.BlockSpec((1, N_SSM_HEADS, SSM_HEAD_DIM, SSM_STATE), lambda bi, c: (bi, 0, 0, 0)),
            const2((CONV_W, CONV_DIM)),
            const2((1, CONV_DIM)),
            const2((1, SMALL_WIDTH)),
            const2((N_SSM_HEADS, 1)),
            const2((1, SMALL_WIDTH)),
            const2((N_SSM_HEADS, 1)),
            const2((1, SSM_WIDTH)),
            const2((1, SSM_WIDTH)),
        ],
        out_specs=[
            pl.BlockSpec((1, l, SSM_WIDTH), lambda bi, c: (bi, c, 0)),
            pl.BlockSpec((1, CONV_W - 1, CONV_DIM), lambda bi, c: (bi, 0, 0)),
            pl.BlockSpec((1, N_SSM_HEADS, SSM_HEAD_DIM, SSM_STATE), lambda bi, c: (bi, 0, 0, 0)),
        ],
        out_shape=[
            jax.ShapeDtypeStruct((b, t, SSM_WIDTH), BF16),
            jax.ShapeDtypeStruct((b, CONV_W - 1, CONV_DIM), F32),
            jax.ShapeDtypeStruct((b, N_SSM_HEADS, SSM_HEAD_DIM, SSM_STATE), F32),
        ],
        scratch_shapes=[
            pltpu.VMEM((SUBLANES + l, CONV_DIM), F32),
            pltpu.VMEM((N_SSM_GROUPS, SSM_STATE, SSM_WIDTH // N_SSM_GROUPS), F32),
            pltpu.VMEM((l, SSM_WIDTH), F32),
        ],
        compiler_params=pltpu.CompilerParams(
            dimension_semantics=("parallel", "arbitrary"), vmem_limit_bytes=VMEM_LIMIT),
        name="ssd",
    )(big3, bc3, bc3, big3, small3, small3_t, prefix, h0,
      conv_w, row(conv_b), _pad_lanes(dt_bias), colv(dt_bias), _pad_lanes(a_log), colv(a_log),
      row(jnp.repeat(d_skip, SSM_HEAD_DIM)), row(ssm_norm))


def _ssd_sample_body(x_ref, bm_ref, cm_ref, z_ref, sm_ref, smt_ref, pre_ref, h0_ref,
                     cw_ref, cb_ref, dtb_ref, dtbt_ref, alog_ref, alogt_ref, dskip_ref, gn_ref,
                     y_ref, tail_ref, hout_ref, xp_scr, xc_scr, yi_scr, xw_scr, y_scr, *, gs, t):
    rows = gs * t
    off = SUBLANES
    tshift = t.bit_length() - 1
    gw = N_SSM_GROUPS * SSM_STATE
    heads_per_group = N_SSM_HEADS // N_SSM_GROUPS
    gcols = heads_per_group * SSM_HEAD_DIM

    for s in range(gs):
        xp_scr[s, off - (CONV_W - 1):off, :] = pre_ref[s]
        xp_scr[s, off:off + t, 0:SSM_WIDTH] = x_ref[s]
        xp_scr[s, off:off + t, SSM_WIDTH:SSM_WIDTH + gw] = bm_ref[s]
        xp_scr[s, off:off + t, SSM_WIDTH + gw:CONV_DIM] = cm_ref[s]
        conv = cb_ref[...]
        for j in range(CONV_W):
            s0 = off - (CONV_W - 1) + j
            conv = conv + xp_scr[s, s0:s0 + t, :] * cw_ref[j:j + 1, :]
        xc_scr[s * t:(s + 1) * t, :] = _silu(conv)
        tail_ref[s] = xp_scr[s, off + t - (CONV_W - 1):off + t, :]

    xc = xc_scr[...]
    xs = xc[:, :SSM_WIDTH]
    bcf = [xc[:, SSM_WIDTH + g * SSM_STATE:SSM_WIDTH + (g + 1) * SSM_STATE] for g in range(N_SSM_GROUPS)]
    ccf = [xc[:, SSM_WIDTH + gw + g * SSM_STATE:SSM_WIDTH + gw + (g + 1) * SSM_STATE] for g in range(N_SSM_GROUPS)]

    def softplus(v):
        return jnp.maximum(v, 0.0) + jnp.log1p(jnp.exp(-jnp.abs(v)))

    dt = softplus(sm_ref[...].reshape(rows, SMALL_WIDTH)[:, SM_DT:SM_DT + N_SSM_HEADS] + dtb_ref[...])
    dtt = softplus(smt_ref[0, SM_DT:SM_DT + N_SSM_HEADS, :] + dtbt_ref[...])
    a = dt * (-jnp.exp(alog_ref[...]))
    at = dtt * (-jnp.exp(alogt_ref[...]))

    ri = lax.broadcasted_iota(jnp.int32, (rows, rows), 0)
    ci = lax.broadcasted_iota(jnp.int32, (rows, rows), 1)
    same = (ri >> tshift) == (ci >> tshift)
    causal = jnp.logical_and(same, ri >= ci)
    tri = jnp.where(causal, 1.0, 0.0).astype(BF16)
    tri_u = jnp.where(jnp.logical_and(same, ri <= ci), 1.0, 0.0).astype(BF16)
    pick_last = jnp.where(ci == (((ri >> tshift) << tshift) + (t - 1)), 1.0, 0.0).astype(BF16)
    acum = sum(_dot(tri, piece) for piece in _split3(a))
    acum_t = sum(_dot(piece, tri_u) for piece in _split3(at))
    last = sum(_dot(pick_last, piece) for piece in _split3(acum))
    wts = jnp.exp(last - acum) * dt
    exp_last = jnp.exp(last)

    for s in range(gs):
        rs = slice(s * t, (s + 1) * t)
        for g in range(N_SSM_GROUPS):
            h0g = h0_ref[s, g * heads_per_group:(g + 1) * heads_per_group].reshape(gcols, SSM_STATE)
            yi_scr[rs, g * gcols:(g + 1) * gcols] = _dot_nt(ccf[g][rs, :], h0g)

    cb = [_dot_nt(ccf[g].astype(BF16), bcf[g].astype(BF16)) for g in range(N_SSM_GROUPS)]
    for h in range(N_SSM_HEADS):
        g = h // heads_per_group
        hs = slice(h * SSM_HEAD_DIM, (h + 1) * SSM_HEAD_DIM)
        col = acum[:, h:h + 1]
        decay = jnp.exp(jnp.where(causal, col - acum_t[h:h + 1, :], -jnp.inf))
        mmat = cb[g] * decay * dtt[h:h + 1, :]
        xh = xs[:, hs]
        y_intra = _dot(mmat.astype(BF16), xh.astype(BF16))
        y_scr[:, hs] = y_intra + yi_scr[:, hs] * jnp.exp(col) + dskip_ref[h] * xh
        xw_scr[:, hs] = xh * wts[:, h:h + 1]

    out = _rms(y_scr[...] * _silu(z_ref[...].reshape(rows, SSM_WIDTH)), gn_ref[...])
    for s in range(gs):
        y_ref[s] = out[s * t:(s + 1) * t, :].astype(y_ref.dtype)

    for s in range(gs):
        rs = slice(s * t, (s + 1) * t)
        for g in range(N_SSM_GROUPS):
            upd = _dot_tn(xw_scr[rs, g * gcols:(g + 1) * gcols], bcf[g][rs, :])
            for hh in range(heads_per_group):
                h = g * heads_per_group + hh
                scale = jnp.broadcast_to(exp_last[s * t:s * t + 1, h:h + 1], (SSM_HEAD_DIM, SSM_STATE))
                hout_ref[s, h] = h0_ref[s, h] * scale + upd[hh * SSM_HEAD_DIM:(hh + 1) * SSM_HEAD_DIM, :]


def _ssd_sample(big3, bc3, small3, small3_t, prefix, h0, conv_w, conv_b, dt_bias, a_log, d_skip, ssm_norm, *, gs):
    b, t, _ = big3.shape
    assert b % gs == 0 and t & (t - 1) == 0 and t == SUBLANES
    rows = gs * t
    body = functools.partial(_ssd_sample_body, gs=gs, t=t)
    gw = N_SSM_GROUPS * SSM_STATE
    row = lambda v: v.reshape(1, -1)
    colv = lambda v: v.reshape(-1, 1)
    const2 = lambda shape: pl.BlockSpec(shape, lambda gi: (0, 0))
    state_block = (gs, N_SSM_HEADS, SSM_HEAD_DIM, SSM_STATE)
    return pl.pallas_call(
        body,
        grid=(b // gs,),
        in_specs=[
            pl.BlockSpec((gs, t, SSM_WIDTH), lambda gi: (gi, 0, COL_X // SSM_WIDTH)),
            pl.BlockSpec((gs, t, gw), lambda gi: (gi, 0, 0)),
            pl.BlockSpec((gs, t, gw), lambda gi: (gi, 0, 1)),
            pl.BlockSpec((gs, t, SSM_WIDTH), lambda gi: (gi, 0, COL_Z // SSM_WIDTH)),
            pl.BlockSpec((gs, t, SMALL_WIDTH), lambda gi: (gi, 0, 0)),
            pl.BlockSpec((1, SMALL_WIDTH, rows), lambda gi: (gi, 0, 0)),
            pl.BlockSpec((gs, CONV_W - 1, CONV_DIM), lambda gi: (gi, 0, 0)),
            pl.BlockSpec(state_block, lambda gi: (gi, 0, 0, 0)),
            const2((CONV_W, CONV_DIM)),
            const2((1, CONV_DIM)),
            const2((1, N_SSM_HEADS)),
            const2((N_SSM_HEADS, 1)),
            const2((1, N_SSM_HEADS)),
            const2((N_SSM_HEADS, 1)),
            pl.BlockSpec(memory_space=pltpu.SMEM),
            const2((1, SSM_WIDTH)),
        ],
        out_specs=[
            pl.BlockSpec((gs, t, SSM_WIDTH), lambda gi: (gi, 0, 0)),
            pl.BlockSpec((gs, CONV_W - 1, CONV_DIM), lambda gi: (gi, 0, 0)),
            pl.BlockSpec(state_block, lambda gi: (gi, 0, 0, 0)),
        ],
        out_shape=[
            jax.ShapeDtypeStruct((b, t, SSM_WIDTH), BF16),
            jax.ShapeDtypeStruct((b, CONV_W - 1, CONV_DIM), F32),
            jax.ShapeDtypeStruct((b, N_SSM_HEADS, SSM_HEAD_DIM, SSM_STATE), F32),
        ],
        scratch_shapes=[
            pltpu.VMEM((gs, SUBLANES + t, CONV_DIM), F32),
            pltpu.VMEM((rows, CONV_DIM), F32),
            pltpu.VMEM((rows, SSM_WIDTH), F32),
            pltpu.VMEM((rows, SSM_WIDTH), F32),
            pltpu.VMEM((rows, SSM_WIDTH), F32),
        ],
        compiler_params=pltpu.CompilerParams(
            dimension_semantics=("parallel",), vmem_limit_bytes=VMEM_LIMIT),
        name="ssd_sample",
    )(big3, bc3, bc3, big3, small3, small3_t, prefix, h0,
      conv_w, row(conv_b), row(dt_bias), colv(dt_bias), row(a_log), colv(a_log), d_skip, row(ssm_norm))


def _ffn_body(x_ref, att_ref, y_ref, woa_ref, woy_ref, gf_ref, wg_ref, wu_ref, wd_ref, gl_ref,
              o_ref, f_scr):
    j = pl.program_id(1)

    @pl.when(j == 0)
    def _():
        h = x_ref[...] + _dot(att_ref[...], woa_ref[...]) + _dot(y_ref[...], woy_ref[...])
        o_ref[...] = h
        f_scr[...] = _rms(h, gf_ref[...]).astype(BF16)

    f = f_scr[...]
    act = _silu(_dot(f, wg_ref[...])) * _dot(f, wu_ref[...])
    o_ref[...] += _dot(act.astype(BF16), wd_ref[...])

    @pl.when(j == pl.num_programs(1) - 1)
    def _():
        o_ref[...] = _rms(o_ref[...], gl_ref[...])


def _out_ffn(x2, att2, y2, woa, woy, gf, wg, wu, wd, gl, *, tm, th):
    m, d = x2.shape
    hidden = wg.shape[1]
    resident = dict(pipeline_mode=pl.Buffered(1))
    return pl.pallas_call(
        _ffn_body,
        grid=(m // tm, hidden // th),
        in_specs=[
            pl.BlockSpec((tm, d), lambda i, j: (i, 0)),
            pl.BlockSpec((tm, ATT_WIDTH), lambda i, j: (i, 0)),
            pl.BlockSpec((tm, SSM_WIDTH), lambda i, j: (i, 0)),
            pl.BlockSpec((ATT_WIDTH, d), lambda i, j: (0, 0), **resident),
            pl.BlockSpec((SSM_WIDTH, d), lambda i, j: (0, 0), **resident),
            pl.BlockSpec((1, d), lambda i, j: (0, 0)),
            pl.BlockSpec((d, th), lambda i, j: (0, j)),
            pl.BlockSpec((d, th), lambda i, j: (0, j)),
            pl.BlockSpec((th, d), lambda i, j: (j, 0)),
            pl.BlockSpec((1, d), lambda i, j: (0, 0)),
        ],
        out_specs=pl.BlockSpec((tm, d), lambda i, j: (i, 0)),
        out_shape=jax.ShapeDtypeStruct((m, d), F32),
        scratch_shapes=[pltpu.VMEM((tm, d), BF16)],
        compiler_params=pltpu.CompilerParams(
            dimension_semantics=("parallel", "arbitrary"), vmem_limit_bytes=VMEM_LIMIT),
        name="out_ffn",
    )(x2, att2, y2, woa, woy, gf, wg, wu, wd, gl)


def _t5_bucket(rel):
    max_exact = N_BUCKETS // 2
    relf = jnp.maximum(rel, 1).astype(jnp.float32)
    large = max_exact + (jnp.log(relf / max_exact) / math.log(MAX_DISTANCE / max_exact)
                         * (N_BUCKETS - max_exact)).astype(jnp.int32)
    large = jnp.minimum(large, N_BUCKETS - 1)
    return jnp.where(rel < max_exact, rel, large)


def _toeplitz(u, rows, cols):
    nh, period = u.shape
    assert cols < period
    flat = jnp.tile(u, (1, rows))
    return flat[:, :rows * (period - 1)].reshape(nh, rows, period - 1)[:, :, :cols]


def _bias_tiles(rel_bias, rows, key_major):
    assert MAX_DISTANCE <= LANES and rows <= LANES
    table = rel_bias[_t5_bucket(jnp.arange(3 * LANES, dtype=jnp.int32))].T
    tiles = []
    for dd in (2, 1, 0):
        neg = table[:, (dd - 1) * LANES:dd * LANES] if dd >= 1 else jnp.tile(table[:, 0:1], (1, LANES))
        u = jnp.concatenate([table[:, dd * LANES:(dd + 1) * LANES], neg], axis=1)
        if key_major:
            tiles.append(_toeplitz(u, LANES, rows))
        else:
            w = jnp.concatenate([u[:, 0:1], jnp.flip(u[:, 1:], axis=1)], axis=1)
            tiles.append(_toeplitz(w, rows, LANES))
    return jnp.stack(tiles)


def _layer(x, att_call, ssd_call, blk_rows, p, *, tm, kv_bf16):
    b, t, d = x.shape
    m = b * t
    tm = min(tm, m)
    proj = _inproj(x.reshape(m, d), p["norm_attn"], p["wb"], p["ws"], p["idx_k_norm"], tm=tm,
                   seq_len=t if kv_bf16 else None)
    big, bc, small, kin = proj[:4]
    big3 = big.reshape(b, t, BIG_WIDTH)
    bc3 = bc.reshape(b, t, BC_WIDTH)
    small3 = small.reshape(b, t, SMALL_WIDTH)
    kin3 = kin.reshape(b, t, IDX_DIM)
    hd = (b, t, N_ATT_HEADS, ATT_HEAD_DIM)
    k4 = big3[:, :, COL_K:COL_K + ATT_WIDTH].reshape(hd)
    v4 = big3[:, :, COL_V:COL_V + ATT_WIDTH].reshape(hd)
    small3_t = jnp.transpose(small.reshape(m // blk_rows, blk_rows, SMALL_WIDTH), (0, 2, 1))
    att = att_call(big3, small3, small3_t, kin3, k4, v4, *proj[4:])
    y, tail, h_fin = ssd_call(big3, bc3, small3, small3_t)
    out = _out_ffn(x.reshape(m, d), att.reshape(m, ATT_WIDTH), y.reshape(m, SSM_WIDTH), p["woa"], p["woy"],
                   p["norm_ffn"], p["wg"], p["wu"], p["wd"], p["norm_final"], tm=tm, th=p["th"])
    return out.reshape(b, t, d), k4, v4, kin3, tail, h_fin


def kernel(x_prompt, x_sample, cache_k, cache_v, cache_kidx, state_conv, state_ssm, page_table, rel_bias,
           norm_attn, w_in, idx_k_norm, conv_w, conv_b, dt_bias, a_log, d_skip, ssm_norm, w_out, norm_ffn,
           w_gate, w_up, w_down, norm_final):
    depth = w_in.shape[0]
    assert depth == 1
    bp, s, d = x_prompt.shape
    bs, t, _ = x_sample.shape
    n_pages = page_table.shape[1]
    past = n_pages * PAGE_SIZE
    lyr = 0

    w = w_in[lyr]
    o_ki = 3 * ATT_WIDTH + N_IDX_HEADS * IDX_DIM
    o_z = o_ki + IDX_DIM + N_IDX_HEADS
    o_dt = o_z + SSM_WIDTH + CONV_DIM
    o_x = o_z + SSM_WIDTH
    o_bc = o_x + SSM_WIDTH
    wb = jnp.concatenate([w[:, :o_ki], w[:, o_z:o_bc]], axis=1).astype(BF16)
    ws = jnp.concatenate([w[:, o_bc:o_dt], w[:, o_ki:o_z], w[:, o_dt:],
                          jnp.zeros((d, SMALL_WIDTH - (o_z - o_ki) - N_SSM_HEADS), F32)], axis=1).astype(BF16)
    row = lambda v: v.reshape(1, -1)
    p = dict(
        norm_attn=row(norm_attn[lyr]), wb=wb, ws=ws, idx_k_norm=row(idx_k_norm[lyr]),
        conv_w=conv_w[lyr], conv_b=conv_b[lyr], dt_bias=dt_bias[lyr], a_log=a_log[lyr], d_skip=d_skip[lyr],
        ssm_norm=ssm_norm[lyr],
        woa=w_out[lyr, :ATT_WIDTH].astype(BF16), woy=w_out[lyr, ATT_WIDTH:].astype(BF16),
        norm_ffn=row(norm_ffn[lyr]), wg=w_gate[lyr].astype(BF16), wu=w_up[lyr].astype(BF16),
        wd=w_down[lyr].astype(BF16), norm_final=row(norm_final),
        th=512,
    )

    topk_p = min(TOPK_MAX, s // 4)
    assert SSD_CHUNK == LANES and s % LANES == 0
    tiles_p = _bias_tiles(rel_bias, LANES, key_major=True)
    tiles_p = tiles_p - tiles_p[0:1]

    def att_prompt(big3, small3, small3_t, kin3, k4, v4, k16, vt16):
        return _attn_prompt(big3, small3_t, kin3, k16.reshape(bp, s, ATT_WIDTH), vt16, tiles_p,
                            topk=topk_p, group=4)

    zero_conv = jnp.zeros((bp, CONV_W - 1, CONV_DIM), F32)
    zero_ssm = jnp.zeros((bp, N_SSM_HEADS, SSM_HEAD_DIM, SSM_STATE), F32)
    ssm_params = (p["conv_w"], p["conv_b"], p["dt_bias"], p["a_log"], p["d_skip"], p["ssm_norm"])

    def ssd_prompt(big3, bc3, small3, small3_t):
        return _ssd(big3, bc3, small3, small3_t, zero_conv, zero_ssm, *ssm_params, l=SSD_CHUNK)

    yp, kp, vp, kip, cp, sp = _layer(x_prompt, att_prompt, ssd_prompt, SSD_CHUNK, p, tm=512, kv_bf16=True)

    topk_s = min(TOPK_MAX, (past + t) // 4)
    tiles_s = jnp.repeat(_bias_tiles(rel_bias, t, key_major=False).reshape(3, N_ATT_HEADS * t, LANES),
                         N_ATT_HEADS, axis=-1)

    cache_kidx_t = jnp.swapaxes(cache_kidx, -1, -2)

    gs = math.gcd(bs, LANES // t)

    def att_sample(big3, small3, small3_t, kin3, k4, v4):
        sel = _select_sample(page_table, big3, small3, kin3, cache_kidx_t, topk=topk_s, gs=gs)
        return _attn_sample(page_table, big3, k4, v4, sel, tiles_s, cache_k, cache_v, gs=gs)

    def ssd_sample(big3, bc3, small3, small3_t):
        return _ssd_sample(big3, bc3, small3, small3_t, state_conv[lyr], state_ssm[lyr], *ssm_params, gs=gs)

    ys, ks, vs, kis, cs, ss = _layer(x_sample, att_sample, ssd_sample, gs * t, p, tm=512, kv_bf16=False)

    st = lambda a: a[None]
    return (yp, ys, st(kp), st(vp), st(kip), st(cp), st(sp), st(ks), st(vs), st(kis), st(cs), st(ss))
```

```python
import functools
import math

import numpy as np
import jax
import jax.numpy as jnp
from jax import lax
from jax.experimental import pallas as pl
from jax.experimental.pallas import tpu as pltpu

N_ATT_HEADS = 8
ATT_HEAD_DIM = 128
ATT_WIDTH = N_ATT_HEADS * ATT_HEAD_DIM
N_IDX_HEADS = 16
IDX_DIM = 64
TOPK_MAX = 256
N_SSM_HEADS = 16
SSM_HEAD_DIM = 64
SSM_WIDTH = N_SSM_HEADS * SSM_HEAD_DIM
N_SSM_GROUPS = 2
SSM_STATE = 128
CONV_W = 4
CONV_DIM = SSM_WIDTH + 2 * N_SSM_GROUPS * SSM_STATE
SSD_CHUNK = 128
PAGE_SIZE = 128
N_BUCKETS = 32
MAX_DISTANCE = 128
NORM_EPS = 1e-6

LANES = 128
SUBLANES = 8
VMEM_LIMIT = 56 * 1024 * 1024

ROW_TILE = 512
FFN_TILE = 512
CAUSAL_GROUP = 2

COL_Q, COL_K, COL_V, COL_QI, COL_Z, COL_X = (i * ATT_WIDTH for i in range(6))
BIG_WIDTH = 6 * ATT_WIDTH
GROUP_WIDTH = N_SSM_GROUPS * SSM_STATE
BC_WIDTH = 2 * GROUP_WIDTH
SM_KI, SM_WI, SM_DT = 0, IDX_DIM, IDX_DIM + N_IDX_HEADS
SMALL_WIDTH = LANES
INPROJ_TN = 2 * ATT_WIDTH

F32 = jnp.float32
BF16 = jnp.bfloat16

_INT_MIN = -(2 ** 31)
_INT_MAX = 2 ** 31 - 1
_KEY_NEG_INF = int(np.array([-np.inf], np.float32).view(np.int32)[0]) ^ 0x7FFFFFFF
_NEG_BIG = -1e30


def _dot(a, b):
    return jnp.dot(a, b, preferred_element_type=F32)


def _dot_nt(a, b):
    return lax.dot_general(a, b, (((1,), (1,)), ((), ())), preferred_element_type=F32)


def _dot_tn(a, b):
    return lax.dot_general(a, b, (((0,), (0,)), ((), ())), preferred_element_type=F32)


def _rms(x, g):
    return x * lax.rsqrt(jnp.mean(x * x, axis=-1, keepdims=True) + NORM_EPS) * g


def _silu(x):
    return x * (1.0 / (1.0 + jnp.exp(-x)))


def _inproj_body(x_ref, g_ref, wb_ref, ws_ref, gk_ref, big_ref, bc_ref, small_ref, kin_ref, *rest, kv_bf16):
    u_scr = rest[-1]
    n = pl.program_id(1)

    @pl.when(n == 0)
    def _():
        ub = _rms(x_ref[...], g_ref[...]).astype(BF16)
        u_scr[...] = ub
        nar = _dot_nt(ub, ws_ref[...])
        bc_ref[...] = nar[:, :BC_WIDTH]
        sm = nar[:, BC_WIDTH:]
        small_ref[...] = sm
        kin_ref[...] = _rms(sm[:, SM_KI:SM_KI + IDX_DIM], gk_ref[...])

    tile = _dot_nt(u_scr[...], wb_ref[...])
    big_ref[...] = tile
    if kv_bf16:
        k16_ref, vt16_ref = rest[:2]

        @pl.when(n == COL_K // INPROJ_TN)
        def _():
            k16_ref[...] = tile[:, COL_K % INPROJ_TN:COL_K % INPROJ_TN + ATT_WIDTH].astype(BF16)

        @pl.when(n == COL_V // INPROJ_TN)
        def _():
            vt16_ref[0] = tile[:, COL_V % INPROJ_TN:COL_V % INPROJ_TN + ATT_WIDTH].T.astype(BF16)


def _inproj(x2, g, wb, ws, gk, *, tm, seq_len=None):
    m, d = x2.shape
    kv_bf16 = seq_len is not None
    out_specs = [
        pl.BlockSpec((tm, INPROJ_TN), lambda i, j: (i, j)),
        pl.BlockSpec((tm, BC_WIDTH), lambda i, j: (i, 0)),
        pl.BlockSpec((tm, SMALL_WIDTH), lambda i, j: (i, 0)),
        pl.BlockSpec((tm, IDX_DIM), lambda i, j: (i, 0)),
    ]
    out_shape = [
        jax.ShapeDtypeStruct((m, BIG_WIDTH), F32),
        jax.ShapeDtypeStruct((m, BC_WIDTH), F32),
        jax.ShapeDtypeStruct((m, SMALL_WIDTH), F32),
        jax.ShapeDtypeStruct((m, IDX_DIM), F32),
    ]
    if kv_bf16:
        assert seq_len % tm == 0
        per_seq = seq_len // tm
        out_specs += [
            pl.BlockSpec((tm, ATT_WIDTH), lambda i, j: (i, 0)),
            pl.BlockSpec((1, ATT_WIDTH, tm), lambda i, j: (i // per_seq, 0, i % per_seq)),
        ]
        out_shape += [
            jax.ShapeDtypeStruct((m, ATT_WIDTH), BF16),
            jax.ShapeDtypeStruct((m // seq_len, ATT_WIDTH, seq_len), BF16),
        ]
    return pl.pallas_call(
        functools.partial(_inproj_body, kv_bf16=kv_bf16),
        grid=(m // tm, BIG_WIDTH // INPROJ_TN),
        in_specs=[
            pl.BlockSpec((tm, d), lambda i, j: (i, 0)),
            pl.BlockSpec((1, d), lambda i, j: (0, 0)),
            pl.BlockSpec((INPROJ_TN, d), lambda i, j: (j, 0)),
            pl.BlockSpec((BC_WIDTH + SMALL_WIDTH, d), lambda i, j: (0, 0), pipeline_mode=pl.Buffered(1)),
            pl.BlockSpec((1, IDX_DIM), lambda i, j: (0, 0)),
        ],
        out_specs=out_specs,
        out_shape=out_shape,
        scratch_shapes=[pltpu.VMEM((tm, d), BF16)],
        compiler_params=pltpu.CompilerParams(
            dimension_semantics=("parallel", "arbitrary"), vmem_limit_bytes=VMEM_LIMIT),
        name="inproj",
    )(x2, g, wb, ws, gk)


def _sortable_key(s):
    s = jnp.where(s == 0.0, 0.0, s)
    b = pltpu.bitcast(s, jnp.int32)
    return b ^ ((b >> 31) & 0x7FFFFFFF)


def _count_keys(key_ref, nblk, pred, key_axis):
    def body(kb, acc):
        return acc + jnp.where(pred(key_ref[kb], kb), 1.0, 0.0)

    acc = jnp.zeros(key_ref.shape[1:], F32)
    if isinstance(nblk, int):
        for kb in range(nblk):
            acc = body(kb, acc)
    else:
        acc = lax.fori_loop(0, nblk, body, acc)
    return jnp.sum(acc, axis=key_axis, keepdims=True)


def _topk_select_params(key_ref, nblk, topk, key_axis):
    shape = key_ref.shape[1:]
    qshape = tuple(1 if a == key_axis else n for a, n in enumerate(shape))
    blk = shape[key_axis]
    kf = float(topk)

    def count_ge(cand):
        cb = jnp.broadcast_to(cand, shape)
        return _count_keys(key_ref, nblk, lambda k, kb: k >= cb, key_axis)

    res = jnp.where(count_ge(jnp.zeros(qshape, jnp.int32)) >= kf, 0, _INT_MIN).astype(jnp.int32)

    def bit_body(it, res):
        cand = res | jnp.left_shift(jnp.int32(1), 30 - it)
        return jnp.where(count_ge(cand) >= kf, cand, res)

    thr = lax.fori_loop(0, 31, bit_body, res)
    thr_b = jnp.broadcast_to(thr, shape)
    n_gt = _count_keys(key_ref, nblk, lambda k, kb: k > thr_b, key_axis)
    n_ge = count_ge(thr)
    tied = jnp.logical_and(n_ge > kf, thr > _KEY_NEG_INF)
    any_tied = jnp.max(jnp.where(tied, 1.0, 0.0)) > 0.0
    need = kf - n_gt
    kpos = lax.broadcasted_iota(jnp.int32, shape, key_axis)

    def tie_break():
        def body(it, cut):
            cand = cut | jnp.left_shift(jnp.int32(1), 30 - it)
            cand_b = jnp.broadcast_to(cand, shape)
            n = _count_keys(key_ref, nblk,
                            lambda k, kb: jnp.logical_and(k == thr_b, kpos + kb * blk < cand_b), key_axis)
            return jnp.where(n <= need, cand, cut)

        return lax.fori_loop(0, 31, body, jnp.zeros(qshape, jnp.int32))

    cut = lax.cond(any_tied, tie_break, lambda: jnp.full(qshape, _INT_MAX, jnp.int32))
    return thr, cut


def _attn_prompt_body(q_ref, qi_ref, smt_ref, kin_ref, k_ref, vt_ref, bias_ref, o_ref,
                      key_scr, mask_scr, lg_scr, qi_scr, *, topk, group):
    i = pl.program_id(1)
    q0 = i * LANES
    blk = (LANES, LANES)
    scale = ATT_HEAD_DIM ** -0.5
    nq = key_scr.shape[0]

    w_t = smt_ref[0, SM_WI:SM_WI + N_IDX_HEADS, :]
    qi_all = qi_ref[0]
    for h in range(N_IDX_HEADS):
        qi_scr[h * LANES:(h + 1) * LANES, :] = qi_all[:, h * IDX_DIM:(h + 1) * IDX_DIM].astype(BF16)
    qi = qi_scr[...]
    kpos = lax.broadcasted_iota(jnp.int32, blk, 0)
    qpos = lax.broadcasted_iota(jnp.int32, blk, 1) + q0
    qb = q_ref[0].astype(BF16)

    def tile(nb, first_near):
        for kb in range(nb):
            kin = kin_ref[0, kb * LANES:(kb + 1) * LANES, :].astype(BF16)
            s = _dot_nt(kin, qi)
            score = jnp.zeros(blk, F32)
            for h in range(N_IDX_HEADS):
                score = score + jnp.maximum(s[:, h * LANES:(h + 1) * LANES], 0.0) * w_t[h:h + 1, :]
            if kb >= first_near:
                score = jnp.where(kpos + kb * LANES <= qpos, score, -jnp.inf)
            key_scr[kb] = _sortable_key(score)

        thr, cut = _topk_select_params(key_scr, nb, topk, key_axis=0)
        thr_b = jnp.broadcast_to(thr, blk)
        cut_b = jnp.broadcast_to(cut, blk)
        for kb in range(nb):
            kp = kpos + kb * LANES
            key = key_scr[kb]
            valid = jnp.logical_or(key > thr_b, jnp.logical_and(key == thr_b, kp < cut_b))
            if kb >= first_near:
                valid = jnp.logical_and(valid, kp <= qpos)
            mask_scr[kb * LANES:(kb + 1) * LANES, :] = jnp.where(valid, 0.0, -jnp.inf)

        nk = nb * LANES
        for h in range(N_ATT_HEADS):
            hs = slice(h * ATT_HEAD_DIM, (h + 1) * ATT_HEAD_DIM)
            lg = _dot_nt(k_ref[0, 0:nk, hs], qb[:, hs]) * scale + mask_scr[0:nk, :]
            for kb in range(first_near, nb):
                tix = 2 - jnp.clip(i - kb, 0, 2)
                lg_scr[kb * LANES:(kb + 1) * LANES, :] = lg[kb * LANES:(kb + 1) * LANES, :] + bias_ref[tix, h]
            if first_near > 0:
                lg_scr[0:first_near * LANES, :] = lg[0:first_near * LANES, :]
            lg = lg_scr[0:nk, :]
            m = jnp.max(lg, axis=0, keepdims=True)
            p = jnp.exp(lg - m)
            l = jnp.sum(p, axis=0, keepdims=True)
            out_t = _dot(vt_ref[0, hs, 0:nk], p.astype(BF16)) / l
            o_ref[0, :, hs] = out_t.T.astype(o_ref.dtype)

    ngroups = -(-nq // group)
    for c in range(ngroups):
        nb = min((c + 1) * group, nq)

        @pl.when(i // group == c)
        def _(nb=nb, c=c):
            tile(nb, max(c * group - 1, 0))


def _attn_prompt(big3, small3_t, kin3, k16, vt16, bias_tiles_t, *, topk, group):
    b, s, _ = big3.shape
    nq = s // LANES
    body = functools.partial(_attn_prompt_body, topk=topk, group=group)
    return pl.pallas_call(
        body,
        grid=(b, nq),
        in_specs=[
            pl.BlockSpec((1, LANES, ATT_WIDTH), lambda bi, i: (bi, i, COL_Q // ATT_WIDTH)),
            pl.BlockSpec((1, LANES, N_IDX_HEADS * IDX_DIM), lambda bi, i: (bi, i, COL_QI // (N_IDX_HEADS * IDX_DIM))),
            pl.BlockSpec((1, SMALL_WIDTH, LANES), lambda bi, i: (bi * nq + i, 0, 0)),
            pl.BlockSpec((1, s, IDX_DIM), lambda bi, i: (bi, 0, 0)),
            pl.BlockSpec((1, s, ATT_WIDTH), lambda bi, i: (bi, 0, 0)),
            pl.BlockSpec((1, ATT_WIDTH, s), lambda bi, i: (bi, 0, 0)),
            pl.BlockSpec(bias_tiles_t.shape, lambda bi, i: (0, 0, 0, 0)),
        ],
        out_specs=pl.BlockSpec((1, LANES, ATT_WIDTH), lambda bi, i: (bi, i, 0)),
        out_shape=jax.ShapeDtypeStruct((b, s, ATT_WIDTH), BF16),
        scratch_shapes=[
            pltpu.VMEM((nq, LANES, LANES), jnp.int32),
            pltpu.VMEM((s, LANES), F32),
            pltpu.VMEM((s, LANES), F32),
            pltpu.VMEM((N_IDX_HEADS * LANES, IDX_DIM), BF16),
        ],
        compiler_params=pltpu.CompilerParams(
            dimension_semantics=("parallel", "arbitrary"), vmem_limit_bytes=VMEM_LIMIT),
        name="attn_prompt",
    )(big3, big3, small3_t, kin3, k16, vt16, bias_tiles_t)


def _select_sample_body(pt_ref, qi_ref, sm_ref, kinew_ref, *rest, n_pages, t, topk):
    kidx_refs = rest[:n_pages]
    sel_ref = rest[n_pages]
    key_scr, wb_scr = rest[n_pages + 1:]
    nblk = n_pages + 1
    j = pl.program_id(1)
    r0 = pl.multiple_of(j * t, t)

    wi = sm_ref[0, :, SM_WI:SM_WI + N_IDX_HEADS]
    for h in range(N_IDX_HEADS):
        wb_scr[h] = jnp.broadcast_to(wi[:, h:h + 1], (t, LANES))
    qi_all = qi_ref[0]
    qi = jnp.concatenate([qi_all[:, h * IDX_DIM:(h + 1) * IDX_DIM] for h in range(N_IDX_HEADS)],
                         axis=0).astype(BF16)

    def scores(s):
        return jnp.sum(jnp.maximum(s.reshape(N_IDX_HEADS, t, LANES), 0.0) * wb_scr[...], axis=0)

    for p in range(n_pages):
        key_scr[p, pl.ds(r0, t), :] = _sortable_key(scores(_dot(qi, kidx_refs[p][0, 0].astype(BF16))))
    kin_new = jnp.concatenate([kinew_ref[0], jnp.zeros((LANES - t, IDX_DIM), F32)], axis=0)
    trow = lax.broadcasted_iota(jnp.int32, (t, LANES), 0)
    lane_t = lax.broadcasted_iota(jnp.int32, (t, LANES), 1)
    s_new = jnp.where(lane_t <= trow, scores(_dot_nt(qi, kin_new.astype(BF16))), -jnp.inf)
    key_scr[n_pages, pl.ds(r0, t), :] = _sortable_key(s_new)

    @pl.when(j == pl.num_programs(1) - 1)
    def _():
        rows = key_scr.shape[1]
        thr, cut = _topk_select_params(key_scr, nblk, topk, key_axis=1)
        thr_b = jnp.broadcast_to(thr, (rows, LANES))
        cut_b = jnp.broadcast_to(cut, (rows, LANES))
        lane = lax.broadcasted_iota(jnp.int32, (rows, LANES), 1)
        tq = lax.broadcasted_iota(jnp.int32, (rows, LANES), 0) & (t - 1)
        for p in range(nblk):
            key = key_scr[p]
            sel = jnp.logical_or(key > thr_b, jnp.logical_and(key == thr_b, lane + p * LANES < cut_b))
            if p == n_pages:
                sel = jnp.logical_and(sel, lane <= tq)
            sel_ref[0, p] = jnp.where(sel, 1.0, 0.0)


def _select_sample(page_table, big3, small3, kin3, cache_kidx_t, *, topk, gs):
    b, t, _ = small3.shape
    n_pages = page_table.shape[1]
    assert b % gs == 0 and t & (t - 1) == 0
    body = functools.partial(_select_sample_body, n_pages=n_pages, t=t, topk=topk)
    seq = lambda g, j: g * gs + j
    qi_width = N_IDX_HEADS * IDX_DIM
    in_specs = [
        pl.BlockSpec((1, t, qi_width), lambda g, j, pt: (seq(g, j), 0, COL_QI // qi_width)),
        pl.BlockSpec((1, t, SMALL_WIDTH), lambda g, j, pt: (seq(g, j), 0, 0)),
        pl.BlockSpec((1, t, IDX_DIM), lambda g, j, pt: (seq(g, j), 0, 0)),
    ]
    in_specs += [pl.BlockSpec((1, 1, IDX_DIM, PAGE_SIZE), lambda g, j, pt, p=p: (0, pt[seq(g, j), p], 0, 0))
                 for p in range(n_pages)]
    sel_shape = (b // gs, n_pages + 1, gs * t, LANES)
    return pl.pallas_call(
        body,
        grid_spec=pltpu.PrefetchScalarGridSpec(
            num_scalar_prefetch=1,
            grid=(b // gs, gs),
            in_specs=in_specs,
            out_specs=pl.BlockSpec((1,) + sel_shape[1:], lambda g, j, pt: (g, 0, 0, 0)),
            scratch_shapes=[
                pltpu.VMEM(sel_shape[1:], jnp.int32),
                pltpu.VMEM((N_IDX_HEADS, t, LANES), F32),
            ],
        ),
        out_shape=jax.ShapeDtypeStruct(sel_shape, F32),
        compiler_params=pltpu.CompilerParams(
            dimension_semantics=("parallel", "arbitrary"), vmem_limit_bytes=VMEM_LIMIT),
        name="select_sample",
    )(page_table, big3, small3, kin3, *([cache_kidx_t] * n_pages))


def _attn_sample_body(pt_ref, q_ref, knew_ref, vnew_ref, sel_ref, bias_ref, *rest, n_pages, t):
    k_refs = rest[:n_pages]
    v_refs = rest[n_pages:2 * n_pages]
    o_ref = rest[2 * n_pages]
    (lg_scr,) = rest[2 * n_pages + 1:]
    nblk = n_pages + 1
    rows = N_ATT_HEADS * t
    flat = PAGE_SIZE * N_ATT_HEADS
    scale = ATT_HEAD_DIM ** -0.5

    q = q_ref[0]
    q64 = jnp.concatenate([q[:, h * ATT_HEAD_DIM:(h + 1) * ATT_HEAD_DIM] for h in range(N_ATT_HEADS)],
                          axis=0).astype(BF16)
    rr = lax.broadcasted_iota(jnp.int32, (rows, flat), 0)
    cc = lax.broadcasted_iota(jnp.int32, (rows, flat), 1)
    head_match = (cc & (N_ATT_HEADS - 1)) == (rr >> (t.bit_length() - 1))
    er = lax.broadcasted_iota(jnp.int32, (LANES, flat), 0)
    ec = lax.broadcasted_iota(jnp.int32, (LANES, flat), 1)
    expand = jnp.where((ec >> (N_ATT_HEADS.bit_length() - 1)) == er, 1.0, 0.0).astype(BF16)

    def flat_kv(refs, p):
        if p < n_pages:
            return refs[p][0, 0].reshape(flat, ATT_HEAD_DIM).astype(BF16), flat
        new = refs[p][0].reshape(t * N_ATT_HEADS, ATT_HEAD_DIM)
        pad = jnp.zeros((LANES - t * N_ATT_HEADS, ATT_HEAD_DIM), F32)
        return jnp.concatenate([new, pad], axis=0).astype(BF16), LANES

    k_all = list(k_refs) + [knew_ref]
    v_all = list(v_refs) + [vnew_ref]
    m = jnp.full((rows, 1), _NEG_BIG, F32)
    for p in range(nblk):
        kflat, width = flat_kv(k_all, p)
        selx = _dot(sel_ref[0, p].astype(BF16), expand[:, :width])
        valid = jnp.logical_and(head_match[:, :width], jnp.concatenate([selx] * N_ATT_HEADS, axis=0) > 0.5)
        tix = 2 - min(n_pages - p, 2)
        lg = _dot_nt(q64, kflat) * scale + bias_ref[tix, :, :width]
        lg = jnp.where(valid, lg, _NEG_BIG)
        lg_scr[p, :, :width] = lg
        m = jnp.maximum(m, jnp.max(lg, axis=-1, keepdims=True))

    l = jnp.zeros((rows, 1), F32)
    acc = jnp.zeros((rows, ATT_HEAD_DIM), F32)
    for p in range(nblk):
        vflat, width = flat_kv(v_all, p)
        pr = jnp.exp(lg_scr[p, :, :width] - m)
        l = l + jnp.sum(pr, axis=-1, keepdims=True)
        acc = acc + _dot(pr.astype(BF16), vflat)
    out = acc / l
    for h in range(N_ATT_HEADS):
        o_ref[0, :, h * ATT_HEAD_DIM:(h + 1) * ATT_HEAD_DIM] = out[h * t:(h + 1) * t, :].astype(o_ref.dtype)


def _attn_sample(page_table, big3, k_new, v_new, sel, bias_rows, cache_k, cache_v, *, gs):
    b, t, _ = big3.shape
    n_pages = page_table.shape[1]
    assert t == SUBLANES and N_ATT_HEADS & (N_ATT_HEADS - 1) == 0 and t * N_ATT_HEADS <= LANES
    body = functools.partial(_attn_sample_body, n_pages=n_pages, t=t)
    kv_block = (1, 1, PAGE_SIZE, N_ATT_HEADS, ATT_HEAD_DIM)
    new_block = (1, t, N_ATT_HEADS, ATT_HEAD_DIM)

    in_specs = [
        pl.BlockSpec((1, t, ATT_WIDTH), lambda bi, pt: (bi, 0, COL_Q // ATT_WIDTH)),
        pl.BlockSpec(new_block, lambda bi, pt: (bi, 0, 0, 0)),
        pl.BlockSpec(new_block, lambda bi, pt: (bi, 0, 0, 0)),
        pl.BlockSpec((1, n_pages + 1, t, LANES), lambda bi, pt: (bi // gs, 0, bi % gs, 0)),
        pl.BlockSpec(bias_rows.shape, lambda bi, pt: (0, 0, 0)),
    ]
    in_specs += [pl.BlockSpec(kv_block, lambda bi, pt, p=p: (0, pt[bi, p], 0, 0, 0)) for p in range(n_pages)] * 2
    rows = N_ATT_HEADS * t
    return pl.pallas_call(
        body,
        grid_spec=pltpu.PrefetchScalarGridSpec(
            num_scalar_prefetch=1,
            grid=(b,),
            in_specs=in_specs,
            out_specs=pl.BlockSpec((1, t, ATT_WIDTH), lambda bi, pt: (bi, 0, 0)),
            scratch_shapes=[pltpu.VMEM((n_pages + 1, rows, PAGE_SIZE * N_ATT_HEADS), F32)],
        ),
        out_shape=jax.ShapeDtypeStruct((b, t, ATT_WIDTH), BF16),
        compiler_params=pltpu.CompilerParams(
            dimension_semantics=("arbitrary",), vmem_limit_bytes=VMEM_LIMIT),
        name="attn_sample",
    )(page_table, big3, k_new, v_new, sel, bias_rows, *([cache_k] * n_pages), *([cache_v] * n_pages))


def _split3(a):
    a1 = a.astype(BF16)
    r1 = a - a1.astype(F32)
    a2 = r1.astype(BF16)
    a3 = (r1 - a2.astype(F32)).astype(BF16)
    return a1, a2, a3


def _softplus(v):
    return jnp.maximum(v, 0.0) + jnp.log1p(jnp.exp(-jnp.abs(v)))


def _head_expander():
    r = lax.broadcasted_iota(jnp.int32, (SMALL_WIDTH, SSM_WIDTH), 0)
    c = lax.broadcasted_iota(jnp.int32, (SMALL_WIDTH, SSM_WIDTH), 1)
    return jnp.where(r - SM_DT == c // SSM_HEAD_DIM, 1.0, 0.0).astype(BF16)


def _ssd_body(x_ref, bm_ref, cm_ref, z_ref, sm_ref, smt_ref, pre_ref, h0_ref,
              cw_ref, cb_ref, dtb_ref, dtbt_ref, alog_ref, alogt_ref, dskipx_ref, gn_ref,
              y_ref, tail_ref, hout_ref, xp_scr, ht_scr, y_scr, *, l):
    c = pl.program_id(1)
    nc = pl.num_programs(1)
    off = SUBLANES
    heads_per_group = N_SSM_HEADS // N_SSM_GROUPS
    gcols = heads_per_group * SSM_HEAD_DIM

    @pl.when(c == 0)
    def _():
        xp_scr[off - (CONV_W - 1):off, :] = pre_ref[0]
        for h in range(N_SSM_HEADS):
            g, hh = divmod(h, heads_per_group)
            ht_scr[g, :, hh * SSM_HEAD_DIM:(hh + 1) * SSM_HEAD_DIM] = h0_ref[0, h].T

    xp_scr[off:off + l, 0:SSM_WIDTH] = x_ref[0]
    xp_scr[off:off + l, SSM_WIDTH:SSM_WIDTH + GROUP_WIDTH] = bm_ref[0]
    xp_scr[off:off + l, SSM_WIDTH + GROUP_WIDTH:CONV_DIM] = cm_ref[0]

    conv = cb_ref[...]
    for j in range(CONV_W):
        s0 = off - (CONV_W - 1) + j
        conv = conv + xp_scr[s0:s0 + l, :] * cw_ref[j:j + 1, :]
    xc = _silu(conv)
    tail = xp_scr[off + l - (CONV_W - 1):off + l, :]
    tail_ref[0] = tail
    xp_scr[off - (CONV_W - 1):off, :] = tail

    xs = xc[:, :SSM_WIDTH]
    bc = [xc[:, SSM_WIDTH + g * SSM_STATE:SSM_WIDTH + (g + 1) * SSM_STATE].astype(BF16)
          for g in range(N_SSM_GROUPS)]
    cc = [xc[:, SSM_WIDTH + GROUP_WIDTH + g * SSM_STATE:SSM_WIDTH + GROUP_WIDTH + (g + 1) * SSM_STATE]
          .astype(BF16) for g in range(N_SSM_GROUPS)]

    dt = _softplus(sm_ref[0] + dtb_ref[...])
    a = dt * (-jnp.exp(alog_ref[...]))
    dtt = _softplus(smt_ref[0, SM_DT:SM_DT + N_SSM_HEADS, :] + dtbt_ref[...])
    at = dtt * (-jnp.exp(alogt_ref[...]))

    ri = lax.broadcasted_iota(jnp.int32, (l, l), 0)
    ci = lax.broadcasted_iota(jnp.int32, (l, l), 1)
    causal = ri >= ci
    tri = jnp.where(causal, 1.0, 0.0).astype(BF16)
    tri_u = jnp.where(ri <= ci, 1.0, 0.0).astype(BF16)
    acum = sum(_dot(tri, piece) for piece in _split3(a))
    acum_t = sum(_dot(piece, tri_u) for piece in _split3(at))
    last = acum[l - 1:l, :]
    wts = jnp.exp(last - acum) * dt
    expand = _head_expander()
    acum_x = sum(_dot(piece, expand) for piece in _split3(acum))
    wts_x = sum(_dot(piece, expand) for piece in _split3(wts))
    exp_last_x = jnp.exp(acum_x[l - 1:l, :])

    cb = [_dot_nt(cc[g], bc[g]) for g in range(N_SSM_GROUPS)]
    for h in range(N_SSM_HEADS):
        g = h // heads_per_group
        hs = slice(h * SSM_HEAD_DIM, (h + 1) * SSM_HEAD_DIM)
        col = acum[:, SM_DT + h:SM_DT + h + 1]
        decay = jnp.exp(jnp.where(causal, col - acum_t[h:h + 1, :], -jnp.inf))
        mmat = cb[g] * decay * dtt[h:h + 1, :]
        y_scr[:, hs] = _dot(mmat.astype(BF16), xs[:, hs].astype(BF16))

    xw = (xs * wts_x).astype(BF16)
    y_inter = []
    for g in range(N_SSM_GROUPS):
        gsl = slice(g * gcols, (g + 1) * gcols)
        ht = ht_scr[g]
        y_inter.append(_dot(cc[g], ht.astype(BF16)))
        ht_scr[g] = ht * exp_last_x[:, gsl] + _dot_tn(bc[g], xw[:, gsl])
    y = y_scr[...] + jnp.concatenate(y_inter, axis=1) * jnp.exp(acum_x) + dskipx_ref[...] * xs
    y_ref[0] = _rms(y * _silu(z_ref[0]), gn_ref[...]).astype(y_ref.dtype)

    @pl.when(c == nc - 1)
    def _():
        for h in range(N_SSM_HEADS):
            g, hh = divmod(h, heads_per_group)
            hout_ref[0, h] = ht_scr[g, :, hh * SSM_HEAD_DIM:(hh + 1) * SSM_HEAD_DIM].T


def _pad_lanes(v):
    return jnp.zeros((1, SMALL_WIDTH), F32).at[0, SM_DT:SM_DT + N_SSM_HEADS].set(v)


def _ssd(big3, bc3, small3, small3_t, prefix, h0, conv_w, conv_b, dt_bias, a_log, d_skip, ssm_norm, *, l):
    b, t, _ = big3.shape
    nc = t // l
    body = functools.partial(_ssd_body, l=l)
    row = lambda v: v.reshape(1, -1)
    colv = lambda v: v.reshape(-1, 1)
    const2 = lambda shape: pl.BlockSpec(shape, lambda bi, c: (0, 0))
    return pl.pallas_call(
        body,
        grid=(b, nc),
        in_specs=[
            pl.BlockSpec((1, l, SSM_WIDTH), lambda bi, c: (bi, c, COL_X // SSM_WIDTH)),
            pl.BlockSpec((1, l, GROUP_WIDTH), lambda bi, c: (bi, c, 0)),
            pl.BlockSpec((1, l, GROUP_WIDTH), lambda bi, c: (bi, c, 1)),
            pl.BlockSpec((1, l, SSM_WIDTH), lambda bi, c: (bi, c, COL_Z // SSM_WIDTH)),
            pl.BlockSpec((1, l, SMALL_WIDTH), lambda bi, c: (bi, c, 0)),
            pl.BlockSpec((1, SMALL_WIDTH, l), lambda bi, c: (bi * nc + c, 0, 0)),
            pl.BlockSpec((1, CONV_W - 1, CONV_DIM), lambda bi, c: (bi, 0, 0)),
            pl.BlockSpec((1, N_SSM_HEADS, SSM_HEAD_DIM, SSM_STATE), lambda bi, c: (bi, 0, 0, 0)),
            const2((CONV_W, CONV_DIM)),
            const2((1, CONV_DIM)),
            const2((1, SMALL_WIDTH)),
            const2((N_SSM_HEADS, 1)),
            const2((1, SMALL_WIDTH)),
            const2((N_SSM_HEADS, 1)),
            const2((1, SSM_WIDTH)),
            const2((1, SSM_WIDTH)),
        ],
        out_specs=[
            pl.BlockSpec((1, l, SSM_WIDTH), lambda bi, c: (bi, c, 0)),
            pl.BlockSpec((1, CONV_W - 1, CONV_DIM), lambda bi, c: (bi, 0, 0)),
            pl.BlockSpec((1, N_SSM_HEADS, SSM_HEAD_DIM, SSM_STATE), lambda bi, c: (bi, 0, 0, 0)),
        ],
        out_shape=[
            jax.ShapeDtypeStruct((b, t, SSM_WIDTH), BF16),
            jax.ShapeDtypeStruct((b, CONV_W - 1, CONV_DIM), F32),
            jax.ShapeDtypeStruct((b, N_SSM_HEADS, SSM_HEAD_DIM, SSM_STATE), F32),
        ],
        scratch_shapes=[
            pltpu.VMEM((SUBLANES + l, CONV_DIM), F32),
            pltpu.VMEM((N_SSM_GROUPS, SSM_STATE, SSM_WIDTH // N_SSM_GROUPS), F32),
            pltpu.VMEM((l, SSM_WIDTH), F32),
        ],
        compiler_params=pltpu.CompilerParams(
            dimension_semantics=("parallel", "arbitrary"), vmem_limit_bytes=VMEM_LIMIT),
        name="ssd",
    )(big3, bc3, bc3, big3, small3, small3_t, prefix, h0,
      conv_w, row(conv_b), _pad_lanes(dt_bias), colv(dt_bias), _pad_lanes(a_log), colv(a_log),
      row(jnp.repeat(d_skip, SSM_HEAD_DIM)), row(ssm_norm))


def _ssd_sample_body(x_ref, bm_ref, cm_ref, z_ref, sm_ref, smt_ref, pre_ref, h0_ref,
                     cw_ref, cb_ref, dtb_ref, dtbt_ref, alog_ref, alogt_ref, dskip_ref, gn_ref,
                     y_ref, tail_ref, hout_ref, xp_scr, xc_scr, yi_scr, xw_scr, y_scr, *, gs, t):
    rows = gs * t
    off = SUBLANES
    tshift = t.bit_length() - 1
    gw = N_SSM_GROUPS * SSM_STATE
    heads_per_group = N_SSM_HEADS // N_SSM_GROUPS
    gcols = heads_per_group * SSM_HEAD_DIM

    for s in range(gs):
        xp_scr[s, off - (CONV_W - 1):off, :] = pre_ref[s]
        xp_scr[s, off:off + t, 0:SSM_WIDTH] = x_ref[s]
        xp_scr[s, off:off + t, SSM_WIDTH:SSM_WIDTH + gw] = bm_ref[s]
        xp_scr[s, off:off + t, SSM_WIDTH + gw:CONV_DIM] = cm_ref[s]
        conv = cb_ref[...]
        for j in range(CONV_W):
            s0 = off - (CONV_W - 1) + j
            conv = conv + xp_scr[s, s0:s0 + t, :] * cw_ref[j:j + 1, :]
        xc_scr[s * t:(s + 1) * t, :] = _silu(conv)
        tail_ref[s] = xp_scr[s, off + t - (CONV_W - 1):off + t, :]

    xc = xc_scr[...]
    xs = xc[:, :SSM_WIDTH]
    bcf = [xc[:, SSM_WIDTH + g * SSM_STATE:SSM_WIDTH + (g + 1) * SSM_STATE] for g in range(N_SSM_GROUPS)]
    ccf = [xc[:, SSM_WIDTH + gw + g * SSM_STATE:SSM_WIDTH + gw + (g + 1) * SSM_STATE] for g in range(N_SSM_GROUPS)]

    dt = _softplus(sm_ref[...].reshape(rows, SMALL_WIDTH)[:, SM_DT:SM_DT + N_SSM_HEADS] + dtb_ref[...])
    dtt = _softplus(smt_ref[0, SM_DT:SM_DT + N_SSM_HEADS, :] + dtbt_ref[...])
    a = dt * (-jnp.exp(alog_ref[...]))
    at = dtt * (-jnp.exp(alogt_ref[...]))

    ri = lax.broadcasted_iota(jnp.int32, (rows, rows), 0)
    ci = lax.broadcasted_iota(jnp.int32, (rows, rows), 1)
    same = (ri >> tshift) == (ci >> tshift)
    causal = jnp.logical_and(same, ri >= ci)
    tri = jnp.where(causal, 1.0, 0.0).astype(BF16)
    tri_u = jnp.where(jnp.logical_and(same, ri <= ci), 1.0, 0.0).astype(BF16)
    pick_last = jnp.where(ci == (((ri >> tshift) << tshift) + (t - 1)), 1.0, 0.0).astype(BF16)
    acum = sum(_dot(tri, piece) for piece in _split3(a))
    acum_t = sum(_dot(piece, tri_u) for piece in _split3(at))
    last = sum(_dot(pick_last, piece) for piece in _split3(acum))
    wts = jnp.exp(last - acum) * dt
    exp_last = jnp.exp(last)

    for s in range(gs):
        rs = slice(s * t, (s + 1) * t)
        for g in range(N_SSM_GROUPS):
            h0g = h0_ref[s, g * heads_per_group:(g + 1) * heads_per_group].reshape(gcols, SSM_STATE)
            yi_scr[rs, g * gcols:(g + 1) * gcols] = _dot_nt(ccf[g][rs, :], h0g)

    cb = [_dot_nt(ccf[g].astype(BF16), bcf[g].astype(BF16)) for g in range(N_SSM_GROUPS)]
    for h in range(N_SSM_HEADS):
        g = h // heads_per_group
        hs = slice(h * SSM_HEAD_DIM, (h + 1) * SSM_HEAD_DIM)
        col = acum[:, h:h + 1]
        decay = jnp.exp(jnp.where(causal, col - acum_t[h:h + 1, :], -jnp.inf))
        mmat = cb[g] * decay * dtt[h:h + 1, :]
        xh = xs[:, hs]
        y_intra = _dot(mmat.astype(BF16), xh.astype(BF16))
        y_scr[:, hs] = y_intra + yi_scr[:, hs] * jnp.exp(col) + dskip_ref[h] * xh
        xw_scr[:, hs] = xh * wts[:, h:h + 1]

    out = _rms(y_scr[...] * _silu(z_ref[...].reshape(rows, SSM_WIDTH)), gn_ref[...])
    for s in range(gs):
        y_ref[s] = out[s * t:(s + 1) * t, :].astype(y_ref.dtype)

    for s in range(gs):
        rs = slice(s * t, (s + 1) * t)
        for g in range(N_SSM_GROUPS):
            upd = _dot_tn(xw_scr[rs, g * gcols:(g + 1) * gcols], bcf[g][rs, :])
            for hh in range(heads_per_group):
                h = g * heads_per_group + hh
                scale = jnp.broadcast_to(exp_last[s * t:s * t + 1, h:h + 1], (SSM_HEAD_DIM, SSM_STATE))
                hout_ref[s, h] = h0_ref[s, h] * scale + upd[hh * SSM_HEAD_DIM:(hh + 1) * SSM_HEAD_DIM, :]


def _ssd_sample(big3, bc3, small3, small3_t, prefix, h0, conv_w, conv_b, dt_bias, a_log, d_skip, ssm_norm, *, gs):
    b, t, _ = big3.shape
    assert b % gs == 0 and t & (t - 1) == 0 and t == SUBLANES
    rows = gs * t
    body = functools.partial(_ssd_sample_body, gs=gs, t=t)
    gw = N_SSM_GROUPS * SSM_STATE
    row = lambda v: v.reshape(1, -1)
    colv = lambda v: v.reshape(-1, 1)
    const2 = lambda shape: pl.BlockSpec(shape, lambda gi: (0, 0))
    state_block = (gs, N_SSM_HEADS, SSM_HEAD_DIM, SSM_STATE)
    return pl.pallas_call(
        body,
        grid=(b // gs,),
        in_specs=[
            pl.BlockSpec((gs, t, SSM_WIDTH), lambda gi: (gi, 0, COL_X // SSM_WIDTH)),
            pl.BlockSpec((gs, t, gw), lambda gi: (gi, 0, 0)),
            pl.BlockSpec((gs, t, gw), lambda gi: (gi, 0, 1)),
            pl.BlockSpec((gs, t, SSM_WIDTH), lambda gi: (gi, 0, COL_Z // SSM_WIDTH)),
            pl.BlockSpec((gs, t, SMALL_WIDTH), lambda gi: (gi, 0, 0)),
            pl.BlockSpec((1, SMALL_WIDTH, rows), lambda gi: (gi, 0, 0)),
            pl.BlockSpec((gs, CONV_W - 1, CONV_DIM), lambda gi: (gi, 0, 0)),
            pl.BlockSpec(state_block, lambda gi: (gi, 0, 0, 0)),
            const2((CONV_W, CONV_DIM)),
            const2((1, CONV_DIM)),
            const2((1, N_SSM_HEADS)),
            const2((N_SSM_HEADS, 1)),
            const2((1, N_SSM_HEADS)),
            const2((N_SSM_HEADS, 1)),
            pl.BlockSpec(memory_space=pltpu.SMEM),
            const2((1, SSM_WIDTH)),
        ],
        out_specs=[
            pl.BlockSpec((gs, t, SSM_WIDTH), lambda gi: (gi, 0, 0)),
            pl.BlockSpec((gs, CONV_W - 1, CONV_DIM), lambda gi: (gi, 0, 0)),
            pl.BlockSpec(state_block, lambda gi: (gi, 0, 0, 0)),
        ],
        out_shape=[
            jax.ShapeDtypeStruct((b, t, SSM_WIDTH), BF16),
            jax.ShapeDtypeStruct((b, CONV_W - 1, CONV_DIM), F32),
            jax.ShapeDtypeStruct((b, N_SSM_HEADS, SSM_HEAD_DIM, SSM_STATE), F32),
        ],
        scratch_shapes=[
            pltpu.VMEM((gs, SUBLANES + t, CONV_DIM), F32),
            pltpu.VMEM((rows, CONV_DIM), F32),
            pltpu.VMEM((rows, SSM_WIDTH), F32),
            pltpu.VMEM((rows, SSM_WIDTH), F32),
            pltpu.VMEM((rows, SSM_WIDTH), F32),
        ],
        compiler_params=pltpu.CompilerParams(
            dimension_semantics=("parallel",), vmem_limit_bytes=VMEM_LIMIT),
        name="ssd_sample",
    )(big3, bc3, bc3, big3, small3, small3_t, prefix, h0,
      conv_w, row(conv_b), row(dt_bias), colv(dt_bias), row(a_log), colv(a_log), d_skip, row(ssm_norm))


def _ffn_body(x_ref, att_ref, y_ref, woa_ref, woy_ref, gf_ref, wg_ref, wu_ref, wd_ref, gl_ref,
              o_ref, f_scr):
    j = pl.program_id(1)

    @pl.when(j == 0)
    def _():
        h = x_ref[...] + _dot(att_ref[...], woa_ref[...]) + _dot(y_ref[...], woy_ref[...])
        o_ref[...] = h
        f_scr[...] = _rms(h, gf_ref[...]).astype(BF16)

    f = f_scr[...]
    act = _silu(_dot(f, wg_ref[...])) * _dot(f, wu_ref[...])
    o_ref[...] += _dot(act.astype(BF16), wd_ref[...])

    @pl.when(j == pl.num_programs(1) - 1)
    def _():
        o_ref[...] = _rms(o_ref[...], gl_ref[...])


def _out_ffn(x2, att2, y2, woa, woy, gf, wg, wu, wd, gl, *, tm, th):
    m, d = x2.shape
    hidden = wg.shape[1]
    resident = dict(pipeline_mode=pl.Buffered(1))
    return pl.pallas_call(
        _ffn_body,
        grid=(m // tm, hidden // th),
        in_specs=[
            pl.BlockSpec((tm, d), lambda i, j: (i, 0)),
            pl.BlockSpec((tm, ATT_WIDTH), lambda i, j: (i, 0)),
            pl.BlockSpec((tm, SSM_WIDTH), lambda i, j: (i, 0)),
            pl.BlockSpec((ATT_WIDTH, d), lambda i, j: (0, 0), **resident),
            pl.BlockSpec((SSM_WIDTH, d), lambda i, j: (0, 0), **resident),
            pl.BlockSpec((1, d), lambda i, j: (0, 0)),
            pl.BlockSpec((d, th), lambda i, j: (0, j)),
            pl.BlockSpec((d, th), lambda i, j: (0, j)),
            pl.BlockSpec((th, d), lambda i, j: (j, 0)),
            pl.BlockSpec((1, d), lambda i, j: (0, 0)),
        ],
        out_specs=pl.BlockSpec((tm, d), lambda i, j: (i, 0)),
        out_shape=jax.ShapeDtypeStruct((m, d), F32),
        scratch_shapes=[pltpu.VMEM((tm, d), BF16)],
        compiler_params=pltpu.CompilerParams(
            dimension_semantics=("parallel", "arbitrary"), vmem_limit_bytes=VMEM_LIMIT),
        name="out_ffn",
    )(x2, att2, y2, woa, woy, gf, wg, wu, wd, gl)


def _t5_bucket(rel):
    max_exact = N_BUCKETS // 2
    relf = jnp.maximum(rel, 1).astype(jnp.float32)
    large = max_exact + (jnp.log(relf / max_exact) / math.log(MAX_DISTANCE / max_exact)
                         * (N_BUCKETS - max_exact)).astype(jnp.int32)
    large = jnp.minimum(large, N_BUCKETS - 1)
    return jnp.where(rel < max_exact, rel, large)


def _toeplitz(u, rows, cols):
    nh, period = u.shape
    assert cols < period
    flat = jnp.tile(u, (1, rows))
    return flat[:, :rows * (period - 1)].reshape(nh, rows, period - 1)[:, :, :cols]


def _bias_tiles(rel_bias, rows, key_major):
    assert MAX_DISTANCE <= LANES and rows <= LANES
    table = rel_bias[_t5_bucket(jnp.arange(3 * LANES, dtype=jnp.int32))].T
    tiles = []
    for dd in (2, 1, 0):
        neg = table[:, (dd - 1) * LANES:dd * LANES] if dd >= 1 else jnp.tile(table[:, 0:1], (1, LANES))
        u = jnp.concatenate([table[:, dd * LANES:(dd + 1) * LANES], neg], axis=1)
        if key_major:
            tiles.append(_toeplitz(u, LANES, rows))
        else:
            w = jnp.concatenate([u[:, 0:1], jnp.flip(u[:, 1:], axis=1)], axis=1)
            tiles.append(_toeplitz(w, rows, LANES))
    return jnp.stack(tiles)


def _layer(x, att_call, ssd_call, blk_rows, p, *, kv_bf16):
    b, t, d = x.shape
    m = b * t
    tm = min(ROW_TILE, m)
    proj = _inproj(x.reshape(m, d), p["norm_attn"], p["wb"], p["ws"], p["idx_k_norm"], tm=tm,
                   seq_len=t if kv_bf16 else None)
    big, bc, small, kin = proj[:4]
    big3 = big.reshape(b, t, BIG_WIDTH)
    bc3 = bc.reshape(b, t, BC_WIDTH)
    small3 = small.reshape(b, t, SMALL_WIDTH)
    kin3 = kin.reshape(b, t, IDX_DIM)
    hd = (b, t, N_ATT_HEADS, ATT_HEAD_DIM)
    k4 = big3[:, :, COL_K:COL_K + ATT_WIDTH].reshape(hd)
    v4 = big3[:, :, COL_V:COL_V + ATT_WIDTH].reshape(hd)
    small3_t = jnp.transpose(small.reshape(m // blk_rows, blk_rows, SMALL_WIDTH), (0, 2, 1))
    att = att_call(big3, small3, small3_t, kin3, k4, v4, *proj[4:])
    y, tail, h_fin = ssd_call(big3, bc3, small3, small3_t)
    out = _out_ffn(x.reshape(m, d), att.reshape(m, ATT_WIDTH), y.reshape(m, SSM_WIDTH), p["woa"], p["woy"],
                   p["norm_ffn"], p["wg"], p["wu"], p["wd"], p["norm_final"], tm=tm, th=FFN_TILE)
    return out.reshape(b, t, d), k4, v4, kin3, tail, h_fin


def kernel(x_prompt, x_sample, cache_k, cache_v, cache_kidx, state_conv, state_ssm, page_table, rel_bias,
           norm_attn, w_in, idx_k_norm, conv_w, conv_b, dt_bias, a_log, d_skip, ssm_norm, w_out, norm_ffn,
           w_gate, w_up, w_down, norm_final):
    depth = w_in.shape[0]
    assert depth == 1
    bp, s, d = x_prompt.shape
    bs, t, _ = x_sample.shape
    n_pages = page_table.shape[1]
    past = n_pages * PAGE_SIZE
    lyr = 0

    w = jnp.swapaxes(w_in[lyr], 0, 1)
    o_ki = 3 * ATT_WIDTH + N_IDX_HEADS * IDX_DIM
    o_z = o_ki + IDX_DIM + N_IDX_HEADS
    o_dt = o_z + SSM_WIDTH + CONV_DIM
    o_x = o_z + SSM_WIDTH
    o_bc = o_x + SSM_WIDTH
    wb = jnp.concatenate([w[:o_ki], w[o_z:o_bc]], axis=0).astype(BF16)
    ws = jnp.concatenate([w[o_bc:o_dt], w[o_ki:o_z], w[o_dt:],
                          jnp.zeros((SMALL_WIDTH - (o_z - o_ki) - N_SSM_HEADS, d), F32)], axis=0).astype(BF16)
    row = lambda v: v.reshape(1, -1)
    p = dict(
        norm_attn=row(norm_attn[lyr]), wb=wb, ws=ws, idx_k_norm=row(idx_k_norm[lyr]),
        conv_w=conv_w[lyr], conv_b=conv_b[lyr], dt_bias=dt_bias[lyr], a_log=a_log[lyr], d_skip=d_skip[lyr],
        ssm_norm=ssm_norm[lyr],
        woa=w_out[lyr, :ATT_WIDTH].astype(BF16), woy=w_out[lyr, ATT_WIDTH:].astype(BF16),
        norm_ffn=row(norm_ffn[lyr]), wg=w_gate[lyr].astype(BF16), wu=w_up[lyr].astype(BF16),
        wd=w_down[lyr].astype(BF16), norm_final=row(norm_final),
    )

    topk_p = min(TOPK_MAX, s // 4)
    assert SSD_CHUNK == LANES and s % LANES == 0
    tiles_p = _bias_tiles(rel_bias, LANES, key_major=True)
    tiles_p = tiles_p - tiles_p[0:1]

    def att_prompt(big3, small3, small3_t, kin3, k4, v4, k16, vt16):
        return _attn_prompt(big3, small3_t, kin3, k16.reshape(bp, s, ATT_WIDTH), vt16, tiles_p,
                            topk=topk_p, group=CAUSAL_GROUP)

    zero_conv = jnp.zeros((bp, CONV_W - 1, CONV_DIM), F32)
    zero_ssm = jnp.zeros((bp, N_SSM_HEADS, SSM_HEAD_DIM, SSM_STATE), F32)
    ssm_params = (p["conv_w"], p["conv_b"], p["dt_bias"], p["a_log"], p["d_skip"], p["ssm_norm"])

    def ssd_prompt(big3, bc3, small3, small3_t):
        return _ssd(big3, bc3, small3, small3_t, zero_conv, zero_ssm, *ssm_params, l=SSD_CHUNK)

    yp, kp, vp, kip, cp, sp = _layer(x_prompt, att_prompt, ssd_prompt, SSD_CHUNK, p, kv_bf16=True)

    topk_s = min(TOPK_MAX, (past + t) // 4)
    tiles_s = jnp.repeat(_bias_tiles(rel_bias, t, key_major=False).reshape(3, N_ATT_HEADS * t, LANES),
                         N_ATT_HEADS, axis=-1)

    cache_kidx_t = jnp.swapaxes(cache_kidx, -1, -2)

    gs = math.gcd(bs, LANES // t)

    def att_sample(big3, small3, small3_t, kin3, k4, v4):
        sel = _select_sample(page_table, big3, small3, kin3, cache_kidx_t, topk=topk_s, gs=gs)
        return _attn_sample(page_table, big3, k4, v4, sel, tiles_s, cache_k, cache_v, gs=gs)

    def ssd_sample(big3, bc3, small3, small3_t):
        return _ssd_sample(big3, bc3, small3, small3_t, state_conv[lyr], state_ssm[lyr], *ssm_params, gs=gs)

    ys, ks, vs, kis, cs, ss = _layer(x_sample, att_sample, ssd_sample, gs * t, p, kv_bf16=False)

    st = lambda a: a[None]
    return (yp, ys, st(kp), st(vp), st(kip), st(cp), st(sp), st(ks), st(vs), st(kis), st(cs), st(ss))
```

```python
import functools
import math

import numpy as np
import jax
import jax.numpy as jnp
from jax import lax
from jax.experimental import pallas as pl
from jax.experimental.pallas import tpu as pltpu

N_ATT_HEADS = 8
ATT_HEAD_DIM = 128
ATT_WIDTH = N_ATT_HEADS * ATT_HEAD_DIM
N_IDX_HEADS = 16
IDX_DIM = 64
TOPK_MAX = 256
N_SSM_HEADS = 16
SSM_HEAD_DIM = 64
SSM_WIDTH = N_SSM_HEADS * SSM_HEAD_DIM
N_SSM_GROUPS = 2
SSM_STATE = 128
CONV_W = 4
CONV_DIM = SSM_WIDTH + 2 * N_SSM_GROUPS * SSM_STATE
SSD_CHUNK = 128
PAGE_SIZE = 128
N_BUCKETS = 32
MAX_DISTANCE = 128
NORM_EPS = 1e-6

LANES = 128
SUBLANES = 8
VMEM_LIMIT = 56 * 1024 * 1024

ROW_TILE = 512
FFN_TILE = 512
CAUSAL_GROUP = 2

COL_Q, COL_K, COL_V, COL_QI, COL_Z, COL_X = (i * ATT_WIDTH for i in range(6))
BIG_WIDTH = 6 * ATT_WIDTH
GROUP_WIDTH = N_SSM_GROUPS * SSM_STATE
BC_WIDTH = 2 * GROUP_WIDTH
SM_KI, SM_WI, SM_DT = 0, IDX_DIM, IDX_DIM + N_IDX_HEADS
SMALL_WIDTH = LANES
INPROJ_TN = 2 * ATT_WIDTH
IN_KI = 3 * ATT_WIDTH + N_IDX_HEADS * IDX_DIM
IN_Z = IN_KI + IDX_DIM + N_IDX_HEADS
IN_B = IN_Z + 2 * SSM_WIDTH
IN_WIDTH = IN_B + BC_WIDTH + N_SSM_HEADS

F32 = jnp.float32
BF16 = jnp.bfloat16

_INT_MIN = -(2 ** 31)
_INT_MAX = 2 ** 31 - 1
_KEY_NEG_INF = int(np.array([-np.inf], np.float32).view(np.int32)[0]) ^ 0x7FFFFFFF
_NEG_BIG = -1e30


def _dot(a, b):
    return jnp.dot(a, b, preferred_element_type=F32)


def _dot_nt(a, b):
    return lax.dot_general(a, b, (((1,), (1,)), ((), ())), preferred_element_type=F32)


def _dot_tn(a, b):
    return lax.dot_general(a, b, (((0,), (0,)), ((), ())), preferred_element_type=F32)


def _rms(x, g):
    return x * lax.rsqrt(jnp.mean(x * x, axis=-1, keepdims=True) + NORM_EPS) * g


def _silu(x):
    return x * (1.0 / (1.0 + jnp.exp(-x)))


def _inproj_body(x_ref, g_ref, wb_ref, wsa_ref, wsb_ref, gk_ref, big_ref, bc_ref, small_ref, kin_ref, *rest,
                 kv_bf16):
    u_scr = rest[-1]
    n = pl.program_id(1)

    @pl.when(n == 0)
    def _():
        ub = _rms(x_ref[...], g_ref[...]).astype(BF16)
        u_scr[...] = ub
        nar_a = _dot_nt(ub, wsa_ref[...])
        nar_b = _dot_nt(ub, wsb_ref[...])
        bc_ref[...] = nar_b[:, :BC_WIDTH]
        pad = jnp.zeros((ub.shape[0], SMALL_WIDTH - SM_DT - N_SSM_HEADS), F32)
        small_ref[...] = jnp.concatenate([nar_a, nar_b[:, BC_WIDTH:], pad], axis=1)
        kin_ref[...] = _rms(nar_a[:, :IDX_DIM], gk_ref[...])

    tile = _dot_nt(u_scr[...], wb_ref[...])
    big_ref[...] = tile
    if kv_bf16:
        k16_ref, vt16_ref = rest[:2]

        @pl.when(n == COL_K // INPROJ_TN)
        def _():
            k16_ref[...] = tile[:, COL_K % INPROJ_TN:COL_K % INPROJ_TN + ATT_WIDTH].astype(BF16)

        @pl.when(n == COL_V // INPROJ_TN)
        def _():
            vt16_ref[0] = tile[:, COL_V % INPROJ_TN:COL_V % INPROJ_TN + ATT_WIDTH].T.astype(BF16)


def _inproj(x2, g, w_t, gk, *, tm, seq_len=None):
    m, d = x2.shape
    kv_bf16 = seq_len is not None
    assert IN_Z + INPROJ_TN == IN_B and IN_KI == 2 * INPROJ_TN
    assert (IN_Z - IN_KI) % (2 * SUBLANES) == 0 and IN_B % (2 * SUBLANES) == 0
    big_rows = lambda i, j: (pl.multiple_of(j * INPROJ_TN + (j // 2) * (IN_Z - IN_KI), 2 * SUBLANES), 0)
    out_specs = [
        pl.BlockSpec((tm, INPROJ_TN), lambda i, j: (i, j)),
        pl.BlockSpec((tm, BC_WIDTH), lambda i, j: (i, 0)),
        pl.BlockSpec((tm, SMALL_WIDTH), lambda i, j: (i, 0)),
        pl.BlockSpec((tm, IDX_DIM), lambda i, j: (i, 0)),
    ]
    out_shape = [
        jax.ShapeDtypeStruct((m, BIG_WIDTH), F32),
        jax.ShapeDtypeStruct((m, BC_WIDTH), F32),
        jax.ShapeDtypeStruct((m, SMALL_WIDTH), F32),
        jax.ShapeDtypeStruct((m, IDX_DIM), F32),
    ]
    if kv_bf16:
        assert seq_len % tm == 0
        per_seq = seq_len // tm
        out_specs += [
            pl.BlockSpec((tm, ATT_WIDTH), lambda i, j: (i, 0)),
            pl.BlockSpec((1, ATT_WIDTH, tm), lambda i, j: (i // per_seq, 0, i % per_seq)),
        ]
        out_shape += [
            jax.ShapeDtypeStruct((m, ATT_WIDTH), BF16),
            jax.ShapeDtypeStruct((m // seq_len, ATT_WIDTH, seq_len), BF16),
        ]
    return pl.pallas_call(
        functools.partial(_inproj_body, kv_bf16=kv_bf16),
        grid=(m // tm, BIG_WIDTH // INPROJ_TN),
        in_specs=[
            pl.BlockSpec((tm, d), lambda i, j: (i, 0)),
            pl.BlockSpec((1, d), lambda i, j: (0, 0)),
            pl.BlockSpec((pl.Element(INPROJ_TN), pl.Element(d)), big_rows),
            pl.BlockSpec((pl.Element(IN_Z - IN_KI), pl.Element(d)), lambda i, j: (IN_KI, 0),
                         pipeline_mode=pl.Buffered(1)),
            pl.BlockSpec((pl.Element(IN_WIDTH - IN_B), pl.Element(d)), lambda i, j: (IN_B, 0),
                         pipeline_mode=pl.Buffered(1)),
            pl.BlockSpec((1, IDX_DIM), lambda i, j: (0, 0)),
        ],
        out_specs=out_specs,
        out_shape=out_shape,
        scratch_shapes=[pltpu.VMEM((tm, d), BF16)],
        compiler_params=pltpu.CompilerParams(
            dimension_semantics=("parallel", "arbitrary"), vmem_limit_bytes=VMEM_LIMIT),
        name="inproj",
    )(x2, g, w_t, w_t, w_t, gk)


def _sortable_key(s):
    s = jnp.where(s == 0.0, 0.0, s)
    b = pltpu.bitcast(s, jnp.int32)
    return b ^ ((b >> 31) & 0x7FFFFFFF)


def _count_keys(key_ref, nblk, pred, key_axis):
    def body(kb, acc):
        return acc + jnp.where(pred(key_ref[kb], kb), 1.0, 0.0)

    acc = jnp.zeros(key_ref.shape[1:], F32)
    if isinstance(nblk, int):
        for kb in range(nblk):
            acc = body(kb, acc)
    else:
        acc = lax.fori_loop(0, nblk, body, acc)
    return jnp.sum(acc, axis=key_axis, keepdims=True)


_I16_MIN = -(2 ** 15)


def _pack_halves(key):
    r = key.shape[0] // 2
    a, b = key[:r], key[r:]
    hi = (a & jnp.int32(-65536)) | ((b >> 16) & 0xFFFF)
    lo = ((a ^ 0x8000) << 16) | ((b ^ 0x8000) & 0xFFFF)
    return hi, lo


def _count16(ref, nblk, cmp, cand):
    r = ref.shape[1]
    pair = (cand << 16) | (cand & 0xFFFF)
    cb = pltpu.bitcast(jnp.broadcast_to(pair, (r, LANES)), jnp.int16)
    acc = jnp.zeros((2 * r, LANES), jnp.int16)
    for kb in range(nblk):
        acc = acc + jnp.where(cmp(pltpu.bitcast(ref[kb], jnp.int16), cb), jnp.int16(1), jnp.int16(0))
    words = pltpu.bitcast(acc, jnp.int32)
    both = (words >> 16) + ((words << 16) >> 16)
    return jnp.sum(both.astype(F32), axis=0, keepdims=True)


def _kth_of_16bit(ref, nblk, need):
    ge = lambda k, c: k >= c
    res = jnp.where(_count16(ref, nblk, ge, jnp.zeros((1, LANES), jnp.int32)) >= need, 0, _I16_MIN).astype(jnp.int32)

    def body(it, res):
        cand = res | jnp.left_shift(jnp.int32(1), 14 - it)
        return jnp.where(_count16(ref, nblk, ge, cand) >= need, cand, res)

    return lax.fori_loop(0, 15, body, res)


def _kth_key_two_phase(hi_ref, lo_ref, nblk, topk):
    kf = float(topk)
    top = _kth_of_16bit(hi_ref, nblk, kf)
    n_above = _count16(hi_ref, nblk, lambda k, c: k > c, top)
    r = hi_ref.shape[1]
    top_b = pltpu.bitcast(jnp.broadcast_to((top << 16) | (top & 0xFFFF), (r, LANES)), jnp.int16)
    for kb in range(nblk):
        band = pltpu.bitcast(hi_ref[kb], jnp.int16) == top_b
        lo_ref[kb] = pltpu.bitcast(jnp.where(band, pltpu.bitcast(lo_ref[kb], jnp.int16), jnp.int16(_I16_MIN)),
                                   jnp.int32)
    low = _kth_of_16bit(lo_ref, nblk, kf - n_above)
    return (top << 16) | ((low ^ 0x8000) & 0xFFFF)


def _topk_select_params(key_ref, nblk, topk, key_axis, thr=None):
    shape = key_ref.shape[1:]
    qshape = tuple(1 if a == key_axis else n for a, n in enumerate(shape))
    blk = shape[key_axis]
    kf = float(topk)

    def count_ge(cand):
        cb = jnp.broadcast_to(cand, shape)
        return _count_keys(key_ref, nblk, lambda k, kb: k >= cb, key_axis)

    if thr is None:
        res = jnp.where(count_ge(jnp.zeros(qshape, jnp.int32)) >= kf, 0, _INT_MIN).astype(jnp.int32)

        def bit_body(it, res):
            cand = res | jnp.left_shift(jnp.int32(1), 30 - it)
            return jnp.where(count_ge(cand) >= kf, cand, res)

        thr = lax.fori_loop(0, 31, bit_body, res)
    thr_b = jnp.broadcast_to(thr, shape)
    n_gt = _count_keys(key_ref, nblk, lambda k, kb: k > thr_b, key_axis)
    n_ge = count_ge(thr)
    tied = jnp.logical_and(n_ge > kf, thr > _KEY_NEG_INF)
    any_tied = jnp.max(jnp.where(tied, 1.0, 0.0)) > 0.0
    need = kf - n_gt
    kpos = lax.broadcasted_iota(jnp.int32, shape, key_axis)

    def tie_break():
        def body(it, cut):
            cand = cut | jnp.left_shift(jnp.int32(1), 30 - it)
            cand_b = jnp.broadcast_to(cand, shape)
            n = _count_keys(key_ref, nblk,
                            lambda k, kb: jnp.logical_and(k == thr_b, kpos + kb * blk < cand_b), key_axis)
            return jnp.where(n <= need, cand, cut)

        return lax.fori_loop(0, 31, body, jnp.zeros(qshape, jnp.int32))

    cut = lax.cond(any_tied, tie_break, lambda: jnp.full(qshape, _INT_MAX, jnp.int32))
    return thr, cut


def _attn_prompt_body(q_ref, qi_ref, smt_ref, kin_ref, k_ref, vt_ref, bias_ref, o_ref,
                      key_scr, hi_scr, lo_scr, mask_scr, lg_scr, qi_scr, *, topk, group):
    i = pl.program_id(1)
    q0 = i * LANES
    blk = (LANES, LANES)
    scale = ATT_HEAD_DIM ** -0.5
    nq = key_scr.shape[0]

    w_t = smt_ref[0, SM_WI:SM_WI + N_IDX_HEADS, :]
    qi_all = qi_ref[0]
    for h in range(N_IDX_HEADS):
        qi_scr[h * LANES:(h + 1) * LANES, :] = qi_all[:, h * IDX_DIM:(h + 1) * IDX_DIM].astype(BF16)
    qi = qi_scr[...]
    kpos = lax.broadcasted_iota(jnp.int32, blk, 0)
    qpos = lax.broadcasted_iota(jnp.int32, blk, 1) + q0
    qb = q_ref[0].astype(BF16)

    def tile(nb, first_near):
        for kb in range(nb):
            kin = kin_ref[0, kb * LANES:(kb + 1) * LANES, :].astype(BF16)
            s = _dot_nt(kin, qi)
            score = jnp.zeros(blk, F32)
            for h in range(N_IDX_HEADS):
                score = score + jnp.maximum(s[:, h * LANES:(h + 1) * LANES], 0.0) * w_t[h:h + 1, :]
            if kb >= first_near:
                score = jnp.where(kpos + kb * LANES <= qpos, score, -jnp.inf)
            key = _sortable_key(score)
            key_scr[kb] = key
            hi_scr[kb], lo_scr[kb] = _pack_halves(key)

        thr = _kth_key_two_phase(hi_scr, lo_scr, nb, topk)
        thr, cut = _topk_select_params(key_scr, nb, topk, key_axis=0, thr=thr)
        thr_b = jnp.broadcast_to(thr, blk)
        cut_b = jnp.broadcast_to(cut, blk)
        for kb in range(nb):
            kp = kpos + kb * LANES
            key = key_scr[kb]
            valid = jnp.logical_or(key > thr_b, jnp.logical_and(key == thr_b, kp < cut_b))
            if kb >= first_near:
                valid = jnp.logical_and(valid, kp <= qpos)
            mask_scr[kb * LANES:(kb + 1) * LANES, :] = jnp.where(valid, 0.0, -jnp.inf)

        nk = nb * LANES
        for h in range(N_ATT_HEADS):
            hs = slice(h * ATT_HEAD_DIM, (h + 1) * ATT_HEAD_DIM)
            lg = _dot_nt(k_ref[0, 0:nk, hs], qb[:, hs]) * scale + mask_scr[0:nk, :]
            for kb in range(first_near, nb):
                tix = 2 - jnp.clip(i - kb, 0, 2)
                lg_scr[kb * LANES:(kb + 1) * LANES, :] = lg[kb * LANES:(kb + 1) * LANES, :] + bias_ref[tix, h]
            if first_near > 0:
                lg_scr[0:first_near * LANES, :] = lg[0:first_near * LANES, :]
            lg = lg_scr[0:nk, :]
            m = jnp.max(lg, axis=0, keepdims=True)
            p = jnp.exp(lg - m)
            l = jnp.sum(p, axis=0, keepdims=True)
            out_t = _dot(vt_ref[0, hs, 0:nk], p.astype(BF16)) / l
            o_ref[0, :, hs] = out_t.T.astype(o_ref.dtype)

    ngroups = -(-nq // group)
    for c in range(ngroups):
        nb = min((c + 1) * group, nq)

        @pl.when(i // group == c)
        def _(nb=nb, c=c):
            tile(nb, max(c * group - 1, 0))


def _attn_prompt(big3, small3_t, kin3, k16, vt16, bias_tiles_t, *, topk, group):
    b, s, _ = big3.shape
    nq = s // LANES
    body = functools.partial(_attn_prompt_body, topk=topk, group=group)
    return pl.pallas_call(
        body,
        grid=(b, nq),
        in_specs=[
            pl.BlockSpec((1, LANES, ATT_WIDTH), lambda bi, i: (bi, i, COL_Q // ATT_WIDTH)),
            pl.BlockSpec((1, LANES, N_IDX_HEADS * IDX_DIM), lambda bi, i: (bi, i, COL_QI // (N_IDX_HEADS * IDX_DIM))),
            pl.BlockSpec((1, SMALL_WIDTH, LANES), lambda bi, i: (bi * nq + i, 0, 0)),
            pl.BlockSpec((1, s, IDX_DIM), lambda bi, i: (bi, 0, 0)),
            pl.BlockSpec((1, s, ATT_WIDTH), lambda bi, i: (bi, 0, 0)),
            pl.BlockSpec((1, ATT_WIDTH, s), lambda bi, i: (bi, 0, 0)),
            pl.BlockSpec(bias_tiles_t.shape, lambda bi, i: (0, 0, 0, 0)),
        ],
        out_specs=pl.BlockSpec((1, LANES, ATT_WIDTH), lambda bi, i: (bi, i, 0)),
        out_shape=jax.ShapeDtypeStruct((b, s, ATT_WIDTH), BF16),
        scratch_shapes=[
            pltpu.VMEM((nq, LANES, LANES), jnp.int32),
            pltpu.VMEM((nq, LANES // 2, LANES), jnp.int32),
            pltpu.VMEM((nq, LANES // 2, LANES), jnp.int32),
            pltpu.VMEM((s, LANES), F32),
            pltpu.VMEM((s, LANES), F32),
            pltpu.VMEM((N_IDX_HEADS * LANES, IDX_DIM), BF16),
        ],
        compiler_params=pltpu.CompilerParams(
            dimension_semantics=("parallel", "arbitrary"), vmem_limit_bytes=VMEM_LIMIT),
        name="attn_prompt",
    )(big3, big3, small3_t, kin3, k16, vt16, bias_tiles_t)


def _select_sample_body(pt_ref, qi_ref, sm_ref, kinew_ref, *rest, n_pages, t, topk):
    kidx_refs = rest[:n_pages]
    sel_ref = rest[n_pages]
    key_scr, wb_scr = rest[n_pages + 1:]
    nblk = n_pages + 1
    j = pl.program_id(1)
    r0 = pl.multiple_of(j * t, t)

    wi = sm_ref[0, :, SM_WI:SM_WI + N_IDX_HEADS]
    for h in range(N_IDX_HEADS):
        wb_scr[h] = jnp.broadcast_to(wi[:, h:h + 1], (t, LANES))
    qi_all = qi_ref[0]
    qi = jnp.concatenate([qi_all[:, h * IDX_DIM:(h + 1) * IDX_DIM] for h in range(N_IDX_HEADS)],
                         axis=0).astype(BF16)

    def scores(s):
        return jnp.sum(jnp.maximum(s.reshape(N_IDX_HEADS, t, LANES), 0.0) * wb_scr[...], axis=0)

    for p in range(n_pages):
        key_scr[p, pl.ds(r0, t), :] = _sortable_key(scores(_dot(qi, kidx_refs[p][0, 0].astype(BF16))))
    kin_new = jnp.concatenate([kinew_ref[0], jnp.zeros((LANES - t, IDX_DIM), F32)], axis=0)
    trow = lax.broadcasted_iota(jnp.int32, (t, LANES), 0)
    lane_t = lax.broadcasted_iota(jnp.int32, (t, LANES), 1)
    s_new = jnp.where(lane_t <= trow, scores(_dot_nt(qi, kin_new.astype(BF16))), -jnp.inf)
    key_scr[n_pages, pl.ds(r0, t), :] = _sortable_key(s_new)

    @pl.when(j == pl.num_programs(1) - 1)
    def _():
        rows = key_scr.shape[1]
        thr, cut = _topk_select_params(key_scr, nblk, topk, key_axis=1)
        thr_b = jnp.broadcast_to(thr, (rows, LANES))
        cut_b = jnp.broadcast_to(cut, (rows, LANES))
        lane = lax.broadcasted_iota(jnp.int32, (rows, LANES), 1)
        tq = lax.broadcasted_iota(jnp.int32, (rows, LANES), 0) & (t - 1)
        for p in range(nblk):
            key = key_scr[p]
            sel = jnp.logical_or(key > thr_b, jnp.logical_and(key == thr_b, lane + p * LANES < cut_b))
            if p == n_pages:
                sel = jnp.logical_and(sel, lane <= tq)
            sel_ref[0, p] = jnp.where(sel, 1.0, 0.0)


def _select_sample(page_table, big3, small3, kin3, cache_kidx_t, *, topk, gs):
    b, t, _ = small3.shape
    n_pages = page_table.shape[1]
    assert b % gs == 0 and t & (t - 1) == 0
    body = functools.partial(_select_sample_body, n_pages=n_pages, t=t, topk=topk)
    seq = lambda g, j: g * gs + j
    qi_width = N_IDX_HEADS * IDX_DIM
    in_specs = [
        pl.BlockSpec((1, t, qi_width), lambda g, j, pt: (seq(g, j), 0, COL_QI // qi_width)),
        pl.BlockSpec((1, t, SMALL_WIDTH), lambda g, j, pt: (seq(g, j), 0, 0)),
        pl.BlockSpec((1, t, IDX_DIM), lambda g, j, pt: (seq(g, j), 0, 0)),
    ]
    in_specs += [pl.BlockSpec((1, 1, IDX_DIM, PAGE_SIZE), lambda g, j, pt, p=p: (0, pt[seq(g, j), p], 0, 0))
                 for p in range(n_pages)]
    sel_shape = (b // gs, n_pages + 1, gs * t, LANES)
    return pl.pallas_call(
        body,
        grid_spec=pltpu.PrefetchScalarGridSpec(
            num_scalar_prefetch=1,
            grid=(b // gs, gs),
            in_specs=in_specs,
            out_specs=pl.BlockSpec((1,) + sel_shape[1:], lambda g, j, pt: (g, 0, 0, 0)),
            scratch_shapes=[
                pltpu.VMEM(sel_shape[1:], jnp.int32),
                pltpu.VMEM((N_IDX_HEADS, t, LANES), F32),
            ],
        ),
        out_shape=jax.ShapeDtypeStruct(sel_shape, F32),
        compiler_params=pltpu.CompilerParams(
            dimension_semantics=("parallel", "arbitrary"), vmem_limit_bytes=VMEM_LIMIT),
        name="select_sample",
    )(page_table, big3, small3, kin3, *([cache_kidx_t] * n_pages))


def _attn_sample_body(pt_ref, q_ref, knew_ref, vnew_ref, sel_ref, bias_ref, *rest, n_pages, t):
    k_refs = rest[:n_pages]
    v_refs = rest[n_pages:2 * n_pages]
    o_ref = rest[2 * n_pages]
    (lg_scr,) = rest[2 * n_pages + 1:]
    nblk = n_pages + 1
    rows = N_ATT_HEADS * t
    flat = PAGE_SIZE * N_ATT_HEADS
    scale = ATT_HEAD_DIM ** -0.5

    q = q_ref[0]
    q64 = jnp.concatenate([q[:, h * ATT_HEAD_DIM:(h + 1) * ATT_HEAD_DIM] for h in range(N_ATT_HEADS)],
                          axis=0).astype(BF16)
    rr = lax.broadcasted_iota(jnp.int32, (rows, flat), 0)
    cc = lax.broadcasted_iota(jnp.int32, (rows, flat), 1)
    head_match = (cc & (N_ATT_HEADS - 1)) == (rr >> (t.bit_length() - 1))
    er = lax.broadcasted_iota(jnp.int32, (LANES, flat), 0)
    ec = lax.broadcasted_iota(jnp.int32, (LANES, flat), 1)
    expand = jnp.where((ec >> (N_ATT_HEADS.bit_length() - 1)) == er, 1.0, 0.0).astype(BF16)

    def flat_kv(refs, p):
        if p < n_pages:
            return refs[p][0, 0].reshape(flat, ATT_HEAD_DIM).astype(BF16), flat
        new = refs[p][0].reshape(t * N_ATT_HEADS, ATT_HEAD_DIM)
        pad = jnp.zeros((LANES - t * N_ATT_HEADS, ATT_HEAD_DIM), F32)
        return jnp.concatenate([new, pad], axis=0).astype(BF16), LANES

    k_all = list(k_refs) + [knew_ref]
    v_all = list(v_refs) + [vnew_ref]
    m = jnp.full((rows, 1), _NEG_BIG, F32)
    for p in range(nblk):
        kflat, width = flat_kv(k_all, p)
        selx = _dot(sel_ref[0, p].astype(BF16), expand[:, :width])
        valid = jnp.logical_and(head_match[:, :width], jnp.concatenate([selx] * N_ATT_HEADS, axis=0) > 0.5)
        tix = 2 - min(n_pages - p, 2)
        lg = _dot_nt(q64, kflat) * scale + bias_ref[tix, :, :width]
        lg = jnp.where(valid, lg, _NEG_BIG)
        lg_scr[p, :, :width] = lg
        m = jnp.maximum(m, jnp.max(lg, axis=-1, keepdims=True))

    l = jnp.zeros((rows, 1), F32)
    acc = jnp.zeros((rows, ATT_HEAD_DIM), F32)
    for p in range(nblk):
        vflat, width = flat_kv(v_all, p)
        pr = jnp.exp(lg_scr[p, :, :width] - m)
        l = l + jnp.sum(pr, axis=-1, keepdims=True)
        acc = acc + _dot(pr.astype(BF16), vflat)
    out = acc / l
    for h in range(N_ATT_HEADS):
        o_ref[0, :, h * ATT_HEAD_DIM:(h + 1) * ATT_HEAD_DIM] = out[h * t:(h + 1) * t, :].astype(o_ref.dtype)


def _attn_sample(page_table, big3, k_new, v_new, sel, bias_rows, cache_k, cache_v, *, gs):
    b, t, _ = big3.shape
    n_pages = page_table.shape[1]
    assert t == SUBLANES and N_ATT_HEADS & (N_ATT_HEADS - 1) == 0 and t * N_ATT_HEADS <= LANES
    body = functools.partial(_attn_sample_body, n_pages=n_pages, t=t)
    kv_block = (1, 1, PAGE_SIZE, N_ATT_HEADS, ATT_HEAD_DIM)
    new_block = (1, t, N_ATT_HEADS, ATT_HEAD_DIM)

    in_specs = [
        pl.BlockSpec((1, t, ATT_WIDTH), lambda bi, pt: (bi, 0, COL_Q // ATT_WIDTH)),
        pl.BlockSpec(new_block, lambda bi, pt: (bi, 0, 0, 0)),
        pl.BlockSpec(new_block, lambda bi, pt: (bi, 0, 0, 0)),
        pl.BlockSpec((1, n_pages + 1, t, LANES), lambda bi, pt: (bi // gs, 0, bi % gs, 0)),
        pl.BlockSpec(bias_rows.shape, lambda bi, pt: (0, 0, 0)),
    ]
    in_specs += [pl.BlockSpec(kv_block, lambda bi, pt, p=p: (0, pt[bi, p], 0, 0, 0)) for p in range(n_pages)] * 2
    rows = N_ATT_HEADS * t
    return pl.pallas_call(
        body,
        grid_spec=pltpu.PrefetchScalarGridSpec(
            num_scalar_prefetch=1,
            grid=(b,),
            in_specs=in_specs,
            out_specs=pl.BlockSpec((1, t, ATT_WIDTH), lambda bi, pt: (bi, 0, 0)),
            scratch_shapes=[pltpu.VMEM((n_pages + 1, rows, PAGE_SIZE * N_ATT_HEADS), F32)],
        ),
        out_shape=jax.ShapeDtypeStruct((b, t, ATT_WIDTH), BF16),
        compiler_params=pltpu.CompilerParams(
            dimension_semantics=("arbitrary",), vmem_limit_bytes=VMEM_LIMIT),
        name="attn_sample",
    )(page_table, big3, k_new, v_new, sel, bias_rows, *([cache_k] * n_pages), *([cache_v] * n_pages))


def _split3(a):
    a1 = a.astype(BF16)
    r1 = a - a1.astype(F32)
    a2 = r1.astype(BF16)
    a3 = (r1 - a2.astype(F32)).astype(BF16)
    return a1, a2, a3


def _softplus(v):
    return jnp.maximum(v, 0.0) + jnp.log1p(jnp.exp(-jnp.abs(v)))


def _head_expander():
    r = lax.broadcasted_iota(jnp.int32, (SMALL_WIDTH, SSM_WIDTH), 0)
    c = lax.broadcasted_iota(jnp.int32, (SMALL_WIDTH, SSM_WIDTH), 1)
    return jnp.where(r - SM_DT == c // SSM_HEAD_DIM, 1.0, 0.0).astype(BF16)


def _ssd_body(x_ref, bm_ref, cm_ref, z_ref, sm_ref, smt_ref, pre_ref, h0_ref,
              cw_ref, cb_ref, dtb_ref, dtbt_ref, alog_ref, alogt_ref, dskipx_ref, gn_ref,
              y_ref, tail_ref, hout_ref, xp_scr, ht_scr, y_scr, *, l):
    c = pl.program_id(1)
    nc = pl.num_programs(1)
    off = SUBLANES
    heads_per_group = N_SSM_HEADS // N_SSM_GROUPS
    gcols = heads_per_group * SSM_HEAD_DIM

    @pl.when(c == 0)
    def _():
        xp_scr[off - (CONV_W - 1):off, :] = pre_ref[0]
        for h in range(N_SSM_HEADS):
            g, hh = divmod(h, heads_per_group)
            ht_scr[g, :, hh * SSM_HEAD_DIM:(hh + 1) * SSM_HEAD_DIM] = h0_ref[0, h].T

    xp_scr[off:off + l, 0:SSM_WIDTH] = x_ref[0]
    xp_scr[off:off + l, SSM_WIDTH:SSM_WIDTH + GROUP_WIDTH] = bm_ref[0]
    xp_scr[off:off + l, SSM_WIDTH + GROUP_WIDTH:CONV_DIM] = cm_ref[0]

    conv = cb_ref[...]
    for j in range(CONV_W):
        s0 = off - (CONV_W - 1) + j
        conv = conv + xp_scr[s0:s0 + l, :] * cw_ref[j:j + 1, :]
    xc = _silu(conv)
    tail = xp_scr[off + l - (CONV_W - 1):off + l, :]
    tail_ref[0] = tail
    xp_scr[off - (CONV_W - 1):off, :] = tail

    xs = xc[:, :SSM_WIDTH]
    bc = [xc[:, SSM_WIDTH + g * SSM_STATE:SSM_WIDTH + (g + 1) * SSM_STATE].astype(BF16)
          for g in range(N_SSM_GROUPS)]
    cc = [xc[:, SSM_WIDTH + GROUP_WIDTH + g * SSM_STATE:SSM_WIDTH + GROUP_WIDTH + (g + 1) * SSM_STATE]
          .astype(BF16) for g in range(N_SSM_GROUPS)]

    dt = _softplus(sm_ref[0] + dtb_ref[...])
    a = dt * (-jnp.exp(alog_ref[...]))
    dtt = _softplus(smt_ref[0, SM_DT:SM_DT + N_SSM_HEADS, :] + dtbt_ref[...])
    at = dtt * (-jnp.exp(alogt_ref[...]))

    ri = lax.broadcasted_iota(jnp.int32, (l, l), 0)
    ci = lax.broadcasted_iota(jnp.int32, (l, l), 1)
    causal = ri >= ci
    tri = jnp.where(causal, 1.0, 0.0).astype(BF16)
    tri_u = jnp.where(ri <= ci, 1.0, 0.0).astype(BF16)
    acum = sum(_dot(tri, piece) for piece in _split3(a))
    acum_t = sum(_dot(piece, tri_u) for piece in _split3(at))
    last = acum[l - 1:l, :]
    wts = jnp.exp(last - acum) * dt
    expand = _head_expander()
    acum_x = sum(_dot(piece, expand) for piece in _split3(acum))
    wts_x = sum(_dot(piece, expand) for piece in _split3(wts))
    exp_last_x = jnp.exp(acum_x[l - 1:l, :])

    cb = [_dot_nt(cc[g], bc[g]) for g in range(N_SSM_GROUPS)]
    for h in range(N_SSM_HEADS):
        g = h // heads_per_group
        hs = slice(h * SSM_HEAD_DIM, (h + 1) * SSM_HEAD_DIM)
        col = acum[:, SM_DT + h:SM_DT + h + 1]
        decay = jnp.exp(jnp.where(causal, col - acum_t[h:h + 1, :], -jnp.inf))
        mmat = cb[g] * decay * dtt[h:h + 1, :]
        y_scr[:, hs] = _dot(mmat.astype(BF16), xs[:, hs].astype(BF16))

    xw = (xs * wts_x).astype(BF16)
    y_inter = []
    for g in range(N_SSM_GROUPS):
        gsl = slice(g * gcols, (g + 1) * gcols)
        ht = ht_scr[g]
        y_inter.append(_dot(cc[g], ht.astype(BF16)))
        ht_scr[g] = ht * exp_last_x[:, gsl] + _dot_tn(bc[g], xw[:, gsl])
    y = y_scr[...] + jnp.concatenate(y_inter, axis=1) * jnp.exp(acum_x) + dskipx_ref[...] * xs
    y_ref[0] = _rms(y * _silu(z_ref[0]), gn_ref[...]).astype(y_ref.dtype)

    @pl.when(c == nc - 1)
    def _():
        for h in range(N_SSM_HEADS):
            g, hh = divmod(h, heads_per_group)
            hout_ref[0, h] = ht_scr[g, :, hh * SSM_HEAD_DIM:(hh + 1) * SSM_HEAD_DIM].T


def _pad_lanes(v):
    return jnp.zeros((1, SMALL_WIDTH), F32).at[0, SM_DT:SM_DT + N_SSM_HEADS].set(v)


def _ssd(big3, bc3, small3, small3_t, prefix, h0, conv_w, conv_b, dt_bias, a_log, d_skip, ssm_norm, *, l):
    b, t, _ = big3.shape
    nc = t // l
    body = functools.partial(_ssd_body, l=l)
    row = lambda v: v.reshape(1, -1)
    colv = lambda v: v.reshape(-1, 1)
    const2 = lambda shape: pl.BlockSpec(shape, lambda bi, c: (0, 0))
    return pl.pallas_call(
        body,
        grid=(b, nc),
        in_specs=[
            pl.BlockSpec((1, l, SSM_WIDTH), lambda bi, c: (bi, c, COL_X // SSM_WIDTH)),
            pl.BlockSpec((1, l, GROUP_WIDTH), lambda bi, c: (bi, c, 0)),
            pl.BlockSpec((1, l, GROUP_WIDTH), lambda bi, c: (bi, c, 1)),
            pl.BlockSpec((1, l, SSM_WIDTH), lambda bi, c: (bi, c, COL_Z // SSM_WIDTH)),
            pl.BlockSpec((1, l, SMALL_WIDTH), lambda bi, c: (bi, c, 0)),
            pl.BlockSpec((1, SMALL_WIDTH, l), lambda bi, c: (bi * nc + c, 0, 0)),
            pl.BlockSpec((1, CONV_W - 1, CONV_DIM), lambda bi, c: (bi, 0, 0)),
            pl.BlockSpec((1, N_SSM_HEADS, SSM_HEAD_DIM, SSM_STATE), lambda bi, c: (bi, 0, 0, 0)),
            const2((CONV_W, CONV_DIM)),
            const2((1, CONV_DIM)),
            const2((1, SMALL_WIDTH)),
            const2((N_SSM_HEADS, 1)),
            const2((1, SMALL_WIDTH)),
            const2((N_SSM_HEADS, 1)),
            const2((1, SSM_WIDTH)),
            const2((1, SSM_WIDTH)),
        ],
        out_specs=[
            pl.BlockSpec((1, l, SSM_WIDTH), lambda bi, c: (bi, c, 0)),
            pl.BlockSpec((1, CONV_W - 1, CONV_DIM), lambda bi, c: (bi, 0, 0)),
            pl.BlockSpec((1, N_SSM_HEADS, SSM_HEAD_DIM, SSM_STATE), lambda bi, c: (bi, 0, 0, 0)),
        ],
        out_shape=[
            jax.ShapeDtypeStruct((b, t, SSM_WIDTH), BF16),
            jax.ShapeDtypeStruct((b, CONV_W - 1, CONV_DIM), F32),
            jax.ShapeDtypeStruct((b, N_SSM_HEADS, SSM_HEAD_DIM, SSM_STATE), F32),
        ],
        scratch_shapes=[
            pltpu.VMEM((SUBLANES + l, CONV_DIM), F32),
            pltpu.VMEM((N_SSM_GROUPS, SSM_STATE, SSM_WIDTH // N_SSM_GROUPS), F32),
            pltpu.VMEM((l, SSM_WIDTH), F32),
        ],
        compiler_params=pltpu.CompilerParams(
            dimension_semantics=("parallel", "arbitrary"), vmem_limit_bytes=VMEM_LIMIT),
        name="ssd",
    )(big3, bc3, bc3, big3, small3, small3_t, prefix, h0,
      conv_w, row(conv_b), _pad_lanes(dt_bias), colv(dt_bias), _pad_lanes(a_log), colv(a_log),
      row(jnp.repeat(d_skip, SSM_HEAD_DIM)), row(ssm_norm))


def _ssd_sample_body(x_ref, bm_ref, cm_ref, z_ref, sm_ref, smt_ref, pre_ref, h0_ref,
                     cw_ref, cb_ref, dtb_ref, dtbt_ref, alog_ref, alogt_ref, dskip_ref, gn_ref,
                     y_ref, tail_ref, hout_ref, xp_scr, xc_scr, yi_scr, xw_scr, y_scr, *, gs, t):
    rows = gs * t
    off = SUBLANES
    tshift = t.bit_length() - 1
    gw = N_SSM_GROUPS * SSM_STATE
    heads_per_group = N_SSM_HEADS // N_SSM_GROUPS
    gcols = heads_per_group * SSM_HEAD_DIM

    for s in range(gs):
        xp_scr[s, off - (CONV_W - 1):off, :] = pre_ref[s]
        xp_scr[s, off:off + t, 0:SSM_WIDTH] = x_ref[s]
        xp_scr[s, off:off + t, SSM_WIDTH:SSM_WIDTH + gw] = bm_ref[s]
        xp_scr[s, off:off + t, SSM_WIDTH + gw:CONV_DIM] = cm_ref[s]
        conv = cb_ref[...]
        for j in range(CONV_W):
            s0 = off - (CONV_W - 1) + j
            conv = conv + xp_scr[s, s0:s0 + t, :] * cw_ref[j:j + 1, :]
        xc_scr[s * t:(s + 1) * t, :] = _silu(conv)
        tail_ref[s] = xp_scr[s, off + t - (CONV_W - 1):off + t, :]

    xc = xc_scr[...]
    xs = xc[:, :SSM_WIDTH]
    bcf = [xc[:, SSM_WIDTH + g * SSM_STATE:SSM_WIDTH + (g + 1) * SSM_STATE] for g in range(N_SSM_GROUPS)]
    ccf = [xc[:, SSM_WIDTH + gw + g * SSM_STATE:SSM_WIDTH + gw + (g + 1) * SSM_STATE] for g in range(N_SSM_GROUPS)]

    dt = _softplus(sm_ref[...].reshape(rows, SMALL_WIDTH)[:, SM_DT:SM_DT + N_SSM_HEADS] + dtb_ref[...])
    dtt = _softplus(smt_ref[0, SM_DT:SM_DT + N_SSM_HEADS, :] + dtbt_ref[...])
    a = dt * (-jnp.exp(alog_ref[...]))
    at = dtt * (-jnp.exp(alogt_ref[...]))

    ri = lax.broadcasted_iota(jnp.int32, (rows, rows), 0)
    ci = lax.broadcasted_iota(jnp.int32, (rows, rows), 1)
    same = (ri >> tshift) == (ci >> tshift)
    causal = jnp.logical_and(same, ri >= ci)
    tri = jnp.where(causal, 1.0, 0.0).astype(BF16)
    tri_u = jnp.where(jnp.logical_and(same, ri <= ci), 1.0, 0.0).astype(BF16)
    pick_last = jnp.where(ci == (((ri >> tshift) << tshift) + (t - 1)), 1.0, 0.0).astype(BF16)
    acum = sum(_dot(tri, piece) for piece in _split3(a))
    acum_t = sum(_dot(piece, tri_u) for piece in _split3(at))
    last = sum(_dot(pick_last, piece) for piece in _split3(acum))
    wts = jnp.exp(last - acum) * dt
    exp_last = jnp.exp(last)

    for s in range(gs):
        rs = slice(s * t, (s + 1) * t)
        for g in range(N_SSM_GROUPS):
            h0g = h0_ref[s, g * heads_per_group:(g + 1) * heads_per_group].reshape(gcols, SSM_STATE)
            yi_scr[rs, g * gcols:(g + 1) * gcols] = _dot_nt(ccf[g][rs, :], h0g)

    cb = [_dot_nt(ccf[g].astype(BF16), bcf[g].astype(BF16)) for g in range(N_SSM_GROUPS)]
    for h in range(N_SSM_HEADS):
        g = h // heads_per_group
        hs = slice(h * SSM_HEAD_DIM, (h + 1) * SSM_HEAD_DIM)
        col = acum[:, h:h + 1]
        decay = jnp.exp(jnp.where(causal, col - acum_t[h:h + 1, :], -jnp.inf))
        mmat = cb[g] * decay * dtt[h:h + 1, :]
        xh = xs[:, hs]
        y_intra = _dot(mmat.astype(BF16), xh.astype(BF16))
        y_scr[:, hs] = y_intra + yi_scr[:, hs] * jnp.exp(col) + dskip_ref[h] * xh
        xw_scr[:, hs] = xh * wts[:, h:h + 1]

    out = _rms(y_scr[...] * _silu(z_ref[...].reshape(rows, SSM_WIDTH)), gn_ref[...])
    for s in range(gs):
        y_ref[s] = out[s * t:(s + 1) * t, :].astype(y_ref.dtype)

    for s in range(gs):
        rs = slice(s * t, (s + 1) * t)
        for g in range(N_SSM_GROUPS):
            upd = _dot_tn(xw_scr[rs, g * gcols:(g + 1) * gcols], bcf[g][rs, :])
            for hh in range(heads_per_group):
                h = g * heads_per_group + hh
                scale = jnp.broadcast_to(exp_last[s * t:s * t + 1, h:h + 1], (SSM_HEAD_DIM, SSM_STATE))
                hout_ref[s, h] = h0_ref[s, h] * scale + upd[hh * SSM_HEAD_DIM:(hh + 1) * SSM_HEAD_DIM, :]


def _ssd_sample(big3, bc3, small3, small3_t, prefix, h0, conv_w, conv_b, dt_bias, a_log, d_skip, ssm_norm, *, gs):
    b, t, _ = big3.shape
    assert b % gs == 0 and t & (t - 1) == 0 and t == SUBLANES
    rows = gs * t
    body = functools.partial(_ssd_sample_body, gs=gs, t=t)
    gw = N_SSM_GROUPS * SSM_STATE
    row = lambda v: v.reshape(1, -1)
    colv = lambda v: v.reshape(-1, 1)
    const2 = lambda shape: pl.BlockSpec(shape, lambda gi: (0, 0))
    state_block = (gs, N_SSM_HEADS, SSM_HEAD_DIM, SSM_STATE)
    return pl.pallas_call(
        body,
        grid=(b // gs,),
        in_specs=[
            pl.BlockSpec((gs, t, SSM_WIDTH), lambda gi: (gi, 0, COL_X // SSM_WIDTH)),
            pl.BlockSpec((gs, t, gw), lambda gi: (gi, 0, 0)),
            pl.BlockSpec((gs, t, gw), lambda gi: (gi, 0, 1)),
            pl.BlockSpec((gs, t, SSM_WIDTH), lambda gi: (gi, 0, COL_Z // SSM_WIDTH)),
            pl.BlockSpec((gs, t, SMALL_WIDTH), lambda gi: (gi, 0, 0)),
            pl.BlockSpec((1, SMALL_WIDTH, rows), lambda gi: (gi, 0, 0)),
            pl.BlockSpec((gs, CONV_W - 1, CONV_DIM), lambda gi: (gi, 0, 0)),
            pl.BlockSpec(state_block, lambda gi: (gi, 0, 0, 0)),
            const2((CONV_W, CONV_DIM)),
            const2((1, CONV_DIM)),
            const2((1, N_SSM_HEADS)),
            const2((N_SSM_HEADS, 1)),
            const2((1, N_SSM_HEADS)),
            const2((N_SSM_HEADS, 1)),
            pl.BlockSpec(memory_space=pltpu.SMEM),
            const2((1, SSM_WIDTH)),
        ],
        out_specs=[
            pl.BlockSpec((gs, t, SSM_WIDTH), lambda gi: (gi, 0, 0)),
            pl.BlockSpec((gs, CONV_W - 1, CONV_DIM), lambda gi: (gi, 0, 0)),
            pl.BlockSpec(state_block, lambda gi: (gi, 0, 0, 0)),
        ],
        out_shape=[
            jax.ShapeDtypeStruct((b, t, SSM_WIDTH), BF16),
            jax.ShapeDtypeStruct((b, CONV_W - 1, CONV_DIM), F32),
            jax.ShapeDtypeStruct((b, N_SSM_HEADS, SSM_HEAD_DIM, SSM_STATE), F32),
        ],
        scratch_shapes=[
            pltpu.VMEM((gs, SUBLANES + t, CONV_DIM), F32),
            pltpu.VMEM((rows, CONV_DIM), F32),
            pltpu.VMEM((rows, SSM_WIDTH), F32),
            pltpu.VMEM((rows, SSM_WIDTH), F32),
            pltpu.VMEM((rows, SSM_WIDTH), F32),
        ],
        compiler_params=pltpu.CompilerParams(
            dimension_semantics=("parallel",), vmem_limit_bytes=VMEM_LIMIT),
        name="ssd_sample",
    )(big3, bc3, bc3, big3, small3, small3_t, prefix, h0,
      conv_w, row(conv_b), row(dt_bias), colv(dt_bias), row(a_log), colv(a_log), d_skip, row(ssm_norm))


def _ffn_body(x_ref, att_ref, y_ref, woa_ref, woy_ref, gf_ref, wg_ref, wu_ref, wd_ref, gl_ref,
              o_ref, f_scr):
    j = pl.program_id(1)

    @pl.when(j == 0)
    def _():
        h = x_ref[...] + _dot(att_ref[...], woa_ref[...]) + _dot(y_ref[...], woy_ref[...])
        o_ref[...] = h
        f_scr[...] = _rms(h, gf_ref[...]).astype(BF16)

    f = f_scr[...]
    act = _silu(_dot(f, wg_ref[...])) * _dot(f, wu_ref[...])
    o_ref[...] += _dot(act.astype(BF16), wd_ref[...])

    @pl.when(j == pl.num_programs(1) - 1)
    def _():
        o_ref[...] = _rms(o_ref[...], gl_ref[...])


def _out_ffn(x2, att2, y2, woa, woy, gf, wg, wu, wd, gl, *, tm, th):
    m, d = x2.shape
    hidden = wg.shape[1]
    resident = dict(pipeline_mode=pl.Buffered(1))
    return pl.pallas_call(
        _ffn_body,
        grid=(m // tm, hidden // th),
        in_specs=[
            pl.BlockSpec((tm, d), lambda i, j: (i, 0)),
            pl.BlockSpec((tm, ATT_WIDTH), lambda i, j: (i, 0)),
            pl.BlockSpec((tm, SSM_WIDTH), lambda i, j: (i, 0)),
            pl.BlockSpec((ATT_WIDTH, d), lambda i, j: (0, 0), **resident),
            pl.BlockSpec((SSM_WIDTH, d), lambda i, j: (0, 0), **resident),
            pl.BlockSpec((1, d), lambda i, j: (0, 0)),
            pl.BlockSpec((d, th), lambda i, j: (0, j)),
            pl.BlockSpec((d, th), lambda i, j: (0, j)),
            pl.BlockSpec((th, d), lambda i, j: (j, 0)),
            pl.BlockSpec((1, d), lambda i, j: (0, 0)),
        ],
        out_specs=pl.BlockSpec((tm, d), lambda i, j: (i, 0)),
        out_shape=jax.ShapeDtypeStruct((m, d), F32),
        scratch_shapes=[pltpu.VMEM((tm, d), BF16)],
        compiler_params=pltpu.CompilerParams(
            dimension_semantics=("parallel", "arbitrary"), vmem_limit_bytes=VMEM_LIMIT),
        name="out_ffn",
    )(x2, att2, y2, woa, woy, gf, wg, wu, wd, gl)


def _t5_bucket(rel):
    max_exact = N_BUCKETS // 2
    relf = jnp.maximum(rel, 1).astype(jnp.float32)
    large = max_exact + (jnp.log(relf / max_exact) / math.log(MAX_DISTANCE / max_exact)
                         * (N_BUCKETS - max_exact)).astype(jnp.int32)
    large = jnp.minimum(large, N_BUCKETS - 1)
    return jnp.where(rel < max_exact, rel, large)


def _toeplitz(u, rows, cols):
    nh, period = u.shape
    assert cols < period
    flat = jnp.tile(u, (1, rows))
    return flat[:, :rows * (period - 1)].reshape(nh, rows, period - 1)[:, :, :cols]


def _bias_tiles(rel_bias, rows, key_major):
    assert MAX_DISTANCE <= LANES and rows <= LANES
    table = rel_bias[_t5_bucket(jnp.arange(3 * LANES, dtype=jnp.int32))].T
    tiles = []
    for dd in (2, 1, 0):
        neg = table[:, (dd - 1) * LANES:dd * LANES] if dd >= 1 else jnp.tile(table[:, 0:1], (1, LANES))
        u = jnp.concatenate([table[:, dd * LANES:(dd + 1) * LANES], neg], axis=1)
        if key_major:
            tiles.append(_toeplitz(u, LANES, rows))
        else:
            w = jnp.concatenate([u[:, 0:1], jnp.flip(u[:, 1:], axis=1)], axis=1)
            tiles.append(_toeplitz(w, rows, LANES))
    return jnp.stack(tiles)


def _layer(x, att_call, ssd_call, blk_rows, p, *, kv_bf16):
    b, t, d = x.shape
    m = b * t
    tm = min(ROW_TILE, m)
    proj = _inproj(x.reshape(m, d), p["norm_attn"], p["w_t"], p["idx_k_norm"], tm=tm,
                   seq_len=t if kv_bf16 else None)
    big, bc, small, kin = proj[:4]
    big3 = big.reshape(b, t, BIG_WIDTH)
    bc3 = bc.reshape(b, t, BC_WIDTH)
    small3 = small.reshape(b, t, SMALL_WIDTH)
    kin3 = kin.reshape(b, t, IDX_DIM)
    hd = (b, t, N_ATT_HEADS, ATT_HEAD_DIM)
    k4 = big3[:, :, COL_K:COL_K + ATT_WIDTH].reshape(hd)
    v4 = big3[:, :, COL_V:COL_V + ATT_WIDTH].reshape(hd)
    small3_t = jnp.transpose(small.reshape(m // blk_rows, blk_rows, SMALL_WIDTH), (0, 2, 1))
    att = att_call(big3, small3, small3_t, kin3, k4, v4, *proj[4:])
    y, tail, h_fin = ssd_call(big3, bc3, small3, small3_t)
    out = _out_ffn(x.reshape(m, d), att.reshape(m, ATT_WIDTH), y.reshape(m, SSM_WIDTH), p["woa"], p["woy"],
                   p["norm_ffn"], p["wg"], p["wu"], p["wd"], p["norm_final"], tm=tm, th=FFN_TILE)
    return out.reshape(b, t, d), k4, v4, kin3, tail, h_fin


def kernel(x_prompt, x_sample, cache_k, cache_v, cache_kidx, state_conv, state_ssm, page_table, rel_bias,
           norm_attn, w_in, idx_k_norm, conv_w, conv_b, dt_bias, a_log, d_skip, ssm_norm, w_out, norm_ffn,
           w_gate, w_up, w_down, norm_final):
    depth = w_in.shape[0]
    assert depth == 1
    bp, s, d = x_prompt.shape
    bs, t, _ = x_sample.shape
    n_pages = page_table.shape[1]
    past = n_pages * PAGE_SIZE
    lyr = 0

    assert w_in.shape[2] == IN_WIDTH
    w_t = jnp.swapaxes(w_in[lyr], 0, 1).astype(BF16)
    row = lambda v: v.reshape(1, -1)
    p = dict(
        norm_attn=row(norm_attn[lyr]), w_t=w_t, idx_k_norm=row(idx_k_norm[lyr]),
        conv_w=conv_w[lyr], conv_b=conv_b[lyr], dt_bias=dt_bias[lyr], a_log=a_log[lyr], d_skip=d_skip[lyr],
        ssm_norm=ssm_norm[lyr],
        woa=w_out[lyr, :ATT_WIDTH].astype(BF16), woy=w_out[lyr, ATT_WIDTH:].astype(BF16),
        norm_ffn=row(norm_ffn[lyr]), wg=w_gate[lyr].astype(BF16), wu=w_up[lyr].astype(BF16),
        wd=w_down[lyr].astype(BF16), norm_final=row(norm_final),
    )

    topk_p = min(TOPK_MAX, s // 4)
    assert SSD_CHUNK == LANES and s % LANES == 0
    tiles_p = _bias_tiles(rel_bias, LANES, key_major=True)
    tiles_p = tiles_p - tiles_p[0:1]

    def att_prompt(big3, small3, small3_t, kin3, k4, v4, k16, vt16):
        return _attn_prompt(big3, small3_t, kin3, k16.reshape(bp, s, ATT_WIDTH), vt16, tiles_p,
                            topk=topk_p, group=CAUSAL_GROUP)

    zero_conv = jnp.zeros((bp, CONV_W - 1, CONV_DIM), F32)
    zero_ssm = jnp.zeros((bp, N_SSM_HEADS, SSM_HEAD_DIM, SSM_STATE), F32)
    ssm_params = (p["conv_w"], p["conv_b"], p["dt_bias"], p["a_log"], p["d_skip"], p["ssm_norm"])

    def ssd_prompt(big3, bc3, small3, small3_t):
        return _ssd(big3, bc3, small3, small3_t, zero_conv, zero_ssm, *ssm_params, l=SSD_CHUNK)

    yp, kp, vp, kip, cp, sp = _layer(x_prompt, att_prompt, ssd_prompt, SSD_CHUNK, p, kv_bf16=True)

    topk_s = min(TOPK_MAX, (past + t) // 4)
    tiles_s = jnp.repeat(_bias_tiles(rel_bias, t, key_major=False).reshape(3, N_ATT_HEADS * t, LANES),
                         N_ATT_HEADS, axis=-1)

    cache_kidx_t = jnp.swapaxes(cache_kidx, -1, -2)

    gs = math.gcd(bs, LANES // t)

    def att_sample(big3, small3, small3_t, kin3, k4, v4):
        sel = _select_sample(page_table, big3, small3, kin3, cache_kidx_t, topk=topk_s, gs=gs)
        return _attn_sample(page_table, big3, k4, v4, sel, tiles_s, cache_k, cache_v, gs=gs)

    def ssd_sample(big3, bc3, small3, small3_t):
        return _ssd_sample(big3, bc3, small3, small3_t, state_conv[lyr], state_ssm[lyr], *ssm_params, gs=gs)

    ys, ks, vs, kis, cs, ss = _layer(x_sample, att_sample, ssd_sample, gs * t, p, kv_bf16=False)

    st = lambda a: a[None]
    return (yp, ys, st(kp), st(vp), st(kip), st(cp), st(sp), st(ks), st(vs), st(kis), st(cs), st(ss))
```

```python
import functools
import math

import numpy as np
import jax
import jax.numpy as jnp
from jax import lax
from jax.experimental import pallas as pl
from jax.experimental.pallas import tpu as pltpu

N_ATT_HEADS = 8
ATT_HEAD_DIM = 128
ATT_WIDTH = N_ATT_HEADS * ATT_HEAD_DIM
N_IDX_HEADS = 16
IDX_DIM = 64
TOPK_MAX = 256
N_SSM_HEADS = 16
SSM_HEAD_DIM = 64
SSM_WIDTH = N_SSM_HEADS * SSM_HEAD_DIM
N_SSM_GROUPS = 2
SSM_STATE = 128
CONV_W = 4
CONV_DIM = SSM_WIDTH + 2 * N_SSM_GROUPS * SSM_STATE
SSD_CHUNK = 128
PAGE_SIZE = 128
N_BUCKETS = 32
MAX_DISTANCE = 128
NORM_EPS = 1e-6

LANES = 128
SUBLANES = 8
VMEM_LIMIT = 56 * 1024 * 1024

ROW_TILE = 512
FFN_TILE = 512
CAUSAL_GROUP = 2
SELECT_SEQS_PER_STEP = 4

COL_Q, COL_K, COL_V, COL_QI, COL_Z, COL_X = (i * ATT_WIDTH for i in range(6))
BIG_WIDTH = 6 * ATT_WIDTH
GROUP_WIDTH = N_SSM_GROUPS * SSM_STATE
BC_WIDTH = 2 * GROUP_WIDTH
SM_KI, SM_WI, SM_DT = 0, IDX_DIM, IDX_DIM + N_IDX_HEADS
SMALL_WIDTH = LANES
INPROJ_TN = 2 * ATT_WIDTH
IN_KI = 3 * ATT_WIDTH + N_IDX_HEADS * IDX_DIM
IN_Z = IN_KI + IDX_DIM + N_IDX_HEADS
IN_B = IN_Z + 2 * SSM_WIDTH
IN_WIDTH = IN_B + BC_WIDTH + N_SSM_HEADS

F32 = jnp.float32
BF16 = jnp.bfloat16

_INT_MIN = -(2 ** 31)
_INT_MAX = 2 ** 31 - 1
_KEY_NEG_INF = int(np.array([-np.inf], np.float32).view(np.int32)[0]) ^ 0x7FFFFFFF
_NEG_BIG = -1e30


def _dot(a, b):
    return jnp.dot(a, b, preferred_element_type=F32)


def _dot_nt(a, b):
    return lax.dot_general(a, b, (((1,), (1,)), ((), ())), preferred_element_type=F32)


def _dot_tn(a, b):
    return lax.dot_general(a, b, (((0,), (0,)), ((), ())), preferred_element_type=F32)


def _rms(x, g):
    return x * lax.rsqrt(jnp.mean(x * x, axis=-1, keepdims=True) + NORM_EPS) * g


def _silu(x):
    return x * (1.0 / (1.0 + jnp.exp(-x)))


def _inproj_body(x_ref, g_ref, wb_ref, wsa_ref, wsb_ref, gk_ref, big_ref, bc_ref, small_ref, kin_ref, *rest,
                 kv_bf16):
    u_scr = rest[-1]
    n = pl.program_id(1)

    @pl.when(n == 0)
    def _():
        ub = _rms(x_ref[...], g_ref[...]).astype(BF16)
        u_scr[...] = ub
        nar_a = _dot_nt(ub, wsa_ref[...])
        nar_b = _dot_nt(ub, wsb_ref[...])
        bc_ref[...] = nar_b[:, :BC_WIDTH]
        pad = jnp.zeros((ub.shape[0], SMALL_WIDTH - SM_DT - N_SSM_HEADS), F32)
        small_ref[...] = jnp.concatenate([nar_a, nar_b[:, BC_WIDTH:], pad], axis=1)
        kin_ref[...] = _rms(nar_a[:, :IDX_DIM], gk_ref[...])

    tile = _dot_nt(u_scr[...], wb_ref[...])
    big_ref[...] = tile
    if kv_bf16:
        k16_ref, vt16_ref = rest[:2]

        @pl.when(n == COL_K // INPROJ_TN)
        def _():
            k16_ref[...] = tile[:, COL_K % INPROJ_TN:COL_K % INPROJ_TN + ATT_WIDTH].astype(BF16)

        @pl.when(n == COL_V // INPROJ_TN)
        def _():
            vt16_ref[0] = tile[:, COL_V % INPROJ_TN:COL_V % INPROJ_TN + ATT_WIDTH].T.astype(BF16)


def _inproj(x2, g, w_t, gk, *, tm, seq_len=None):
    m, d = x2.shape
    kv_bf16 = seq_len is not None
    assert IN_Z + INPROJ_TN == IN_B and IN_KI == 2 * INPROJ_TN
    assert (IN_Z - IN_KI) % (2 * SUBLANES) == 0 and IN_B % (2 * SUBLANES) == 0
    big_rows = lambda i, j: (pl.multiple_of(j * INPROJ_TN + (j // 2) * (IN_Z - IN_KI), 2 * SUBLANES), 0)
    out_specs = [
        pl.BlockSpec((tm, INPROJ_TN), lambda i, j: (i, j)),
        pl.BlockSpec((tm, BC_WIDTH), lambda i, j: (i, 0)),
        pl.BlockSpec((tm, SMALL_WIDTH), lambda i, j: (i, 0)),
        pl.BlockSpec((tm, IDX_DIM), lambda i, j: (i, 0)),
    ]
    out_shape = [
        jax.ShapeDtypeStruct((m, BIG_WIDTH), F32),
        jax.ShapeDtypeStruct((m, BC_WIDTH), F32),
        jax.ShapeDtypeStruct((m, SMALL_WIDTH), F32),
        jax.ShapeDtypeStruct((m, IDX_DIM), F32),
    ]
    if kv_bf16:
        assert seq_len % tm == 0
        per_seq = seq_len // tm
        out_specs += [
            pl.BlockSpec((tm, ATT_WIDTH), lambda i, j: (i, 0)),
            pl.BlockSpec((1, ATT_WIDTH, tm), lambda i, j: (i // per_seq, 0, i % per_seq)),
        ]
        out_shape += [
            jax.ShapeDtypeStruct((m, ATT_WIDTH), BF16),
            jax.ShapeDtypeStruct((m // seq_len, ATT_WIDTH, seq_len), BF16),
        ]
    return pl.pallas_call(
        functools.partial(_inproj_body, kv_bf16=kv_bf16),
        grid=(m // tm, BIG_WIDTH // INPROJ_TN),
        in_specs=[
            pl.BlockSpec((tm, d), lambda i, j: (i, 0)),
            pl.BlockSpec((1, d), lambda i, j: (0, 0)),
            pl.BlockSpec((pl.Element(INPROJ_TN), pl.Element(d)), big_rows),
            pl.BlockSpec((pl.Element(IN_Z - IN_KI), pl.Element(d)), lambda i, j: (IN_KI, 0),
                         pipeline_mode=pl.Buffered(1)),
            pl.BlockSpec((pl.Element(IN_WIDTH - IN_B), pl.Element(d)), lambda i, j: (IN_B, 0),
                         pipeline_mode=pl.Buffered(1)),
            pl.BlockSpec((1, IDX_DIM), lambda i, j: (0, 0)),
        ],
        out_specs=out_specs,
        out_shape=out_shape,
        scratch_shapes=[pltpu.VMEM((tm, d), BF16)],
        compiler_params=pltpu.CompilerParams(
            dimension_semantics=("parallel", "arbitrary"), vmem_limit_bytes=VMEM_LIMIT),
        name="inproj",
    )(x2, g, w_t, w_t, w_t, gk)


def _sortable_key(s):
    s = jnp.where(s == 0.0, 0.0, s)
    b = pltpu.bitcast(s, jnp.int32)
    return b ^ ((b >> 31) & 0x7FFFFFFF)


def _count_keys(key_ref, nblk, pred, key_axis):
    def body(kb, acc):
        return acc + jnp.where(pred(key_ref[kb], kb), 1.0, 0.0)

    acc = jnp.zeros(key_ref.shape[1:], F32)
    if isinstance(nblk, int):
        for kb in range(nblk):
            acc = body(kb, acc)
    else:
        acc = lax.fori_loop(0, nblk, body, acc)
    return jnp.sum(acc, axis=key_axis, keepdims=True)


_I16_MIN = -(2 ** 15)


def _pack_halves(key):
    r = key.shape[0] // 2
    a, b = key[:r], key[r:]
    hi = (a & jnp.int32(-65536)) | ((b >> 16) & 0xFFFF)
    lo = ((a ^ 0x8000) << 16) | ((b ^ 0x8000) & 0xFFFF)
    return hi, lo


def _count16(ref, nblk, cmp, cand):
    r = ref.shape[1]
    pair = (cand << 16) | (cand & 0xFFFF)
    cb = pltpu.bitcast(jnp.broadcast_to(pair, (r, LANES)), jnp.int16)
    acc = jnp.zeros((2 * r, LANES), jnp.int16)
    for kb in range(nblk):
        acc = acc + jnp.where(cmp(pltpu.bitcast(ref[kb], jnp.int16), cb), jnp.int16(1), jnp.int16(0))
    words = pltpu.bitcast(acc, jnp.int32)
    both = (words >> 16) + ((words << 16) >> 16)
    return jnp.sum(both.astype(F32), axis=0, keepdims=True)


def _kth_of_16bit(ref, nblk, need):
    ge = lambda k, c: k >= c
    res = jnp.where(_count16(ref, nblk, ge, jnp.zeros((1, LANES), jnp.int32)) >= need, 0, _I16_MIN).astype(jnp.int32)

    def body(it, res):
        cand = res | jnp.left_shift(jnp.int32(1), 14 - it)
        return jnp.where(_count16(ref, nblk, ge, cand) >= need, cand, res)

    return lax.fori_loop(0, 15, body, res)


def _kth_key_two_phase(hi_ref, lo_ref, nblk, topk):
    kf = float(topk)
    top = _kth_of_16bit(hi_ref, nblk, kf)
    n_above = _count16(hi_ref, nblk, lambda k, c: k > c, top)
    r = hi_ref.shape[1]
    top_b = pltpu.bitcast(jnp.broadcast_to((top << 16) | (top & 0xFFFF), (r, LANES)), jnp.int16)
    for kb in range(nblk):
        band = pltpu.bitcast(hi_ref[kb], jnp.int16) == top_b
        lo_ref[kb] = pltpu.bitcast(jnp.where(band, pltpu.bitcast(lo_ref[kb], jnp.int16), jnp.int16(_I16_MIN)),
                                   jnp.int32)
    low = _kth_of_16bit(lo_ref, nblk, kf - n_above)
    return (top << 16) | ((low ^ 0x8000) & 0xFFFF)


def _topk_select_params(key_ref, nblk, topk, key_axis, thr=None):
    shape = key_ref.shape[1:]
    qshape = tuple(1 if a == key_axis else n for a, n in enumerate(shape))
    blk = shape[key_axis]
    kf = float(topk)

    def count_ge(cand):
        cb = jnp.broadcast_to(cand, shape)
        return _count_keys(key_ref, nblk, lambda k, kb: k >= cb, key_axis)

    if thr is None:
        res = jnp.where(count_ge(jnp.zeros(qshape, jnp.int32)) >= kf, 0, _INT_MIN).astype(jnp.int32)

        def bit_body(it, res):
            cand = res | jnp.left_shift(jnp.int32(1), 30 - it)
            return jnp.where(count_ge(cand) >= kf, cand, res)

        thr = lax.fori_loop(0, 31, bit_body, res)
    thr_b = jnp.broadcast_to(thr, shape)
    n_gt = _count_keys(key_ref, nblk, lambda k, kb: k > thr_b, key_axis)
    n_ge = count_ge(thr)
    tied = jnp.logical_and(n_ge > kf, thr > _KEY_NEG_INF)
    any_tied = jnp.max(jnp.where(tied, 1.0, 0.0)) > 0.0
    need = kf - n_gt
    kpos = lax.broadcasted_iota(jnp.int32, shape, key_axis)

    def tie_break():
        def body(it, cut):
            cand = cut | jnp.left_shift(jnp.int32(1), 30 - it)
            cand_b = jnp.broadcast_to(cand, shape)
            n = _count_keys(key_ref, nblk,
                            lambda k, kb: jnp.logical_and(k == thr_b, kpos + kb * blk < cand_b), key_axis)
            return jnp.where(n <= need, cand, cut)

        return lax.fori_loop(0, 31, body, jnp.zeros(qshape, jnp.int32))

    cut = lax.cond(any_tied, tie_break, lambda: jnp.full(qshape, _INT_MAX, jnp.int32))
    return thr, cut


def _attn_prompt_body(q_ref, qi_ref, smt_ref, kin_ref, k_ref, vt_ref, bias_ref, o_ref,
                      key_scr, hi_scr, lo_scr, mask_scr, lg_scr, qi_scr, *, topk, group):
    i = pl.program_id(1)
    q0 = i * LANES
    blk = (LANES, LANES)
    scale = ATT_HEAD_DIM ** -0.5
    nq = key_scr.shape[0]

    w_t = smt_ref[0, SM_WI:SM_WI + N_IDX_HEADS, :]
    qi_all = qi_ref[0]
    for h in range(N_IDX_HEADS):
        qi_scr[h * LANES:(h + 1) * LANES, :] = qi_all[:, h * IDX_DIM:(h + 1) * IDX_DIM].astype(BF16)
    qi = qi_scr[...]
    kpos = lax.broadcasted_iota(jnp.int32, blk, 0)
    qpos = lax.broadcasted_iota(jnp.int32, blk, 1) + q0
    qb = q_ref[0].astype(BF16)

    def tile(nb, first_near):
        for kb in range(nb):
            kin = kin_ref[0, kb * LANES:(kb + 1) * LANES, :].astype(BF16)
            s = _dot_nt(kin, qi)
            score = jnp.zeros(blk, F32)
            for h in range(N_IDX_HEADS):
                score = score + jnp.maximum(s[:, h * LANES:(h + 1) * LANES], 0.0) * w_t[h:h + 1, :]
            if kb >= first_near:
                score = jnp.where(kpos + kb * LANES <= qpos, score, -jnp.inf)
            key = _sortable_key(score)
            key_scr[kb] = key
            hi_scr[kb], lo_scr[kb] = _pack_halves(key)

        thr = _kth_key_two_phase(hi_scr, lo_scr, nb, topk)
        thr, cut = _topk_select_params(key_scr, nb, topk, key_axis=0, thr=thr)
        thr_b = jnp.broadcast_to(thr, blk)
        cut_b = jnp.broadcast_to(cut, blk)
        for kb in range(nb):
            kp = kpos + kb * LANES
            key = key_scr[kb]
            valid = jnp.logical_or(key > thr_b, jnp.logical_and(key == thr_b, kp < cut_b))
            if kb >= first_near:
                valid = jnp.logical_and(valid, kp <= qpos)
            mask_scr[kb * LANES:(kb + 1) * LANES, :] = jnp.where(valid, 0.0, -jnp.inf)

        nk = nb * LANES
        for h in range(N_ATT_HEADS):
            hs = slice(h * ATT_HEAD_DIM, (h + 1) * ATT_HEAD_DIM)
            lg = _dot_nt(k_ref[0, 0:nk, hs], qb[:, hs]) * scale + mask_scr[0:nk, :]
            for kb in range(first_near, nb):
                tix = 2 - jnp.clip(i - kb, 0, 2)
                lg_scr[kb * LANES:(kb + 1) * LANES, :] = lg[kb * LANES:(kb + 1) * LANES, :] + bias_ref[tix, h]
            if first_near > 0:
                lg_scr[0:first_near * LANES, :] = lg[0:first_near * LANES, :]
            lg = lg_scr[0:nk, :]
            m = jnp.max(lg, axis=0, keepdims=True)
            p = jnp.exp(lg - m)
            l = jnp.sum(p, axis=0, keepdims=True)
            out_t = _dot(vt_ref[0, hs, 0:nk], p.astype(BF16)) / l
            o_ref[0, :, hs] = out_t.T.astype(o_ref.dtype)

    ngroups = -(-nq // group)
    for c in range(ngroups):
        nb = min((c + 1) * group, nq)

        @pl.when(i // group == c)
        def _(nb=nb, c=c):
            tile(nb, max(c * group - 1, 0))


def _attn_prompt(big3, small3_t, kin3, k16, vt16, bias_tiles_t, *, topk, group):
    b, s, _ = big3.shape
    nq = s // LANES
    body = functools.partial(_attn_prompt_body, topk=topk, group=group)
    return pl.pallas_call(
        body,
        grid=(b, nq),
        in_specs=[
            pl.BlockSpec((1, LANES, ATT_WIDTH), lambda bi, i: (bi, i, COL_Q // ATT_WIDTH)),
            pl.BlockSpec((1, LANES, N_IDX_HEADS * IDX_DIM), lambda bi, i: (bi, i, COL_QI // (N_IDX_HEADS * IDX_DIM))),
            pl.BlockSpec((1, SMALL_WIDTH, LANES), lambda bi, i: (bi * nq + i, 0, 0)),
            pl.BlockSpec((1, s, IDX_DIM), lambda bi, i: (bi, 0, 0)),
            pl.BlockSpec((1, s, ATT_WIDTH), lambda bi, i: (bi, 0, 0)),
            pl.BlockSpec((1, ATT_WIDTH, s), lambda bi, i: (bi, 0, 0)),
            pl.BlockSpec(bias_tiles_t.shape, lambda bi, i: (0, 0, 0, 0)),
        ],
        out_specs=pl.BlockSpec((1, LANES, ATT_WIDTH), lambda bi, i: (bi, i, 0)),
        out_shape=jax.ShapeDtypeStruct((b, s, ATT_WIDTH), BF16),
        scratch_shapes=[
            pltpu.VMEM((nq, LANES, LANES), jnp.int32),
            pltpu.VMEM((nq, LANES // 2, LANES), jnp.int32),
            pltpu.VMEM((nq, LANES // 2, LANES), jnp.int32),
            pltpu.VMEM((s, LANES), F32),
            pltpu.VMEM((s, LANES), F32),
            pltpu.VMEM((N_IDX_HEADS * LANES, IDX_DIM), BF16),
        ],
        compiler_params=pltpu.CompilerParams(
            dimension_semantics=("parallel", "arbitrary"), vmem_limit_bytes=VMEM_LIMIT),
        name="attn_prompt",
    )(big3, big3, small3_t, kin3, k16, vt16, bias_tiles_t)


def _select_sample_body(pt_ref, qi_ref, sm_ref, kinew_ref, *rest, n_pages, t, topk, ss):
    kidx_refs = rest[:ss * n_pages]
    sel_ref = rest[ss * n_pages]
    key_scr, wb_scr = rest[ss * n_pages + 1:]
    nblk = n_pages + 1
    j = pl.program_id(1)
    trow = lax.broadcasted_iota(jnp.int32, (t, LANES), 0)
    lane_t = lax.broadcasted_iota(jnp.int32, (t, LANES), 1)

    def scores(s):
        return jnp.sum(jnp.maximum(s.reshape(N_IDX_HEADS, t, LANES), 0.0) * wb_scr[...], axis=0)

    for u in range(ss):
        r0 = pl.multiple_of((j * ss + u) * t, t)
        wi = sm_ref[u, :, SM_WI:SM_WI + N_IDX_HEADS]
        for h in range(N_IDX_HEADS):
            wb_scr[h] = jnp.broadcast_to(wi[:, h:h + 1], (t, LANES))
        qi_all = qi_ref[u]
        qi = jnp.concatenate([qi_all[:, h * IDX_DIM:(h + 1) * IDX_DIM] for h in range(N_IDX_HEADS)],
                             axis=0).astype(BF16)
        for p in range(n_pages):
            kin_t = kidx_refs[u * n_pages + p][0, 0].astype(BF16)
            key_scr[p, pl.ds(r0, t), :] = _sortable_key(scores(_dot(qi, kin_t)))
        kin_new = jnp.concatenate([kinew_ref[u], jnp.zeros((LANES - t, IDX_DIM), F32)], axis=0)
        s_new = jnp.where(lane_t <= trow, scores(_dot_nt(qi, kin_new.astype(BF16))), -jnp.inf)
        key_scr[n_pages, pl.ds(r0, t), :] = _sortable_key(s_new)

    @pl.when(j == pl.num_programs(1) - 1)
    def _():
        rows = key_scr.shape[1]
        thr, cut = _topk_select_params(key_scr, nblk, topk, key_axis=1)
        thr_b = jnp.broadcast_to(thr, (rows, LANES))
        cut_b = jnp.broadcast_to(cut, (rows, LANES))
        lane = lax.broadcasted_iota(jnp.int32, (rows, LANES), 1)
        tq = lax.broadcasted_iota(jnp.int32, (rows, LANES), 0) & (t - 1)
        for p in range(nblk):
            key = key_scr[p]
            sel = jnp.logical_or(key > thr_b, jnp.logical_and(key == thr_b, lane + p * LANES < cut_b))
            if p == n_pages:
                sel = jnp.logical_and(sel, lane <= tq)
            sel_ref[0, p] = jnp.where(sel, 1.0, 0.0)


def _select_sample(page_table, big3, small3, kin3, cache_kidx_t, *, topk, gs):
    b, t, _ = small3.shape
    n_pages = page_table.shape[1]
    ss = math.gcd(gs, SELECT_SEQS_PER_STEP)
    steps = gs // ss
    assert b % gs == 0 and t & (t - 1) == 0
    body = functools.partial(_select_sample_body, n_pages=n_pages, t=t, topk=topk, ss=ss)
    blk = lambda g, j: g * steps + j
    qi_width = N_IDX_HEADS * IDX_DIM
    in_specs = [
        pl.BlockSpec((ss, t, qi_width), lambda g, j, pt: (blk(g, j), 0, COL_QI // qi_width)),
        pl.BlockSpec((ss, t, SMALL_WIDTH), lambda g, j, pt: (blk(g, j), 0, 0)),
        pl.BlockSpec((ss, t, IDX_DIM), lambda g, j, pt: (blk(g, j), 0, 0)),
    ]
    in_specs += [pl.BlockSpec((1, 1, IDX_DIM, PAGE_SIZE),
                              lambda g, j, pt, u=u, p=p: (0, pt[blk(g, j) * ss + u, p], 0, 0))
                 for u in range(ss) for p in range(n_pages)]
    sel_shape = (b // gs, n_pages + 1, gs * t, LANES)
    return pl.pallas_call(
        body,
        grid_spec=pltpu.PrefetchScalarGridSpec(
            num_scalar_prefetch=1,
            grid=(b // gs, steps),
            in_specs=in_specs,
            out_specs=pl.BlockSpec((1,) + sel_shape[1:], lambda g, j, pt: (g, 0, 0, 0)),
            scratch_shapes=[
                pltpu.VMEM(sel_shape[1:], jnp.int32),
                pltpu.VMEM((N_IDX_HEADS, t, LANES), F32),
            ],
        ),
        out_shape=jax.ShapeDtypeStruct(sel_shape, F32),
        compiler_params=pltpu.CompilerParams(
            dimension_semantics=("parallel", "arbitrary"), vmem_limit_bytes=VMEM_LIMIT),
        name="select_sample",
    )(page_table, big3, small3, kin3, *([cache_kidx_t] * (ss * n_pages)))


def _attn_sample_body(pt_ref, q_ref, knew_ref, vnew_ref, sel_ref, bias_ref, *rest, n_pages, t):
    k_refs = rest[:n_pages]
    v_refs = rest[n_pages:2 * n_pages]
    o_ref = rest[2 * n_pages]
    (lg_scr,) = rest[2 * n_pages + 1:]
    nblk = n_pages + 1
    rows = N_ATT_HEADS * t
    flat = PAGE_SIZE * N_ATT_HEADS
    scale = ATT_HEAD_DIM ** -0.5

    q = q_ref[0]
    q64 = jnp.concatenate([q[:, h * ATT_HEAD_DIM:(h + 1) * ATT_HEAD_DIM] for h in range(N_ATT_HEADS)],
                          axis=0).astype(BF16)
    er = lax.broadcasted_iota(jnp.int32, (LANES, flat), 0)
    ec = lax.broadcasted_iota(jnp.int32, (LANES, flat), 1)
    expand = jnp.where((ec >> (N_ATT_HEADS.bit_length() - 1)) == er, 1.0, 0.0).astype(BF16)

    def flat_kv(refs, p):
        if p < n_pages:
            return refs[p][0, 0].reshape(flat, ATT_HEAD_DIM).astype(BF16), flat
        new = refs[p][0].reshape(t * N_ATT_HEADS, ATT_HEAD_DIM)
        pad = jnp.zeros((LANES - t * N_ATT_HEADS, ATT_HEAD_DIM), F32)
        return jnp.concatenate([new, pad], axis=0).astype(BF16), LANES

    k_all = list(k_refs) + [knew_ref]
    v_all = list(v_refs) + [vnew_ref]
    m = jnp.full((rows, 1), _NEG_BIG, F32)
    for p in range(nblk):
        kflat, width = flat_kv(k_all, p)
        selx = _dot(sel_ref[0, p].astype(BF16), expand[:, :width])
        unsel = (selx - 1.0) * (-_NEG_BIG)
        tix = 2 - min(n_pages - p, 2)
        lg = _dot_nt(q64, kflat) * scale + bias_ref[tix, :, :width] + jnp.concatenate([unsel] * N_ATT_HEADS, axis=0)
        lg_scr[p, :, :width] = lg
        m = jnp.maximum(m, jnp.max(lg, axis=-1, keepdims=True))

    l = jnp.zeros((rows, 1), F32)
    acc = jnp.zeros((rows, ATT_HEAD_DIM), F32)
    for p in range(nblk):
        vflat, width = flat_kv(v_all, p)
        pr = jnp.exp(lg_scr[p, :, :width] - m)
        l = l + jnp.sum(pr, axis=-1, keepdims=True)
        acc = acc + _dot(pr.astype(BF16), vflat)
    out = acc / l
    for h in range(N_ATT_HEADS):
        o_ref[0, :, h * ATT_HEAD_DIM:(h + 1) * ATT_HEAD_DIM] = out[h * t:(h + 1) * t, :].astype(o_ref.dtype)


def _attn_sample(page_table, big3, k_new, v_new, sel, bias_rows, cache_k, cache_v, *, gs):
    b, t, _ = big3.shape
    n_pages = page_table.shape[1]
    assert t == SUBLANES and N_ATT_HEADS & (N_ATT_HEADS - 1) == 0 and t * N_ATT_HEADS <= LANES
    body = functools.partial(_attn_sample_body, n_pages=n_pages, t=t)
    kv_block = (1, 1, PAGE_SIZE, N_ATT_HEADS, ATT_HEAD_DIM)
    new_block = (1, t, N_ATT_HEADS, ATT_HEAD_DIM)

    in_specs = [
        pl.BlockSpec((1, t, ATT_WIDTH), lambda bi, pt: (bi, 0, COL_Q // ATT_WIDTH)),
        pl.BlockSpec(new_block, lambda bi, pt: (bi, 0, 0, 0)),
        pl.BlockSpec(new_block, lambda bi, pt: (bi, 0, 0, 0)),
        pl.BlockSpec((1, n_pages + 1, t, LANES), lambda bi, pt: (bi // gs, 0, bi % gs, 0)),
        pl.BlockSpec(bias_rows.shape, lambda bi, pt: (0, 0, 0)),
    ]
    in_specs += [pl.BlockSpec(kv_block, lambda bi, pt, p=p: (0, pt[bi, p], 0, 0, 0)) for p in range(n_pages)] * 2
    rows = N_ATT_HEADS * t
    return pl.pallas_call(
        body,
        grid_spec=pltpu.PrefetchScalarGridSpec(
            num_scalar_prefetch=1,
            grid=(b,),
            in_specs=in_specs,
            out_specs=pl.BlockSpec((1, t, ATT_WIDTH), lambda bi, pt: (bi, 0, 0)),
            scratch_shapes=[pltpu.VMEM((n_pages + 1, rows, PAGE_SIZE * N_ATT_HEADS), F32)],
        ),
        out_shape=jax.ShapeDtypeStruct((b, t, ATT_WIDTH), BF16),
        compiler_params=pltpu.CompilerParams(
            dimension_semantics=("arbitrary",), vmem_limit_bytes=VMEM_LIMIT),
        name="attn_sample",
    )(page_table, big3, k_new, v_new, sel, bias_rows, *([cache_k] * n_pages), *([cache_v] * n_pages))


def _split3(a):
    a1 = a.astype(BF16)
    r1 = a - a1.astype(F32)
    a2 = r1.astype(BF16)
    a3 = (r1 - a2.astype(F32)).astype(BF16)
    return a1, a2, a3


def _softplus(v):
    return jnp.maximum(v, 0.0) + jnp.log1p(jnp.exp(-jnp.abs(v)))


def _head_expander():
    r = lax.broadcasted_iota(jnp.int32, (SMALL_WIDTH, SSM_WIDTH), 0)
    c = lax.broadcasted_iota(jnp.int32, (SMALL_WIDTH, SSM_WIDTH), 1)
    return jnp.where(r - SM_DT == c // SSM_HEAD_DIM, 1.0, 0.0).astype(BF16)


def _ssd_body(x_ref, bm_ref, cm_ref, z_ref, sm_ref, smt_ref, pre_ref, h0_ref,
              cw_ref, cb_ref, dtb_ref, dtbt_ref, alog_ref, alogt_ref, dskipx_ref, gn_ref,
              y_ref, tail_ref, hout_ref, xp_scr, ht_scr, y_scr, *, l):
    c = pl.program_id(1)
    nc = pl.num_programs(1)
    off = SUBLANES
    heads_per_group = N_SSM_HEADS // N_SSM_GROUPS
    gcols = heads_per_group * SSM_HEAD_DIM

    @pl.when(c == 0)
    def _():
        xp_scr[off - (CONV_W - 1):off, :] = pre_ref[0]
        for h in range(N_SSM_HEADS):
            g, hh = divmod(h, heads_per_group)
            ht_scr[g, :, hh * SSM_HEAD_DIM:(hh + 1) * SSM_HEAD_DIM] = h0_ref[0, h].T

    xp_scr[off:off + l, 0:SSM_WIDTH] = x_ref[0]
    xp_scr[off:off + l, SSM_WIDTH:SSM_WIDTH + GROUP_WIDTH] = bm_ref[0]
    xp_scr[off:off + l, SSM_WIDTH + GROUP_WIDTH:CONV_DIM] = cm_ref[0]

    conv = cb_ref[...]
    for j in range(CONV_W):
        s0 = off - (CONV_W - 1) + j
        conv = conv + xp_scr[s0:s0 + l, :] * cw_ref[j:j + 1, :]
    xc = _silu(conv)
    tail = xp_scr[off + l - (CONV_W - 1):off + l, :]
    tail_ref[0] = tail
    xp_scr[off - (CONV_W - 1):off, :] = tail

    xs = xc[:, :SSM_WIDTH]
    bc = [xc[:, SSM_WIDTH + g * SSM_STATE:SSM_WIDTH + (g + 1) * SSM_STATE].astype(BF16)
          for g in range(N_SSM_GROUPS)]
    cc = [xc[:, SSM_WIDTH + GROUP_WIDTH + g * SSM_STATE:SSM_WIDTH + GROUP_WIDTH + (g + 1) * SSM_STATE]
          .astype(BF16) for g in range(N_SSM_GROUPS)]

    dt = _softplus(sm_ref[0] + dtb_ref[...])
    a = dt * (-jnp.exp(alog_ref[...]))
    dtt = _softplus(smt_ref[0, SM_DT:SM_DT + N_SSM_HEADS, :] + dtbt_ref[...])
    at = dtt * (-jnp.exp(alogt_ref[...]))

    ri = lax.broadcasted_iota(jnp.int32, (l, l), 0)
    ci = lax.broadcasted_iota(jnp.int32, (l, l), 1)
    causal = ri >= ci
    tri = jnp.where(causal, 1.0, 0.0).astype(BF16)
    tri_u = jnp.where(ri <= ci, 1.0, 0.0).astype(BF16)
    acum = sum(_dot(tri, piece) for piece in _split3(a))
    acum_t = sum(_dot(piece, tri_u) for piece in _split3(at))
    last = acum[l - 1:l, :]
    wts = jnp.exp(last - acum) * dt
    expand = _head_expander()
    acum_x = sum(_dot(piece, expand) for piece in _split3(acum))
    wts_x = sum(_dot(piece, expand) for piece in _split3(wts))
    exp_last_x = jnp.exp(acum_x[l - 1:l, :])

    cb = [_dot_nt(cc[g], bc[g]) for g in range(N_SSM_GROUPS)]
    for h in range(N_SSM_HEADS):
        g = h // heads_per_group
        hs = slice(h * SSM_HEAD_DIM, (h + 1) * SSM_HEAD_DIM)
        col = acum[:, SM_DT + h:SM_DT + h + 1]
        decay = jnp.exp(jnp.where(causal, col - acum_t[h:h + 1, :], -jnp.inf))
        mmat = cb[g] * decay * dtt[h:h + 1, :]
        y_scr[:, hs] = _dot(mmat.astype(BF16), xs[:, hs].astype(BF16))

    xw = (xs * wts_x).astype(BF16)
    y_inter = []
    for g in range(N_SSM_GROUPS):
        gsl = slice(g * gcols, (g + 1) * gcols)
        ht = ht_scr[g]
        y_inter.append(_dot(cc[g], ht.astype(BF16)))
        ht_scr[g] = ht * exp_last_x[:, gsl] + _dot_tn(bc[g], xw[:, gsl])
    y = y_scr[...] + jnp.concatenate(y_inter, axis=1) * jnp.exp(acum_x) + dskipx_ref[...] * xs
    y_ref[0] = _rms(y * _silu(z_ref[0]), gn_ref[...]).astype(y_ref.dtype)

    @pl.when(c == nc - 1)
    def _():
        for h in range(N_SSM_HEADS):
            g, hh = divmod(h, heads_per_group)
            hout_ref[0, h] = ht_scr[g, :, hh * SSM_HEAD_DIM:(hh + 1) * SSM_HEAD_DIM].T


def _pad_lanes(v):
    return jnp.zeros((1, SMALL_WIDTH), F32).at[0, SM_DT:SM_DT + N_SSM_HEADS].set(v)


def _ssd(big3, bc3, small3, small3_t, prefix, h0, conv_w, conv_b, dt_bias, a_log, d_skip, ssm_norm, *, l):
    b, t, _ = big3.shape
    nc = t // l
    body = functools.partial(_ssd_body, l=l)
    row = lambda v: v.reshape(1, -1)
    colv = lambda v: v.reshape(-1, 1)
    const2 = lambda shape: pl.BlockSpec(shape, lambda bi, c: (0, 0))
    return pl.pallas_call(
        body,
        grid=(b, nc),
        in_specs=[
            pl.BlockSpec((1, l, SSM_WIDTH), lambda bi, c: (bi, c, COL_X // SSM_WIDTH)),
            pl.BlockSpec((1, l, GROUP_WIDTH), lambda bi, c: (bi, c, 0)),
            pl.BlockSpec((1, l, GROUP_WIDTH), lambda bi, c: (bi, c, 1)),
            pl.BlockSpec((1, l, SSM_WIDTH), lambda bi, c: (bi, c, COL_Z // SSM_WIDTH)),
            pl.BlockSpec((1, l, SMALL_WIDTH), lambda bi, c: (bi, c, 0)),
            pl.BlockSpec((1, SMALL_WIDTH, l), lambda bi, c: (bi * nc + c, 0, 0)),
            pl.BlockSpec((1, CONV_W - 1, CONV_DIM), lambda bi, c: (bi, 0, 0)),
            pl.BlockSpec((1, N_SSM_HEADS, SSM_HEAD_DIM, SSM_STATE), lambda bi, c: (bi, 0, 0, 0)),
            const2((CONV_W, CONV_DIM)),
            const2((1, CONV_DIM)),
            const2((1, SMALL_WIDTH)),
            const2((N_SSM_HEADS, 1)),
            const2((1, SMALL_WIDTH)),
            const2((N_SSM_HEADS, 1)),
            const2((1, SSM_WIDTH)),
            const2((1, SSM_WIDTH)),
        ],
        out_specs=[
            pl.BlockSpec((1, l, SSM_WIDTH), lambda bi, c: (bi, c, 0)),
            pl.BlockSpec((1, CONV_W - 1, CONV_DIM), lambda bi, c: (bi, 0, 0)),
            pl.BlockSpec((1, N_SSM_HEADS, SSM_HEAD_DIM, SSM_STATE), lambda bi, c: (bi, 0, 0, 0)),
        ],
        out_shape=[
            jax.ShapeDtypeStruct((b, t, SSM_WIDTH), BF16),
            jax.ShapeDtypeStruct((b, CONV_W - 1, CONV_DIM), F32),
            jax.ShapeDtypeStruct((b, N_SSM_HEADS, SSM_HEAD_DIM, SSM_STATE), F32),
        ],
        scratch_shapes=[
            pltpu.VMEM((SUBLANES + l, CONV_DIM), F32),
            pltpu.VMEM((N_SSM_GROUPS, SSM_STATE, SSM_WIDTH // N_SSM_GROUPS), F32),
            pltpu.VMEM((l, SSM_WIDTH), F32),
        ],
        compiler_params=pltpu.CompilerParams(
            dimension_semantics=("parallel", "arbitrary"), vmem_limit_bytes=VMEM_LIMIT),
        name="ssd",
    )(big3, bc3, bc3, big3, small3, small3_t, prefix, h0,
      conv_w, row(conv_b), _pad_lanes(dt_bias), colv(dt_bias), _pad_lanes(a_log), colv(a_log),
      row(jnp.repeat(d_skip, SSM_HEAD_DIM)), row(ssm_norm))


def _ssd_sample_body(x_ref, bm_ref, cm_ref, z_ref, sm_ref, smt_ref, pre_ref, h0_ref,
                     cw_ref, cb_ref, dtb_ref, dtbt_ref, alog_ref, alogt_ref, dskip_ref, gn_ref,
                     y_ref, tail_ref, hout_ref, xp_scr, xc_scr, yi_scr, xw_scr, y_scr, *, gs, t):
    rows = gs * t
    off = SUBLANES
    tshift = t.bit_length() - 1
    gw = N_SSM_GROUPS * SSM_STATE
    heads_per_group = N_SSM_HEADS // N_SSM_GROUPS
    gcols = heads_per_group * SSM_HEAD_DIM

    for s in range(gs):
        xp_scr[s, off - (CONV_W - 1):off, :] = pre_ref[s]
        xp_scr[s, off:off + t, 0:SSM_WIDTH] = x_ref[s]
        xp_scr[s, off:off + t, SSM_WIDTH:SSM_WIDTH + gw] = bm_ref[s]
        xp_scr[s, off:off + t, SSM_WIDTH + gw:CONV_DIM] = cm_ref[s]
        conv = cb_ref[...]
        for j in range(CONV_W):
            s0 = off - (CONV_W - 1) + j
            conv = conv + xp_scr[s, s0:s0 + t, :] * cw_ref[j:j + 1, :]
        xc_scr[s * t:(s + 1) * t, :] = _silu(conv)
        tail_ref[s] = xp_scr[s, off + t - (CONV_W - 1):off + t, :]

    xc = xc_scr[...]
    xs = xc[:, :SSM_WIDTH]
    bcf = [xc[:, SSM_WIDTH + g * SSM_STATE:SSM_WIDTH + (g + 1) * SSM_STATE] for g in range(N_SSM_GROUPS)]
    ccf = [xc[:, SSM_WIDTH + gw + g * SSM_STATE:SSM_WIDTH + gw + (g + 1) * SSM_STATE] for g in range(N_SSM_GROUPS)]

    dt = _softplus(sm_ref[...].reshape(rows, SMALL_WIDTH)[:, SM_DT:SM_DT + N_SSM_HEADS] + dtb_ref[...])
    dtt = _softplus(smt_ref[0, SM_DT:SM_DT + N_SSM_HEADS, :] + dtbt_ref[...])
    a = dt * (-jnp.exp(alog_ref[...]))
    at = dtt * (-jnp.exp(alogt_ref[...]))

    ri = lax.broadcasted_iota(jnp.int32, (rows, rows), 0)
    ci = lax.broadcasted_iota(jnp.int32, (rows, rows), 1)
    same = (ri >> tshift) == (ci >> tshift)
    causal = jnp.logical_and(same, ri >= ci)
    tri = jnp.where(causal, 1.0, 0.0).astype(BF16)
    tri_u = jnp.where(jnp.logical_and(same, ri <= ci), 1.0, 0.0).astype(BF16)
    pick_last = jnp.where(ci == (((ri >> tshift) << tshift) + (t - 1)), 1.0, 0.0).astype(BF16)
    acum = sum(_dot(tri, piece) for piece in _split3(a))
    acum_t = sum(_dot(piece, tri_u) for piece in _split3(at))
    last = sum(_dot(pick_last, piece) for piece in _split3(acum))
    wts = jnp.exp(last - acum) * dt
    exp_last = jnp.exp(last)

    for s in range(gs):
        rs = slice(s * t, (s + 1) * t)
        for g in range(N_SSM_GROUPS):
            h0g = h0_ref[s, g * heads_per_group:(g + 1) * heads_per_group].reshape(gcols, SSM_STATE)
            yi_scr[rs, g * gcols:(g + 1) * gcols] = _dot_nt(ccf[g][rs, :], h0g)

    cb = [_dot_nt(ccf[g].astype(BF16), bcf[g].astype(BF16)) for g in range(N_SSM_GROUPS)]
    for h in range(N_SSM_HEADS):
        g = h // heads_per_group
        hs = slice(h * SSM_HEAD_DIM, (h + 1) * SSM_HEAD_DIM)
        col = acum[:, h:h + 1]
        decay = jnp.exp(jnp.where(causal, col - acum_t[h:h + 1, :], -jnp.inf))
        mmat = cb[g] * decay * dtt[h:h + 1, :]
        xh = xs[:, hs]
        y_intra = _dot(mmat.astype(BF16), xh.astype(BF16))
        y_scr[:, hs] = y_intra + yi_scr[:, hs] * jnp.exp(col) + dskip_ref[h] * xh
        xw_scr[:, hs] = xh * wts[:, h:h + 1]

    out = _rms(y_scr[...] * _silu(z_ref[...].reshape(rows, SSM_WIDTH)), gn_ref[...])
    for s in range(gs):
        y_ref[s] = out[s * t:(s + 1) * t, :].astype(y_ref.dtype)

    for s in range(gs):
        rs = slice(s * t, (s + 1) * t)
        for g in range(N_SSM_GROUPS):
            upd = _dot_tn(xw_scr[rs, g * gcols:(g + 1) * gcols], bcf[g][rs, :])
            for hh in range(heads_per_group):
                h = g * heads_per_group + hh
                scale = jnp.broadcast_to(exp_last[s * t:s * t + 1, h:h + 1], (SSM_HEAD_DIM, SSM_STATE))
                hout_ref[s, h] = h0_ref[s, h] * scale + upd[hh * SSM_HEAD_DIM:(hh + 1) * SSM_HEAD_DIM, :]


def _ssd_sample(big3, bc3, small3, small3_t, prefix, h0, conv_w, conv_b, dt_bias, a_log, d_skip, ssm_norm, *, gs):
    b, t, _ = big3.shape
    assert b % gs == 0 and t & (t - 1) == 0 and t == SUBLANES
    rows = gs * t
    body = functools.partial(_ssd_sample_body, gs=gs, t=t)
    gw = N_SSM_GROUPS * SSM_STATE
    row = lambda v: v.reshape(1, -1)
    colv = lambda v: v.reshape(-1, 1)
    const2 = lambda shape: pl.BlockSpec(shape, lambda gi: (0, 0))
    state_block = (gs, N_SSM_HEADS, SSM_HEAD_DIM, SSM_STATE)
    return pl.pallas_call(
        body,
        grid=(b // gs,),
        in_specs=[
            pl.BlockSpec((gs, t, SSM_WIDTH), lambda gi: (gi, 0, COL_X // SSM_WIDTH)),
            pl.BlockSpec((gs, t, gw), lambda gi: (gi, 0, 0)),
            pl.BlockSpec((gs, t, gw), lambda gi: (gi, 0, 1)),
            pl.BlockSpec((gs, t, SSM_WIDTH), lambda gi: (gi, 0, COL_Z // SSM_WIDTH)),
            pl.BlockSpec((gs, t, SMALL_WIDTH), lambda gi: (gi, 0, 0)),
            pl.BlockSpec((1, SMALL_WIDTH, rows), lambda gi: (gi, 0, 0)),
            pl.BlockSpec((gs, CONV_W - 1, CONV_DIM), lambda gi: (gi, 0, 0)),
            pl.BlockSpec(state_block, lambda gi: (gi, 0, 0, 0)),
            const2((CONV_W, CONV_DIM)),
            const2((1, CONV_DIM)),
            const2((1, N_SSM_HEADS)),
            const2((N_SSM_HEADS, 1)),
            const2((1, N_SSM_HEADS)),
            const2((N_SSM_HEADS, 1)),
            pl.BlockSpec(memory_space=pltpu.SMEM),
            const2((1, SSM_WIDTH)),
        ],
        out_specs=[
            pl.BlockSpec((gs, t, SSM_WIDTH), lambda gi: (gi, 0, 0)),
            pl.BlockSpec((gs, CONV_W - 1, CONV_DIM), lambda gi: (gi, 0, 0)),
            pl.BlockSpec(state_block, lambda gi: (gi, 0, 0, 0)),
        ],
        out_shape=[
            jax.ShapeDtypeStruct((b, t, SSM_WIDTH), BF16),
            jax.ShapeDtypeStruct((b, CONV_W - 1, CONV_DIM), F32),
            jax.ShapeDtypeStruct((b, N_SSM_HEADS, SSM_HEAD_DIM, SSM_STATE), F32),
        ],
        scratch_shapes=[
            pltpu.VMEM((gs, SUBLANES + t, CONV_DIM), F32),
            pltpu.VMEM((rows, CONV_DIM), F32),
            pltpu.VMEM((rows, SSM_WIDTH), F32),
            pltpu.VMEM((rows, SSM_WIDTH), F32),
            pltpu.VMEM((rows, SSM_WIDTH), F32),
        ],
        compiler_params=pltpu.CompilerParams(
            dimension_semantics=("parallel",), vmem_limit_bytes=VMEM_LIMIT),
        name="ssd_sample",
    )(big3, bc3, bc3, big3, small3, small3_t, prefix, h0,
      conv_w, row(conv_b), row(dt_bias), colv(dt_bias), row(a_log), colv(a_log), d_skip, row(ssm_norm))


def _ffn_body(x_ref, att_ref, y_ref, woa_ref, woy_ref, gf_ref, wg_ref, wu_ref, wd_ref, gl_ref,
              o_ref, f_scr):
    j = pl.program_id(1)

    @pl.when(j == 0)
    def _():
        h = x_ref[...] + _dot(att_ref[...], woa_ref[...]) + _dot(y_ref[...], woy_ref[...])
        o_ref[...] = h
        f_scr[...] = _rms(h, gf_ref[...]).astype(BF16)

    f = f_scr[...]
    act = _silu(_dot(f, wg_ref[...])) * _dot(f, wu_ref[...])
    o_ref[...] += _dot(act.astype(BF16), wd_ref[...])

    @pl.when(j == pl.num_programs(1) - 1)
    def _():
        o_ref[...] = _rms(o_ref[...], gl_ref[...])


def _out_ffn(x2, att2, y2, woa, woy, gf, wg, wu, wd, gl, *, tm, th):
    m, d = x2.shape
    hidden = wg.shape[1]
    resident = dict(pipeline_mode=pl.Buffered(1))
    return pl.pallas_call(
        _ffn_body,
        grid=(m // tm, hidden // th),
        in_specs=[
            pl.BlockSpec((tm, d), lambda i, j: (i, 0)),
            pl.BlockSpec((tm, ATT_WIDTH), lambda i, j: (i, 0)),
            pl.BlockSpec((tm, SSM_WIDTH), lambda i, j: (i, 0)),
            pl.BlockSpec((ATT_WIDTH, d), lambda i, j: (0, 0), **resident),
            pl.BlockSpec((SSM_WIDTH, d), lambda i, j: (0, 0), **resident),
            pl.BlockSpec((1, d), lambda i, j: (0, 0)),
            pl.BlockSpec((d, th), lambda i, j: (0, j)),
            pl.BlockSpec((d, th), lambda i, j: (0, j)),
            pl.BlockSpec((th, d), lambda i, j: (j, 0)),
            pl.BlockSpec((1, d), lambda i, j: (0, 0)),
        ],
        out_specs=pl.BlockSpec((tm, d), lambda i, j: (i, 0)),
        out_shape=jax.ShapeDtypeStruct((m, d), F32),
        scratch_shapes=[pltpu.VMEM((tm, d), BF16)],
        compiler_params=pltpu.CompilerParams(
            dimension_semantics=("parallel", "arbitrary"), vmem_limit_bytes=VMEM_LIMIT),
        name="out_ffn",
    )(x2, att2, y2, woa, woy, gf, wg, wu, wd, gl)


def _t5_bucket(rel):
    max_exact = N_BUCKETS // 2
    relf = jnp.maximum(rel, 1).astype(jnp.float32)
    large = max_exact + (jnp.log(relf / max_exact) / math.log(MAX_DISTANCE / max_exact)
                         * (N_BUCKETS - max_exact)).astype(jnp.int32)
    large = jnp.minimum(large, N_BUCKETS - 1)
    return jnp.where(rel < max_exact, rel, large)


def _toeplitz(u, rows, cols):
    nh, period = u.shape
    assert cols < period
    flat = jnp.tile(u, (1, rows))
    return flat[:, :rows * (period - 1)].reshape(nh, rows, period - 1)[:, :, :cols]


def _bias_tiles(rel_bias, rows, key_major):
    assert MAX_DISTANCE <= LANES and rows <= LANES
    table = rel_bias[_t5_bucket(jnp.arange(3 * LANES, dtype=jnp.int32))].T
    tiles = []
    for dd in (2, 1, 0):
        neg = table[:, (dd - 1) * LANES:dd * LANES] if dd >= 1 else jnp.tile(table[:, 0:1], (1, LANES))
        u = jnp.concatenate([table[:, dd * LANES:(dd + 1) * LANES], neg], axis=1)
        if key_major:
            tiles.append(_toeplitz(u, LANES, rows))
        else:
            w = jnp.concatenate([u[:, 0:1], jnp.flip(u[:, 1:], axis=1)], axis=1)
            tiles.append(_toeplitz(w, rows, LANES))
    return jnp.stack(tiles)


def _layer(x, att_call, ssd_call, blk_rows, p, *, kv_bf16):
    b, t, d = x.shape
    m = b * t
    tm = min(ROW_TILE, m)
    proj = _inproj(x.reshape(m, d), p["norm_attn"], p["w_t"], p["idx_k_norm"], tm=tm,
                   seq_len=t if kv_bf16 else None)
    big, bc, small, kin = proj[:4]
    big3 = big.reshape(b, t, BIG_WIDTH)
    bc3 = bc.reshape(b, t, BC_WIDTH)
    small3 = small.reshape(b, t, SMALL_WIDTH)
    kin3 = kin.reshape(b, t, IDX_DIM)
    hd = (b, t, N_ATT_HEADS, ATT_HEAD_DIM)
    k4 = big3[:, :, COL_K:COL_K + ATT_WIDTH].reshape(hd)
    v4 = big3[:, :, COL_V:COL_V + ATT_WIDTH].reshape(hd)
    small3_t = jnp.transpose(small.reshape(m // blk_rows, blk_rows, SMALL_WIDTH), (0, 2, 1))
    att = att_call(big3, small3, small3_t, kin3, k4, v4, *proj[4:])
    y, tail, h_fin = ssd_call(big3, bc3, small3, small3_t)
    out = _out_ffn(x.reshape(m, d), att.reshape(m, ATT_WIDTH), y.reshape(m, SSM_WIDTH), p["woa"], p["woy"],
                   p["norm_ffn"], p["wg"], p["wu"], p["wd"], p["norm_final"], tm=tm, th=FFN_TILE)
    return out.reshape(b, t, d), k4, v4, kin3, tail, h_fin


def kernel(x_prompt, x_sample, cache_k, cache_v, cache_kidx, state_conv, state_ssm, page_table, rel_bias,
           norm_attn, w_in, idx_k_norm, conv_w, conv_b, dt_bias, a_log, d_skip, ssm_norm, w_out, norm_ffn,
           w_gate, w_up, w_down, norm_final):
    depth = w_in.shape[0]
    assert depth == 1
    bp, s, d = x_prompt.shape
    bs, t, _ = x_sample.shape
    n_pages = page_table.shape[1]
    past = n_pages * PAGE_SIZE
    lyr = 0

    assert w_in.shape[2] == IN_WIDTH
    w_t = jnp.swapaxes(w_in[lyr], 0, 1).astype(BF16)
    row = lambda v: v.reshape(1, -1)
    p = dict(
        norm_attn=row(norm_attn[lyr]), w_t=w_t, idx_k_norm=row(idx_k_norm[lyr]),
        conv_w=conv_w[lyr], conv_b=conv_b[lyr], dt_bias=dt_bias[lyr], a_log=a_log[lyr], d_skip=d_skip[lyr],
        ssm_norm=ssm_norm[lyr],
        woa=w_out[lyr, :ATT_WIDTH].astype(BF16), woy=w_out[lyr, ATT_WIDTH:].astype(BF16),
        norm_ffn=row(norm_ffn[lyr]), wg=w_gate[lyr].astype(BF16), wu=w_up[lyr].astype(BF16),
        wd=w_down[lyr].astype(BF16), norm_final=row(norm_final),
    )

    topk_p = min(TOPK_MAX, s // 4)
    assert SSD_CHUNK == LANES and s % LANES == 0
    tiles_p = _bias_tiles(rel_bias, LANES, key_major=True)
    tiles_p = tiles_p - tiles_p[0:1]

    def att_prompt(big3, small3, small3_t, kin3, k4, v4, k16, vt16):
        return _attn_prompt(big3, small3_t, kin3, k16.reshape(bp, s, ATT_WIDTH), vt16, tiles_p,
                            topk=topk_p, group=CAUSAL_GROUP)

    zero_conv = jnp.zeros((bp, CONV_W - 1, CONV_DIM), F32)
    zero_ssm = jnp.zeros((bp, N_SSM_HEADS, SSM_HEAD_DIM, SSM_STATE), F32)
    ssm_params = (p["conv_w"], p["conv_b"], p["dt_bias"], p["a_log"], p["d_skip"], p["ssm_norm"])

    def ssd_prompt(big3, bc3, small3, small3_t):
        return _ssd(big3, bc3, small3, small3_t, zero_conv, zero_ssm, *ssm_params, l=SSD_CHUNK)

    yp, kp, vp, kip, cp, sp = _layer(x_prompt, att_prompt, ssd_prompt, SSD_CHUNK, p, kv_bf16=True)

    topk_s = min(TOPK_MAX, (past + t) // 4)
    tiles_s = jnp.repeat(_bias_tiles(rel_bias, t, key_major=False).reshape(3, N_ATT_HEADS * t, LANES),
                         N_ATT_HEADS, axis=-1)
    lane_head = jnp.arange(LANES * N_ATT_HEADS, dtype=jnp.int32) % N_ATT_HEADS
    row_head = jnp.arange(N_ATT_HEADS * t, dtype=jnp.int32) // t
    tiles_s = jnp.where(lane_head[None, :] == row_head[:, None], tiles_s, -jnp.inf)

    cache_kidx_t = jnp.swapaxes(cache_kidx, -1, -2)

    gs = math.gcd(bs, LANES // t)

    def att_sample(big3, small3, small3_t, kin3, k4, v4):
        sel = _select_sample(page_table, big3, small3, kin3, cache_kidx_t, topk=topk_s, gs=gs)
        return _attn_sample(page_table, big3, k4, v4, sel, tiles_s, cache_k, cache_v, gs=gs)

    def ssd_sample(big3, bc3, small3, small3_t):
        return _ssd_sample(big3, bc3, small3, small3_t, state_conv[lyr], state_ssm[lyr], *ssm_params, gs=gs)

    ys, ks, vs, kis, cs, ss = _layer(x_sample, att_sample, ssd_sample, gs * t, p, kv_bf16=False)

    st = lambda a: a[None]
    return (yp, ys, st(kp), st(vp), st(kip), st(cp), st(sp), st(ks), st(vs), st(kis), st(cs), st(ss))
```

```python
import functools
import math

import numpy as np
import jax
import jax.numpy as jnp
from jax import lax
from jax.experimental import pallas as pl
from jax.experimental.pallas import tpu as pltpu

N_ATT_HEADS = 8
ATT_HEAD_DIM = 128
ATT_WIDTH = N_ATT_HEADS * ATT_HEAD_DIM
N_IDX_HEADS = 16
IDX_DIM = 64
TOPK_MAX = 256
N_SSM_HEADS = 16
SSM_HEAD_DIM = 64
SSM_WIDTH = N_SSM_HEADS * SSM_HEAD_DIM
N_SSM_GROUPS = 2
SSM_STATE = 128
CONV_W = 4
CONV_DIM = SSM_WIDTH + 2 * N_SSM_GROUPS * SSM_STATE
SSD_CHUNK = 128
PAGE_SIZE = 128
N_BUCKETS = 32
MAX_DISTANCE = 128
NORM_EPS = 1e-6

LANES = 128
SUBLANES = 8
VMEM_LIMIT = 56 * 1024 * 1024

ROW_TILE = 512
FFN_TILE = 512
CAUSAL_GROUP = 2
SELECT_SEQS_PER_STEP = 8

COL_Q, COL_K, COL_V, COL_QI, COL_Z, COL_X = (i * ATT_WIDTH for i in range(6))
BIG_WIDTH = 6 * ATT_WIDTH
GROUP_WIDTH = N_SSM_GROUPS * SSM_STATE
BC_WIDTH = 2 * GROUP_WIDTH
SM_KI, SM_WI, SM_DT = 0, IDX_DIM, IDX_DIM + N_IDX_HEADS
SMALL_WIDTH = LANES
INPROJ_TN = 2 * ATT_WIDTH
IN_KI = 3 * ATT_WIDTH + N_IDX_HEADS * IDX_DIM
IN_Z = IN_KI + IDX_DIM + N_IDX_HEADS
IN_B = IN_Z + 2 * SSM_WIDTH
IN_WIDTH = IN_B + BC_WIDTH + N_SSM_HEADS

F32 = jnp.float32
BF16 = jnp.bfloat16

_INT_MIN = -(2 ** 31)
_INT_MAX = 2 ** 31 - 1
_KEY_NEG_INF = int(np.array([-np.inf], np.float32).view(np.int32)[0]) ^ 0x7FFFFFFF
_NEG_BIG = -1e30


def _dot(a, b):
    return jnp.dot(a, b, preferred_element_type=F32)


def _dot_nt(a, b):
    return lax.dot_general(a, b, (((1,), (1,)), ((), ())), preferred_element_type=F32)


def _dot_tn(a, b):
    return lax.dot_general(a, b, (((0,), (0,)), ((), ())), preferred_element_type=F32)


def _rms(x, g):
    return x * lax.rsqrt(jnp.mean(x * x, axis=-1, keepdims=True) + NORM_EPS) * g


def _silu(x):
    return x * (1.0 / (1.0 + jnp.exp(-x)))


def _inproj_body(x_ref, g_ref, wb_ref, wsa_ref, wsb_ref, gk_ref, big_ref, bc_ref, small_ref, kin_ref, *rest,
                 kv_bf16):
    u_scr = rest[-1]
    n = pl.program_id(1)

    @pl.when(n == 0)
    def _():
        ub = _rms(x_ref[...], g_ref[...]).astype(BF16)
        u_scr[...] = ub
        nar_a = _dot_nt(ub, wsa_ref[...])
        nar_b = _dot_nt(ub, wsb_ref[...])
        bc_ref[...] = nar_b[:, :BC_WIDTH]
        pad = jnp.zeros((ub.shape[0], SMALL_WIDTH - SM_DT - N_SSM_HEADS), F32)
        small_ref[...] = jnp.concatenate([nar_a, nar_b[:, BC_WIDTH:], pad], axis=1)
        kin_ref[...] = _rms(nar_a[:, :IDX_DIM], gk_ref[...])

    tile = _dot_nt(u_scr[...], wb_ref[...])
    big_ref[...] = tile
    if kv_bf16:
        k16_ref, vt16_ref = rest[:2]

        @pl.when(n == COL_K // INPROJ_TN)
        def _():
            k16_ref[...] = tile[:, COL_K % INPROJ_TN:COL_K % INPROJ_TN + ATT_WIDTH].astype(BF16)

        @pl.when(n == COL_V // INPROJ_TN)
        def _():
            vt16_ref[0] = tile[:, COL_V % INPROJ_TN:COL_V % INPROJ_TN + ATT_WIDTH].T.astype(BF16)


def _inproj(x2, g, w_t, gk, *, tm, seq_len=None):
    m, d = x2.shape
    kv_bf16 = seq_len is not None
    assert IN_Z + INPROJ_TN == IN_B and IN_KI == 2 * INPROJ_TN
    assert (IN_Z - IN_KI) % (2 * SUBLANES) == 0 and IN_B % (2 * SUBLANES) == 0
    big_rows = lambda i, j: (pl.multiple_of(j * INPROJ_TN + (j // 2) * (IN_Z - IN_KI), 2 * SUBLANES), 0)
    out_specs = [
        pl.BlockSpec((tm, INPROJ_TN), lambda i, j: (i, j)),
        pl.BlockSpec((tm, BC_WIDTH), lambda i, j: (i, 0)),
        pl.BlockSpec((tm, SMALL_WIDTH), lambda i, j: (i, 0)),
        pl.BlockSpec((tm, IDX_DIM), lambda i, j: (i, 0)),
    ]
    out_shape = [
        jax.ShapeDtypeStruct((m, BIG_WIDTH), F32),
        jax.ShapeDtypeStruct((m, BC_WIDTH), F32),
        jax.ShapeDtypeStruct((m, SMALL_WIDTH), F32),
        jax.ShapeDtypeStruct((m, IDX_DIM), F32),
    ]
    if kv_bf16:
        assert seq_len % tm == 0
        per_seq = seq_len // tm
        out_specs += [
            pl.BlockSpec((tm, ATT_WIDTH), lambda i, j: (i, 0)),
            pl.BlockSpec((1, ATT_WIDTH, tm), lambda i, j: (i // per_seq, 0, i % per_seq)),
        ]
        out_shape += [
            jax.ShapeDtypeStruct((m, ATT_WIDTH), BF16),
            jax.ShapeDtypeStruct((m // seq_len, ATT_WIDTH, seq_len), BF16),
        ]
    return pl.pallas_call(
        functools.partial(_inproj_body, kv_bf16=kv_bf16),
        grid=(m // tm, BIG_WIDTH // INPROJ_TN),
        in_specs=[
            pl.BlockSpec((tm, d), lambda i, j: (i, 0)),
            pl.BlockSpec((1, d), lambda i, j: (0, 0)),
            pl.BlockSpec((pl.Element(INPROJ_TN), pl.Element(d)), big_rows),
            pl.BlockSpec((pl.Element(IN_Z - IN_KI), pl.Element(d)), lambda i, j: (IN_KI, 0),
                         pipeline_mode=pl.Buffered(1)),
            pl.BlockSpec((pl.Element(IN_WIDTH - IN_B), pl.Element(d)), lambda i, j: (IN_B, 0),
                         pipeline_mode=pl.Buffered(1)),
            pl.BlockSpec((1, IDX_DIM), lambda i, j: (0, 0)),
        ],
        out_specs=out_specs,
        out_shape=out_shape,
        scratch_shapes=[pltpu.VMEM((tm, d), BF16)],
        compiler_params=pltpu.CompilerParams(
            dimension_semantics=("parallel", "arbitrary"), vmem_limit_bytes=VMEM_LIMIT),
        name="inproj",
    )(x2, g, w_t, w_t, w_t, gk)


def _sortable_key(s):
    s = jnp.where(s == 0.0, 0.0, s)
    b = pltpu.bitcast(s, jnp.int32)
    return b ^ ((b >> 31) & 0x7FFFFFFF)


def _count_keys(key_ref, nblk, pred, key_axis):
    def body(kb, acc):
        return acc + jnp.where(pred(key_ref[kb], kb), 1.0, 0.0)

    acc = jnp.zeros(key_ref.shape[1:], F32)
    if isinstance(nblk, int):
        for kb in range(nblk):
            acc = body(kb, acc)
    else:
        acc = lax.fori_loop(0, nblk, body, acc)
    return jnp.sum(acc, axis=key_axis, keepdims=True)


_I16_MIN = -(2 ** 15)


def _pack_halves(key):
    r = key.shape[0] // 2
    a, b = key[:r], key[r:]
    hi = (a & jnp.int32(-65536)) | ((b >> 16) & 0xFFFF)
    lo = ((a ^ 0x8000) << 16) | ((b ^ 0x8000) & 0xFFFF)
    return hi, lo


def _count16(ref, nblk, cmp, cand):
    r = ref.shape[1]
    pair = (cand << 16) | (cand & 0xFFFF)
    cb = pltpu.bitcast(jnp.broadcast_to(pair, (r, LANES)), jnp.int16)
    acc = jnp.zeros((2 * r, LANES), jnp.int16)
    for kb in range(nblk):
        acc = acc + jnp.where(cmp(pltpu.bitcast(ref[kb], jnp.int16), cb), jnp.int16(1), jnp.int16(0))
    words = pltpu.bitcast(acc, jnp.int32)
    both = (words >> 16) + ((words << 16) >> 16)
    return jnp.sum(both.astype(F32), axis=0, keepdims=True)


def _kth_of_16bit(ref, nblk, need):
    ge = lambda k, c: k >= c
    res = jnp.where(_count16(ref, nblk, ge, jnp.zeros((1, LANES), jnp.int32)) >= need, 0, _I16_MIN).astype(jnp.int32)

    def body(it, res):
        cand = res | jnp.left_shift(jnp.int32(1), 14 - it)
        return jnp.where(_count16(ref, nblk, ge, cand) >= need, cand, res)

    return lax.fori_loop(0, 15, body, res)


def _kth_key_two_phase(hi_ref, lo_ref, nblk, topk):
    kf = float(topk)
    top = _kth_of_16bit(hi_ref, nblk, kf)
    n_above = _count16(hi_ref, nblk, lambda k, c: k > c, top)
    r = hi_ref.shape[1]
    top_b = pltpu.bitcast(jnp.broadcast_to((top << 16) | (top & 0xFFFF), (r, LANES)), jnp.int16)
    for kb in range(nblk):
        band = pltpu.bitcast(hi_ref[kb], jnp.int16) == top_b
        lo_ref[kb] = pltpu.bitcast(jnp.where(band, pltpu.bitcast(lo_ref[kb], jnp.int16), jnp.int16(_I16_MIN)),
                                   jnp.int32)
    low = _kth_of_16bit(lo_ref, nblk, kf - n_above)
    return (top << 16) | ((low ^ 0x8000) & 0xFFFF)


def _topk_select_params(key_ref, nblk, topk, key_axis, thr=None):
    shape = key_ref.shape[1:]
    qshape = tuple(1 if a == key_axis else n for a, n in enumerate(shape))
    blk = shape[key_axis]
    kf = float(topk)

    def count_ge(cand):
        cb = jnp.broadcast_to(cand, shape)
        return _count_keys(key_ref, nblk, lambda k, kb: k >= cb, key_axis)

    if thr is None:
        res = jnp.where(count_ge(jnp.zeros(qshape, jnp.int32)) >= kf, 0, _INT_MIN).astype(jnp.int32)

        def bit_body(it, res):
            cand = res | jnp.left_shift(jnp.int32(1), 30 - it)
            return jnp.where(count_ge(cand) >= kf, cand, res)

        thr = lax.fori_loop(0, 31, bit_body, res)
    thr_b = jnp.broadcast_to(thr, shape)
    n_gt = _count_keys(key_ref, nblk, lambda k, kb: k > thr_b, key_axis)
    n_ge = count_ge(thr)
    tied = jnp.logical_and(n_ge > kf, thr > _KEY_NEG_INF)
    any_tied = jnp.max(jnp.where(tied, 1.0, 0.0)) > 0.0
    need = kf - n_gt
    kpos = lax.broadcasted_iota(jnp.int32, shape, key_axis)

    def tie_break():
        def body(it, cut):
            cand = cut | jnp.left_shift(jnp.int32(1), 30 - it)
            cand_b = jnp.broadcast_to(cand, shape)
            n = _count_keys(key_ref, nblk,
                            lambda k, kb: jnp.logical_and(k == thr_b, kpos + kb * blk < cand_b), key_axis)
            return jnp.where(n <= need, cand, cut)

        return lax.fori_loop(0, 31, body, jnp.zeros(qshape, jnp.int32))

    cut = lax.cond(any_tied, tie_break, lambda: jnp.full(qshape, _INT_MAX, jnp.int32))
    return thr, cut


def _attn_prompt_body(q_ref, qi_ref, smt_ref, kin_ref, k_ref, vt_ref, bias_ref, o_ref,
                      key_scr, hi_scr, lo_scr, mask_scr, lg_scr, qi_scr, *, topk, group):
    i = pl.program_id(1)
    q0 = i * LANES
    blk = (LANES, LANES)
    scale = ATT_HEAD_DIM ** -0.5
    nq = key_scr.shape[0]

    w_t = smt_ref[0, SM_WI:SM_WI + N_IDX_HEADS, :]
    qi_all = qi_ref[0]
    for h in range(N_IDX_HEADS):
        qi_scr[h * LANES:(h + 1) * LANES, :] = qi_all[:, h * IDX_DIM:(h + 1) * IDX_DIM].astype(BF16)
    qi = qi_scr[...]
    kpos = lax.broadcasted_iota(jnp.int32, blk, 0)
    qpos = lax.broadcasted_iota(jnp.int32, blk, 1) + q0
    qb = q_ref[0].astype(BF16)

    def tile(nb, first_near):
        if nb * LANES <= topk:
            for kb in range(nb):
                mask_scr[kb * LANES:(kb + 1) * LANES, :] = jnp.where(kpos + kb * LANES <= qpos, 0.0, -jnp.inf)
        else:
            select_mask(nb, first_near)
        attend(nb, first_near)

    def select_mask(nb, first_near):
        for kb in range(nb):
            kin = kin_ref[0, kb * LANES:(kb + 1) * LANES, :].astype(BF16)
            s = _dot_nt(kin, qi)
            score = jnp.zeros(blk, F32)
            for h in range(N_IDX_HEADS):
                score = score + jnp.maximum(s[:, h * LANES:(h + 1) * LANES], 0.0) * w_t[h:h + 1, :]
            if kb >= first_near:
                score = jnp.where(kpos + kb * LANES <= qpos, score, -jnp.inf)
            key = _sortable_key(score)
            key_scr[kb] = key
            hi_scr[kb], lo_scr[kb] = _pack_halves(key)

        thr = _kth_key_two_phase(hi_scr, lo_scr, nb, topk)
        thr, cut = _topk_select_params(key_scr, nb, topk, key_axis=0, thr=thr)
        thr_b = jnp.broadcast_to(thr, blk)
        cut_b = jnp.broadcast_to(cut, blk)
        for kb in range(nb):
            kp = kpos + kb * LANES
            key = key_scr[kb]
            valid = jnp.logical_or(key > thr_b, jnp.logical_and(key == thr_b, kp < cut_b))
            if kb >= first_near:
                valid = jnp.logical_and(valid, kp <= qpos)
            mask_scr[kb * LANES:(kb + 1) * LANES, :] = jnp.where(valid, 0.0, -jnp.inf)

    def attend(nb, first_near):
        nk = nb * LANES
        for h in range(N_ATT_HEADS):
            hs = slice(h * ATT_HEAD_DIM, (h + 1) * ATT_HEAD_DIM)
            lg = _dot_nt(k_ref[0, 0:nk, hs], qb[:, hs]) * scale + mask_scr[0:nk, :]
            for kb in range(first_near, nb):
                tix = 2 - jnp.clip(i - kb, 0, 2)
                lg_scr[kb * LANES:(kb + 1) * LANES, :] = lg[kb * LANES:(kb + 1) * LANES, :] + bias_ref[tix, h]
            if first_near > 0:
                lg_scr[0:first_near * LANES, :] = lg[0:first_near * LANES, :]
            lg = lg_scr[0:nk, :]
            m = jnp.max(lg, axis=0, keepdims=True)
            p = jnp.exp(lg - m)
            l = jnp.sum(p, axis=0, keepdims=True)
            out_t = _dot(vt_ref[0, hs, 0:nk], p.astype(BF16)) / l
            o_ref[0, :, hs] = out_t.T.astype(o_ref.dtype)

    ngroups = -(-nq // group)
    for c in range(ngroups):
        nb = min((c + 1) * group, nq)

        @pl.when(i // group == c)
        def _(nb=nb, c=c):
            tile(nb, max(c * group - 1, 0))


def _attn_prompt(big3, small3_t, kin3, k16, vt16, bias_tiles_t, *, topk, group):
    b, s, _ = big3.shape
    nq = s // LANES
    body = functools.partial(_attn_prompt_body, topk=topk, group=group)
    return pl.pallas_call(
        body,
        grid=(b, nq),
        in_specs=[
            pl.BlockSpec((1, LANES, ATT_WIDTH), lambda bi, i: (bi, i, COL_Q // ATT_WIDTH)),
            pl.BlockSpec((1, LANES, N_IDX_HEADS * IDX_DIM), lambda bi, i: (bi, i, COL_QI // (N_IDX_HEADS * IDX_DIM))),
            pl.BlockSpec((1, SMALL_WIDTH, LANES), lambda bi, i: (bi * nq + i, 0, 0)),
            pl.BlockSpec((1, s, IDX_DIM), lambda bi, i: (bi, 0, 0)),
            pl.BlockSpec((1, s, ATT_WIDTH), lambda bi, i: (bi, 0, 0)),
            pl.BlockSpec((1, ATT_WIDTH, s), lambda bi, i: (bi, 0, 0)),
            pl.BlockSpec(bias_tiles_t.shape, lambda bi, i: (0, 0, 0, 0)),
        ],
        out_specs=pl.BlockSpec((1, LANES, ATT_WIDTH), lambda bi, i: (bi, i, 0)),
        out_shape=jax.ShapeDtypeStruct((b, s, ATT_WIDTH), BF16),
        scratch_shapes=[
            pltpu.VMEM((nq, LANES, LANES), jnp.int32),
            pltpu.VMEM((nq, LANES // 2, LANES), jnp.int32),
            pltpu.VMEM((nq, LANES // 2, LANES), jnp.int32),
            pltpu.VMEM((s, LANES), F32),
            pltpu.VMEM((s, LANES), F32),
            pltpu.VMEM((N_IDX_HEADS * LANES, IDX_DIM), BF16),
        ],
        compiler_params=pltpu.CompilerParams(
            dimension_semantics=("parallel", "arbitrary"), vmem_limit_bytes=VMEM_LIMIT),
        name="attn_prompt",
    )(big3, big3, small3_t, kin3, k16, vt16, bias_tiles_t)


def _select_sample_body(pt_ref, qi_ref, sm_ref, kinew_ref, *rest, n_pages, t, topk, ss):
    kidx_refs = rest[:ss * n_pages]
    sel_ref = rest[ss * n_pages]
    key_scr, wb_scr = rest[ss * n_pages + 1:]
    nblk = n_pages + 1
    j = pl.program_id(1)
    trow = lax.broadcasted_iota(jnp.int32, (t, LANES), 0)
    lane_t = lax.broadcasted_iota(jnp.int32, (t, LANES), 1)

    def scores(s):
        return jnp.sum(jnp.maximum(s.reshape(N_IDX_HEADS, t, LANES), 0.0) * wb_scr[...], axis=0)

    for u in range(ss):
        r0 = pl.multiple_of((j * ss + u) * t, t)
        wi = sm_ref[u, :, SM_WI:SM_WI + N_IDX_HEADS]
        for h in range(N_IDX_HEADS):
            wb_scr[h] = jnp.broadcast_to(wi[:, h:h + 1], (t, LANES))
        qi_all = qi_ref[u]
        qi = jnp.concatenate([qi_all[:, h * IDX_DIM:(h + 1) * IDX_DIM] for h in range(N_IDX_HEADS)],
                             axis=0).astype(BF16)
        for p in range(n_pages):
            kin_t = kidx_refs[u * n_pages + p][0, 0].astype(BF16)
            key_scr[p, pl.ds(r0, t), :] = _sortable_key(scores(_dot(qi, kin_t)))
        kin_new = jnp.concatenate([kinew_ref[u], jnp.zeros((LANES - t, IDX_DIM), F32)], axis=0)
        s_new = jnp.where(lane_t <= trow, scores(_dot_nt(qi, kin_new.astype(BF16))), -jnp.inf)
        key_scr[n_pages, pl.ds(r0, t), :] = _sortable_key(s_new)

    @pl.when(j == pl.num_programs(1) - 1)
    def _():
        rows = key_scr.shape[1]
        thr, cut = _topk_select_params(key_scr, nblk, topk, key_axis=1)
        thr_b = jnp.broadcast_to(thr, (rows, LANES))
        cut_b = jnp.broadcast_to(cut, (rows, LANES))
        lane = lax.broadcasted_iota(jnp.int32, (rows, LANES), 1)
        tq = lax.broadcasted_iota(jnp.int32, (rows, LANES), 0) & (t - 1)
        for p in range(nblk):
            key = key_scr[p]
            sel = jnp.logical_or(key > thr_b, jnp.logical_and(key == thr_b, lane + p * LANES < cut_b))
            if p == n_pages:
                sel = jnp.logical_and(sel, lane <= tq)
            sel_ref[0, p] = jnp.where(sel, 1.0, 0.0)


def _select_sample(page_table, big3, small3, kin3, cache_kidx_t, *, topk, gs):
    b, t, _ = small3.shape
    n_pages = page_table.shape[1]
    ss = math.gcd(gs, SELECT_SEQS_PER_STEP)
    steps = gs // ss
    assert b % gs == 0 and t & (t - 1) == 0
    body = functools.partial(_select_sample_body, n_pages=n_pages, t=t, topk=topk, ss=ss)
    blk = lambda g, j: g * steps + j
    qi_width = N_IDX_HEADS * IDX_DIM
    in_specs = [
        pl.BlockSpec((ss, t, qi_width), lambda g, j, pt: (blk(g, j), 0, COL_QI // qi_width)),
        pl.BlockSpec((ss, t, SMALL_WIDTH), lambda g, j, pt: (blk(g, j), 0, 0)),
        pl.BlockSpec((ss, t, IDX_DIM), lambda g, j, pt: (blk(g, j), 0, 0)),
    ]
    in_specs += [pl.BlockSpec((1, 1, IDX_DIM, PAGE_SIZE),
                              lambda g, j, pt, u=u, p=p: (0, pt[blk(g, j) * ss + u, p], 0, 0))
                 for u in range(ss) for p in range(n_pages)]
    sel_shape = (b // gs, n_pages + 1, gs * t, LANES)
    return pl.pallas_call(
        body,
        grid_spec=pltpu.PrefetchScalarGridSpec(
            num_scalar_prefetch=1,
            grid=(b // gs, steps),
            in_specs=in_specs,
            out_specs=pl.BlockSpec((1,) + sel_shape[1:], lambda g, j, pt: (g, 0, 0, 0)),
            scratch_shapes=[
                pltpu.VMEM(sel_shape[1:], jnp.int32),
                pltpu.VMEM((N_IDX_HEADS, t, LANES), F32),
            ],
        ),
        out_shape=jax.ShapeDtypeStruct(sel_shape, F32),
        compiler_params=pltpu.CompilerParams(
            dimension_semantics=("parallel", "arbitrary"), vmem_limit_bytes=VMEM_LIMIT),
        name="select_sample",
    )(page_table, big3, small3, kin3, *([cache_kidx_t] * (ss * n_pages)))


def _attn_sample_body(pt_ref, q_ref, knew_ref, vnew_ref, sel_ref, bias_ref, *rest, n_pages, t):
    k_refs = rest[:n_pages]
    v_refs = rest[n_pages:2 * n_pages]
    o_ref = rest[2 * n_pages]
    (lg_scr,) = rest[2 * n_pages + 1:]
    nblk = n_pages + 1
    rows = N_ATT_HEADS * t
    flat = PAGE_SIZE * N_ATT_HEADS
    scale = ATT_HEAD_DIM ** -0.5

    q = q_ref[0]
    q64 = jnp.concatenate([q[:, h * ATT_HEAD_DIM:(h + 1) * ATT_HEAD_DIM] for h in range(N_ATT_HEADS)],
                          axis=0).astype(BF16)
    er = lax.broadcasted_iota(jnp.int32, (LANES, flat), 0)
    ec = lax.broadcasted_iota(jnp.int32, (LANES, flat), 1)
    expand = jnp.where((ec >> (N_ATT_HEADS.bit_length() - 1)) == er, 1.0, 0.0).astype(BF16)

    def flat_kv(refs, p):
        if p < n_pages:
            return refs[p][0, 0].reshape(flat, ATT_HEAD_DIM).astype(BF16), flat
        new = refs[p][0].reshape(t * N_ATT_HEADS, ATT_HEAD_DIM)
        pad = jnp.zeros((LANES - t * N_ATT_HEADS, ATT_HEAD_DIM), F32)
        return jnp.concatenate([new, pad], axis=0).astype(BF16), LANES

    k_all = list(k_refs) + [knew_ref]
    v_all = list(v_refs) + [vnew_ref]
    m = jnp.full((rows, 1), _NEG_BIG, F32)
    for p in range(nblk):
        kflat, width = flat_kv(k_all, p)
        selx = _dot(sel_ref[0, p].astype(BF16), expand[:, :width])
        unsel = (selx - 1.0) * (-_NEG_BIG)
        tix = 2 - min(n_pages - p, 2)
        lg = _dot_nt(q64, kflat) * scale + bias_ref[tix, :, :width] + jnp.concatenate([unsel] * N_ATT_HEADS, axis=0)
        lg_scr[p, :, :width] = lg
        m = jnp.maximum(m, jnp.max(lg, axis=-1, keepdims=True))

    l = jnp.zeros((rows, 1), F32)
    acc = jnp.zeros((rows, ATT_HEAD_DIM), F32)
    for p in range(nblk):
        vflat, width = flat_kv(v_all, p)
        pr = jnp.exp(lg_scr[p, :, :width] - m)
        l = l + jnp.sum(pr, axis=-1, keepdims=True)
        acc = acc + _dot(pr.astype(BF16), vflat)
    out = acc / l
    for h in range(N_ATT_HEADS):
        o_ref[0, :, h * ATT_HEAD_DIM:(h + 1) * ATT_HEAD_DIM] = out[h * t:(h + 1) * t, :].astype(o_ref.dtype)


def _attn_sample(page_table, big3, k_new, v_new, sel, bias_rows, cache_k, cache_v, *, gs):
    b, t, _ = big3.shape
    n_pages = page_table.shape[1]
    assert t == SUBLANES and N_ATT_HEADS & (N_ATT_HEADS - 1) == 0 and t * N_ATT_HEADS <= LANES
    body = functools.partial(_attn_sample_body, n_pages=n_pages, t=t)
    kv_block = (1, 1, PAGE_SIZE, N_ATT_HEADS, ATT_HEAD_DIM)
    new_block = (1, t, N_ATT_HEADS, ATT_HEAD_DIM)

    in_specs = [
        pl.BlockSpec((1, t, ATT_WIDTH), lambda bi, pt: (bi, 0, COL_Q // ATT_WIDTH)),
        pl.BlockSpec(new_block, lambda bi, pt: (bi, 0, 0, 0)),
        pl.BlockSpec(new_block, lambda bi, pt: (bi, 0, 0, 0)),
        pl.BlockSpec((1, n_pages + 1, t, LANES), lambda bi, pt: (bi // gs, 0, bi % gs, 0)),
        pl.BlockSpec(bias_rows.shape, lambda bi, pt: (0, 0, 0)),
    ]
    in_specs += [pl.BlockSpec(kv_block, lambda bi, pt, p=p: (0, pt[bi, p], 0, 0, 0)) for p in range(n_pages)] * 2
    rows = N_ATT_HEADS * t
    return pl.pallas_call(
        body,
        grid_spec=pltpu.PrefetchScalarGridSpec(
            num_scalar_prefetch=1,
            grid=(b,),
            in_specs=in_specs,
            out_specs=pl.BlockSpec((1, t, ATT_WIDTH), lambda bi, pt: (bi, 0, 0)),
            scratch_shapes=[pltpu.VMEM((n_pages + 1, rows, PAGE_SIZE * N_ATT_HEADS), F32)],
        ),
        out_shape=jax.ShapeDtypeStruct((b, t, ATT_WIDTH), BF16),
        compiler_params=pltpu.CompilerParams(
            dimension_semantics=("arbitrary",), vmem_limit_bytes=VMEM_LIMIT),
        name="attn_sample",
    )(page_table, big3, k_new, v_new, sel, bias_rows, *([cache_k] * n_pages), *([cache_v] * n_pages))


def _split3(a):
    a1 = a.astype(BF16)
    r1 = a - a1.astype(F32)
    a2 = r1.astype(BF16)
    a3 = (r1 - a2.astype(F32)).astype(BF16)
    return a1, a2, a3


def _softplus(v):
    return jnp.maximum(v, 0.0) + jnp.log1p(jnp.exp(-jnp.abs(v)))


def _head_expander():
    r = lax.broadcasted_iota(jnp.int32, (SMALL_WIDTH, SSM_WIDTH), 0)
    c = lax.broadcasted_iota(jnp.int32, (SMALL_WIDTH, SSM_WIDTH), 1)
    return jnp.where(r - SM_DT == c // SSM_HEAD_DIM, 1.0, 0.0).astype(BF16)


def _ssd_body(x_ref, bm_ref, cm_ref, z_ref, sm_ref, smt_ref, pre_ref, h0_ref,
              cw_ref, cb_ref, dtb_ref, dtbt_ref, alog_ref, alogt_ref, dskipx_ref, gn_ref,
              y_ref, tail_ref, hout_ref, xp_scr, ht_scr, y_scr, *, l):
    c = pl.program_id(1)
    nc = pl.num_programs(1)
    off = SUBLANES
    heads_per_group = N_SSM_HEADS // N_SSM_GROUPS
    gcols = heads_per_group * SSM_HEAD_DIM

    @pl.when(c == 0)
    def _():
        xp_scr[off - (CONV_W - 1):off, :] = pre_ref[0]
        for h in range(N_SSM_HEADS):
            g, hh = divmod(h, heads_per_group)
            ht_scr[g, :, hh * SSM_HEAD_DIM:(hh + 1) * SSM_HEAD_DIM] = h0_ref[0, h].T

    xp_scr[off:off + l, 0:SSM_WIDTH] = x_ref[0]
    xp_scr[off:off + l, SSM_WIDTH:SSM_WIDTH + GROUP_WIDTH] = bm_ref[0]
    xp_scr[off:off + l, SSM_WIDTH + GROUP_WIDTH:CONV_DIM] = cm_ref[0]

    conv = cb_ref[...]
    for j in range(CONV_W):
        s0 = off - (CONV_W - 1) + j
        conv = conv + xp_scr[s0:s0 + l, :] * cw_ref[j:j + 1, :]
    xc = _silu(conv)
    tail = xp_scr[off + l - (CONV_W - 1):off + l, :]
    tail_ref[0] = tail
    xp_scr[off - (CONV_W - 1):off, :] = tail

    xs = xc[:, :SSM_WIDTH]
    bc = [xc[:, SSM_WIDTH + g * SSM_STATE:SSM_WIDTH + (g + 1) * SSM_STATE].astype(BF16)
          for g in range(N_SSM_GROUPS)]
    cc = [xc[:, SSM_WIDTH + GROUP_WIDTH + g * SSM_STATE:SSM_WIDTH + GROUP_WIDTH + (g + 1) * SSM_STATE]
          .astype(BF16) for g in range(N_SSM_GROUPS)]

    dt = _softplus(sm_ref[0] + dtb_ref[...])
    a = dt * (-jnp.exp(alog_ref[...]))
    dtt = _softplus(smt_ref[0, SM_DT:SM_DT + N_SSM_HEADS, :] + dtbt_ref[...])
    at = dtt * (-jnp.exp(alogt_ref[...]))

    ri = lax.broadcasted_iota(jnp.int32, (l, l), 0)
    ci = lax.broadcasted_iota(jnp.int32, (l, l), 1)
    causal = ri >= ci
    tri = jnp.where(causal, 1.0, 0.0).astype(BF16)
    tri_u = jnp.where(ri <= ci, 1.0, 0.0).astype(BF16)
    acum = sum(_dot(tri, piece) for piece in _split3(a))
    acum_t = sum(_dot(piece, tri_u) for piece in _split3(at))
    last = acum[l - 1:l, :]
    wts = jnp.exp(last - acum) * dt
    expand = _head_expander()
    acum_x = sum(_dot(piece, expand) for piece in _split3(acum))
    wts_x = sum(_dot(piece, expand) for piece in _split3(wts))
    exp_last_x = jnp.exp(acum_x[l - 1:l, :])

    cb = [_dot_nt(cc[g], bc[g]) for g in range(N_SSM_GROUPS)]
    for h in range(N_SSM_HEADS):
        g = h // heads_per_group
        hs = slice(h * SSM_HEAD_DIM, (h + 1) * SSM_HEAD_DIM)
        col = acum[:, SM_DT + h:SM_DT + h + 1]
        decay = jnp.exp(jnp.where(causal, col - acum_t[h:h + 1, :], -jnp.inf))
        mmat = cb[g] * decay * dtt[h:h + 1, :]
        y_scr[:, hs] = _dot(mmat.astype(BF16), xs[:, hs].astype(BF16))

    xw = (xs * wts_x).astype(BF16)
    y_inter = []
    for g in range(N_SSM_GROUPS):
        gsl = slice(g * gcols, (g + 1) * gcols)
        ht = ht_scr[g]
        y_inter.append(_dot(cc[g], ht.astype(BF16)))
        ht_scr[g] = ht * exp_last_x[:, gsl] + _dot_tn(bc[g], xw[:, gsl])
    y = y_scr[...] + jnp.concatenate(y_inter, axis=1) * jnp.exp(acum_x) + dskipx_ref[...] * xs
    y_ref[0] = _rms(y * _silu(z_ref[0]), gn_ref[...]).astype(y_ref.dtype)

    @pl.when(c == nc - 1)
    def _():
        for h in range(N_SSM_HEADS):
            g, hh = divmod(h, heads_per_group)
            hout_ref[0, h] = ht_scr[g, :, hh * SSM_HEAD_DIM:(hh + 1) * SSM_HEAD_DIM].T


def _pad_lanes(v):
    return jnp.zeros((1, SMALL_WIDTH), F32).at[0, SM_DT:SM_DT + N_SSM_HEADS].set(v)


def _ssd(big3, bc3, small3, small3_t, prefix, h0, conv_w, conv_b, dt_bias, a_log, d_skip, ssm_norm, *, l):
    b, t, _ = big3.shape
    nc = t // l
    body = functools.partial(_ssd_body, l=l)
    row = lambda v: v.reshape(1, -1)
    colv = lambda v: v.reshape(-1, 1)
    const2 = lambda shape: pl.BlockSpec(shape, lambda bi, c: (0, 0))
    return pl.pallas_call(
        body,
        grid=(b, nc),
        in_specs=[
            pl.BlockSpec((1, l, SSM_WIDTH), lambda bi, c: (bi, c, COL_X // SSM_WIDTH)),
            pl.BlockSpec((1, l, GROUP_WIDTH), lambda bi, c: (bi, c, 0)),
            pl.BlockSpec((1, l, GROUP_WIDTH), lambda bi, c: (bi, c, 1)),
            pl.BlockSpec((1, l, SSM_WIDTH), lambda bi, c: (bi, c, COL_Z // SSM_WIDTH)),
            pl.BlockSpec((1, l, SMALL_WIDTH), lambda bi, c: (bi, c, 0)),
            pl.BlockSpec((1, SMALL_WIDTH, l), lambda bi, c: (bi * nc + c, 0, 0)),
            pl.BlockSpec((1, CONV_W - 1, CONV_DIM), lambda bi, c: (bi, 0, 0)),
            pl.BlockSpec((1, N_SSM_HEADS, SSM_HEAD_DIM, SSM_STATE), lambda bi, c: (bi, 0, 0, 0)),
            const2((CONV_W, CONV_DIM)),
            const2((1, CONV_DIM)),
            const2((1, SMALL_WIDTH)),
            const2((N_SSM_HEADS, 1)),
            const2((1, SMALL_WIDTH)),
            const2((N_SSM_HEADS, 1)),
            const2((1, SSM_WIDTH)),
            const2((1, SSM_WIDTH)),
        ],
        out_specs=[
            pl.BlockSpec((1, l, SSM_WIDTH), lambda bi, c: (bi, c, 0)),
            pl.BlockSpec((1, CONV_W - 1, CONV_DIM), lambda bi, c: (bi, 0, 0)),
            pl.BlockSpec((1, N_SSM_HEADS, SSM_HEAD_DIM, SSM_STATE), lambda bi, c: (bi, 0, 0, 0)),
        ],
        out_shape=[
            jax.ShapeDtypeStruct((b, t, SSM_WIDTH), BF16),
            jax.ShapeDtypeStruct((b, CONV_W - 1, CONV_DIM), F32),
            jax.ShapeDtypeStruct((b, N_SSM_HEADS, SSM_HEAD_DIM, SSM_STATE), F32),
        ],
        scratch_shapes=[
            pltpu.VMEM((SUBLANES + l, CONV_DIM), F32),
            pltpu.VMEM((N_SSM_GROUPS, SSM_STATE, SSM_WIDTH // N_SSM_GROUPS), F32),
            pltpu.VMEM((l, SSM_WIDTH), F32),
        ],
        compiler_params=pltpu.CompilerParams(
            dimension_semantics=("parallel", "arbitrary"), vmem_limit_bytes=VMEM_LIMIT),
        name="ssd",
    )(big3, bc3, bc3, big3, small3, small3_t, prefix, h0,
      conv_w, row(conv_b), _pad_lanes(dt_bias), colv(dt_bias), _pad_lanes(a_log), colv(a_log),
      row(jnp.repeat(d_skip, SSM_HEAD_DIM)), row(ssm_norm))


def _ssd_sample_body(x_ref, bm_ref, cm_ref, z_ref, sm_ref, smt_ref, pre_ref, h0_ref,
                     cw_ref, cb_ref, dtb_ref, dtbt_ref, alog_ref, alogt_ref, dskip_ref, gn_ref,
                     y_ref, tail_ref, hout_ref, xp_scr, xc_scr, yi_scr, xw_scr, y_scr, *, gs, t):
    rows = gs * t
    off = SUBLANES
    tshift = t.bit_length() - 1
    gw = N_SSM_GROUPS * SSM_STATE
    heads_per_group = N_SSM_HEADS // N_SSM_GROUPS
    gcols = heads_per_group * SSM_HEAD_DIM

    for s in range(gs):
        xp_scr[s, off - (CONV_W - 1):off, :] = pre_ref[s]
        xp_scr[s, off:off + t, 0:SSM_WIDTH] = x_ref[s]
        xp_scr[s, off:off + t, SSM_WIDTH:SSM_WIDTH + gw] = bm_ref[s]
        xp_scr[s, off:off + t, SSM_WIDTH + gw:CONV_DIM] = cm_ref[s]
        conv = cb_ref[...]
        for j in range(CONV_W):
            s0 = off - (CONV_W - 1) + j
            conv = conv + xp_scr[s, s0:s0 + t, :] * cw_ref[j:j + 1, :]
        xc_scr[s * t:(s + 1) * t, :] = _silu(conv)
        tail_ref[s] = xp_scr[s, off + t - (CONV_W - 1):off + t, :]

    xc = xc_scr[...]
    xs = xc[:, :SSM_WIDTH]
    bcf = [xc[:, SSM_WIDTH + g * SSM_STATE:SSM_WIDTH + (g + 1) * SSM_STATE] for g in range(N_SSM_GROUPS)]
    ccf = [xc[:, SSM_WIDTH + gw + g * SSM_STATE:SSM_WIDTH + gw + (g + 1) * SSM_STATE] for g in range(N_SSM_GROUPS)]

    dt = _softplus(sm_ref[...].reshape(rows, SMALL_WIDTH)[:, SM_DT:SM_DT + N_SSM_HEADS] + dtb_ref[...])
    dtt = _softplus(smt_ref[0, SM_DT:SM_DT + N_SSM_HEADS, :] + dtbt_ref[...])
    a = dt * (-jnp.exp(alog_ref[...]))
    at = dtt * (-jnp.exp(alogt_ref[...]))

    ri = lax.broadcasted_iota(jnp.int32, (rows, rows), 0)
    ci = lax.broadcasted_iota(jnp.int32, (rows, rows), 1)
    same = (ri >> tshift) == (ci >> tshift)
    causal = jnp.logical_and(same, ri >= ci)
    tri = jnp.where(causal, 1.0, 0.0).astype(BF16)
    tri_u = jnp.where(jnp.logical_and(same, ri <= ci), 1.0, 0.0).astype(BF16)
    pick_last = jnp.where(ci == (((ri >> tshift) << tshift) + (t - 1)), 1.0, 0.0).astype(BF16)
    acum = sum(_dot(tri, piece) for piece in _split3(a))
    acum_t = sum(_dot(piece, tri_u) for piece in _split3(at))
    last = sum(_dot(pick_last, piece) for piece in _split3(acum))
    wts = jnp.exp(last - acum) * dt
    exp_last = jnp.exp(last)

    for s in range(gs):
        rs = slice(s * t, (s + 1) * t)
        for g in range(N_SSM_GROUPS):
            h0g = h0_ref[s, g * heads_per_group:(g + 1) * heads_per_group].reshape(gcols, SSM_STATE)
            yi_scr[rs, g * gcols:(g + 1) * gcols] = _dot_nt(ccf[g][rs, :], h0g)

    cb = [_dot_nt(ccf[g].astype(BF16), bcf[g].astype(BF16)) for g in range(N_SSM_GROUPS)]
    for h in range(N_SSM_HEADS):
        g = h // heads_per_group
        hs = slice(h * SSM_HEAD_DIM, (h + 1) * SSM_HEAD_DIM)
        col = acum[:, h:h + 1]
        decay = jnp.exp(jnp.where(causal, col - acum_t[h:h + 1, :], -jnp.inf))
        mmat = cb[g] * decay * dtt[h:h + 1, :]
        xh = xs[:, hs]
        y_intra = _dot(mmat.astype(BF16), xh.astype(BF16))
        y_scr[:, hs] = y_intra + yi_scr[:, hs] * jnp.exp(col) + dskip_ref[h] * xh
        xw_scr[:, hs] = xh * wts[:, h:h + 1]

    out = _rms(y_scr[...] * _silu(z_ref[...].reshape(rows, SSM_WIDTH)), gn_ref[...])
    for s in range(gs):
        y_ref[s] = out[s * t:(s + 1) * t, :].astype(y_ref.dtype)

    for s in range(gs):
        rs = slice(s * t, (s + 1) * t)
        for g in range(N_SSM_GROUPS):
            upd = _dot_tn(xw_scr[rs, g * gcols:(g + 1) * gcols], bcf[g][rs, :])
            for hh in range(heads_per_group):
                h = g * heads_per_group + hh
                scale = jnp.broadcast_to(exp_last[s * t:s * t + 1, h:h + 1], (SSM_HEAD_DIM, SSM_STATE))
                hout_ref[s, h] = h0_ref[s, h] * scale + upd[hh * SSM_HEAD_DIM:(hh + 1) * SSM_HEAD_DIM, :]


def _ssd_sample(big3, bc3, small3, small3_t, prefix, h0, conv_w, conv_b, dt_bias, a_log, d_skip, ssm_norm, *, gs):
    b, t, _ = big3.shape
    assert b % gs == 0 and t & (t - 1) == 0 and t == SUBLANES
    rows = gs * t
    body = functools.partial(_ssd_sample_body, gs=gs, t=t)
    gw = N_SSM_GROUPS * SSM_STATE
    row = lambda v: v.reshape(1, -1)
    colv = lambda v: v.reshape(-1, 1)
    const2 = lambda shape: pl.BlockSpec(shape, lambda gi: (0, 0))
    state_block = (gs, N_SSM_HEADS, SSM_HEAD_DIM, SSM_STATE)
    return pl.pallas_call(
        body,
        grid=(b // gs,),
        in_specs=[
            pl.BlockSpec((gs, t, SSM_WIDTH), lambda gi: (gi, 0, COL_X // SSM_WIDTH)),
            pl.BlockSpec((gs, t, gw), lambda gi: (gi, 0, 0)),
            pl.BlockSpec((gs, t, gw), lambda gi: (gi, 0, 1)),
            pl.BlockSpec((gs, t, SSM_WIDTH), lambda gi: (gi, 0, COL_Z // SSM_WIDTH)),
            pl.BlockSpec((gs, t, SMALL_WIDTH), lambda gi: (gi, 0, 0)),
            pl.BlockSpec((1, SMALL_WIDTH, rows), lambda gi: (gi, 0, 0)),
            pl.BlockSpec((gs, CONV_W - 1, CONV_DIM), lambda gi: (gi, 0, 0)),
            pl.BlockSpec(state_block, lambda gi: (gi, 0, 0, 0)),
            const2((CONV_W, CONV_DIM)),
            const2((1, CONV_DIM)),
            const2((1, N_SSM_HEADS)),
            const2((N_SSM_HEADS, 1)),
            const2((1, N_SSM_HEADS)),
            const2((N_SSM_HEADS, 1)),
            pl.BlockSpec(memory_space=pltpu.SMEM),
            const2((1, SSM_WIDTH)),
        ],
        out_specs=[
            pl.BlockSpec((gs, t, SSM_WIDTH), lambda gi: (gi, 0, 0)),
            pl.BlockSpec((gs, CONV_W - 1, CONV_DIM), lambda gi: (gi, 0, 0)),
            pl.BlockSpec(state_block, lambda gi: (gi, 0, 0, 0)),
        ],
        out_shape=[
            jax.ShapeDtypeStruct((b, t, SSM_WIDTH), BF16),
            jax.ShapeDtypeStruct((b, CONV_W - 1, CONV_DIM), F32),
            jax.ShapeDtypeStruct((b, N_SSM_HEADS, SSM_HEAD_DIM, SSM_STATE), F32),
        ],
        scratch_shapes=[
            pltpu.VMEM((gs, SUBLANES + t, CONV_DIM), F32),
            pltpu.VMEM((rows, CONV_DIM), F32),
            pltpu.VMEM((rows, SSM_WIDTH), F32),
            pltpu.VMEM((rows, SSM_WIDTH), F32),
            pltpu.VMEM((rows, SSM_WIDTH), F32),
        ],
        compiler_params=pltpu.CompilerParams(
            dimension_semantics=("parallel",), vmem_limit_bytes=VMEM_LIMIT),
        name="ssd_sample",
    )(big3, bc3, bc3, big3, small3, small3_t, prefix, h0,
      conv_w, row(conv_b), row(dt_bias), colv(dt_bias), row(a_log), colv(a_log), d_skip, row(ssm_norm))


def _ffn_body(x_ref, att_ref, y_ref, woa_ref, woy_ref, gf_ref, wg_ref, wu_ref, wd_ref, gl_ref,
              o_ref, f_scr):
    j = pl.program_id(1)

    @pl.when(j == 0)
    def _():
        h = x_ref[...] + _dot(att_ref[...], woa_ref[...]) + _dot(y_ref[...], woy_ref[...])
        o_ref[...] = h
        f_scr[...] = _rms(h, gf_ref[...]).astype(BF16)

    f = f_scr[...]
    act = _silu(_dot(f, wg_ref[...])) * _dot(f, wu_ref[...])
    o_ref[...] += _dot(act.astype(BF16), wd_ref[...])

    @pl.when(j == pl.num_programs(1) - 1)
    def _():
        o_ref[...] = _rms(o_ref[...], gl_ref[...])


def _out_ffn(x2, att2, y2, woa, woy, gf, wg, wu, wd, gl, *, tm, th):
    m, d = x2.shape
    hidden = wg.shape[1]
    resident = dict(pipeline_mode=pl.Buffered(1))
    return pl.pallas_call(
        _ffn_body,
        grid=(m // tm, hidden // th),
        in_specs=[
            pl.BlockSpec((tm, d), lambda i, j: (i, 0)),
            pl.BlockSpec((tm, ATT_WIDTH), lambda i, j: (i, 0)),
            pl.BlockSpec((tm, SSM_WIDTH), lambda i, j: (i, 0)),
            pl.BlockSpec((ATT_WIDTH, d), lambda i, j: (0, 0), **resident),
            pl.BlockSpec((SSM_WIDTH, d), lambda i, j: (0, 0), **resident),
            pl.BlockSpec((1, d), lambda i, j: (0, 0)),
            pl.BlockSpec((d, th), lambda i, j: (0, j)),
            pl.BlockSpec((d, th), lambda i, j: (0, j)),
            pl.BlockSpec((th, d), lambda i, j: (j, 0)),
            pl.BlockSpec((1, d), lambda i, j: (0, 0)),
        ],
        out_specs=pl.BlockSpec((tm, d), lambda i, j: (i, 0)),
        out_shape=jax.ShapeDtypeStruct((m, d), F32),
        scratch_shapes=[pltpu.VMEM((tm, d), BF16)],
        compiler_params=pltpu.CompilerParams(
            dimension_semantics=("parallel", "arbitrary"), vmem_limit_bytes=VMEM_LIMIT),
        name="out_ffn",
    )(x2, att2, y2, woa, woy, gf, wg, wu, wd, gl)


def _t5_bucket(rel):
    max_exact = N_BUCKETS // 2
    relf = jnp.maximum(rel, 1).astype(jnp.float32)
    large = max_exact + (jnp.log(relf / max_exact) / math.log(MAX_DISTANCE / max_exact)
                         * (N_BUCKETS - max_exact)).astype(jnp.int32)
    large = jnp.minimum(large, N_BUCKETS - 1)
    return jnp.where(rel < max_exact, rel, large)


def _toeplitz(u, rows, cols):
    nh, period = u.shape
    assert cols < period
    flat = jnp.tile(u, (1, rows))
    return flat[:, :rows * (period - 1)].reshape(nh, rows, period - 1)[:, :, :cols]


def _bias_tiles(rel_bias, rows, key_major):
    assert MAX_DISTANCE <= LANES and rows <= LANES
    table = rel_bias[_t5_bucket(jnp.arange(3 * LANES, dtype=jnp.int32))].T
    tiles = []
    for dd in (2, 1, 0):
        neg = table[:, (dd - 1) * LANES:dd * LANES] if dd >= 1 else jnp.tile(table[:, 0:1], (1, LANES))
        u = jnp.concatenate([table[:, dd * LANES:(dd + 1) * LANES], neg], axis=1)
        if key_major:
            tiles.append(_toeplitz(u, LANES, rows))
        else:
            w = jnp.concatenate([u[:, 0:1], jnp.flip(u[:, 1:], axis=1)], axis=1)
            tiles.append(_toeplitz(w, rows, LANES))
    return jnp.stack(tiles)


def _layer(x, att_call, ssd_call, blk_rows, p, *, kv_bf16):
    b, t, d = x.shape
    m = b * t
    tm = min(ROW_TILE, m)
    proj = _inproj(x.reshape(m, d), p["norm_attn"], p["w_t"], p["idx_k_norm"], tm=tm,
                   seq_len=t if kv_bf16 else None)
    big, bc, small, kin = proj[:4]
    big3 = big.reshape(b, t, BIG_WIDTH)
    bc3 = bc.reshape(b, t, BC_WIDTH)
    small3 = small.reshape(b, t, SMALL_WIDTH)
    kin3 = kin.reshape(b, t, IDX_DIM)
    hd = (b, t, N_ATT_HEADS, ATT_HEAD_DIM)
    k4 = big3[:, :, COL_K:COL_K + ATT_WIDTH].reshape(hd)
    v4 = big3[:, :, COL_V:COL_V + ATT_WIDTH].reshape(hd)
    small3_t = jnp.transpose(small.reshape(m // blk_rows, blk_rows, SMALL_WIDTH), (0, 2, 1))
    att = att_call(big3, small3, small3_t, kin3, k4, v4, *proj[4:])
    y, tail, h_fin = ssd_call(big3, bc3, small3, small3_t)
    out = _out_ffn(x.reshape(m, d), att.reshape(m, ATT_WIDTH), y.reshape(m, SSM_WIDTH), p["woa"], p["woy"],
                   p["norm_ffn"], p["wg"], p["wu"], p["wd"], p["norm_final"], tm=tm, th=FFN_TILE)
    return out.reshape(b, t, d), k4, v4, kin3, tail, h_fin


def kernel(x_prompt, x_sample, cache_k, cache_v, cache_kidx, state_conv, state_ssm, page_table, rel_bias,
           norm_attn, w_in, idx_k_norm, conv_w, conv_b, dt_bias, a_log, d_skip, ssm_norm, w_out, norm_ffn,
           w_gate, w_up, w_down, norm_final):
    depth = w_in.shape[0]
    assert depth == 1
    bp, s, d = x_prompt.shape
    bs, t, _ = x_sample.shape
    n_pages = page_table.shape[1]
    past = n_pages * PAGE_SIZE
    lyr = 0

    assert w_in.shape[2] == IN_WIDTH
    w_t = jnp.swapaxes(w_in[lyr], 0, 1).astype(BF16)
    row = lambda v: v.reshape(1, -1)
    p = dict(
        norm_attn=row(norm_attn[lyr]), w_t=w_t, idx_k_norm=row(idx_k_norm[lyr]),
        conv_w=conv_w[lyr], conv_b=conv_b[lyr], dt_bias=dt_bias[lyr], a_log=a_log[lyr], d_skip=d_skip[lyr],
        ssm_norm=ssm_norm[lyr],
        woa=w_out[lyr, :ATT_WIDTH].astype(BF16), woy=w_out[lyr, ATT_WIDTH:].astype(BF16),
        norm_ffn=row(norm_ffn[lyr]), wg=w_gate[lyr].astype(BF16), wu=w_up[lyr].astype(BF16),
        wd=w_down[lyr].astype(BF16), norm_final=row(norm_final),
    )

    topk_p = min(TOPK_MAX, s // 4)
    assert SSD_CHUNK == LANES and s % LANES == 0
    tiles_p = _bias_tiles(rel_bias, LANES, key_major=True)
    tiles_p = tiles_p - tiles_p[0:1]

    def att_prompt(big3, small3, small3_t, kin3, k4, v4, k16, vt16):
        return _attn_prompt(big3, small3_t, kin3, k16.reshape(bp, s, ATT_WIDTH), vt16, tiles_p,
                            topk=topk_p, group=CAUSAL_GROUP)

    zero_conv = jnp.zeros((bp, CONV_W - 1, CONV_DIM), F32)
    zero_ssm = jnp.zeros((bp, N_SSM_HEADS, SSM_HEAD_DIM, SSM_STATE), F32)
    ssm_params = (p["conv_w"], p["conv_b"], p["dt_bias"], p["a_log"], p["d_skip"], p["ssm_norm"])

    def ssd_prompt(big3, bc3, small3, small3_t):
        return _ssd(big3, bc3, small3, small3_t, zero_conv, zero_ssm, *ssm_params, l=SSD_CHUNK)

    yp, kp, vp, kip, cp, sp = _layer(x_prompt, att_prompt, ssd_prompt, SSD_CHUNK, p, kv_bf16=True)

    topk_s = min(TOPK_MAX, (past + t) // 4)
    tiles_s = jnp.repeat(_bias_tiles(rel_bias, t, key_major=False).reshape(3, N_ATT_HEADS * t, LANES),
                         N_ATT_HEADS, axis=-1)
    lane_head = jnp.arange(LANES * N_ATT_HEADS, dtype=jnp.int32) % N_ATT_HEADS
    row_head = jnp.arange(N_ATT_HEADS * t, dtype=jnp.int32) // t
    tiles_s = jnp.where(lane_head[None, :] == row_head[:, None], tiles_s, -jnp.inf)

    cache_kidx_t = jnp.swapaxes(cache_kidx, -1, -2)

    gs = math.gcd(bs, LANES // t)

    def att_sample(big3, small3, small3_t, kin3, k4, v4):
        sel = _select_sample(page_table, big3, small3, kin3, cache_kidx_t, topk=topk_s, gs=gs)
        return _attn_sample(page_table, big3, k4, v4, sel, tiles_s, cache_k, cache_v, gs=gs)

    def ssd_sample(big3, bc3, small3, small3_t):
        return _ssd_sample(big3, bc3, small3, small3_t, state_conv[lyr], state_ssm[lyr], *ssm_params, gs=gs)

    ys, ks, vs, kis, cs, ss = _layer(x_sample, att_sample, ssd_sample, gs * t, p, kv_bf16=False)

    st = lambda a: a[None]
    return (yp, ys, st(kp), st(vp), st(kip), st(cp), st(sp), st(ks), st(vs), st(kis), st(cs), st(ss))
```

```python
import functools
import math

import numpy as np
import jax
import jax.numpy as jnp
from jax import lax
from jax.experimental import pallas as pl
from jax.experimental.pallas import tpu as pltpu

N_ATT_HEADS = 8
ATT_HEAD_DIM = 128
ATT_WIDTH = N_ATT_HEADS * ATT_HEAD_DIM
N_IDX_HEADS = 16
IDX_DIM = 64
TOPK_MAX = 256
N_SSM_HEADS = 16
SSM_HEAD_DIM = 64
SSM_WIDTH = N_SSM_HEADS * SSM_HEAD_DIM
N_SSM_GROUPS = 2
SSM_STATE = 128
CONV_W = 4
CONV_DIM = SSM_WIDTH + 2 * N_SSM_GROUPS * SSM_STATE
SSD_CHUNK = 128
PAGE_SIZE = 128
N_BUCKETS = 32
MAX_DISTANCE = 128
NORM_EPS = 1e-6

LANES = 128
SUBLANES = 8
VMEM_LIMIT = 56 * 1024 * 1024
INPROJ_VMEM_LIMIT = 60000 * 1024

ROW_TILE = 512
FFN_TILE = 512
CAUSAL_GROUP = 2
SELECT_SEQS_PER_STEP = 8

COL_Q, COL_K, COL_V, COL_QI, COL_Z, COL_X = (i * ATT_WIDTH for i in range(6))
BIG_WIDTH = 6 * ATT_WIDTH
GROUP_WIDTH = N_SSM_GROUPS * SSM_STATE
BC_WIDTH = 2 * GROUP_WIDTH
SM_KI, SM_WI, SM_DT = 0, IDX_DIM, IDX_DIM + N_IDX_HEADS
SMALL_WIDTH = LANES
INPROJ_TN = 2 * ATT_WIDTH
IN_KI = 3 * ATT_WIDTH + N_IDX_HEADS * IDX_DIM
IN_Z = IN_KI + IDX_DIM + N_IDX_HEADS
IN_B = IN_Z + 2 * SSM_WIDTH
IN_WIDTH = IN_B + BC_WIDTH + N_SSM_HEADS

F32 = jnp.float32
BF16 = jnp.bfloat16

_INT_MIN = -(2 ** 31)
_INT_MAX = 2 ** 31 - 1
_KEY_NEG_INF = int(np.array([-np.inf], np.float32).view(np.int32)[0]) ^ 0x7FFFFFFF
_NEG_BIG = -1e30


def _dot(a, b):
    return jnp.dot(a, b, preferred_element_type=F32)


def _dot_nt(a, b):
    return lax.dot_general(a, b, (((1,), (1,)), ((), ())), preferred_element_type=F32)


def _dot_tn(a, b):
    return lax.dot_general(a, b, (((0,), (0,)), ((), ())), preferred_element_type=F32)


def _rms(x, g):
    return x * lax.rsqrt(jnp.mean(x * x, axis=-1, keepdims=True) + NORM_EPS) * g


def _silu(x):
    return x * (1.0 / (1.0 + jnp.exp(-x)))


def _inproj_body(x_ref, g_ref, wb_ref, wsa_ref, wsb_ref, gk_ref, big_ref, bc_ref, small_ref, kin_ref, k4_ref,
                 v4_ref, *rest, kv_bf16):
    u_scr = rest[-1]
    n = pl.program_id(1)
    rows = x_ref.shape[0]

    def head_rows(out_ref, cols):
        for h in range(N_ATT_HEADS):
            out_ref[pl.ds(h, rows, stride=N_ATT_HEADS), :] = cols[:, h * ATT_HEAD_DIM:(h + 1) * ATT_HEAD_DIM]

    @pl.when(n == 0)
    def _():
        ub = _rms(x_ref[...], g_ref[...]).astype(BF16)
        u_scr[...] = ub
        nar_a = _dot_nt(ub, wsa_ref[...])
        nar_b = _dot_nt(ub, wsb_ref[...])
        bc_ref[...] = nar_b[:, :BC_WIDTH]
        pad = jnp.zeros((ub.shape[0], SMALL_WIDTH - SM_DT - N_SSM_HEADS), F32)
        small_ref[...] = jnp.concatenate([nar_a, nar_b[:, BC_WIDTH:], pad], axis=1)
        kin_ref[...] = _rms(nar_a[:, :IDX_DIM], gk_ref[...])

    tile = _dot_nt(u_scr[...], wb_ref[...])
    big_ref[...] = tile

    @pl.when(n == COL_K // INPROJ_TN)
    def _():
        k = tile[:, COL_K % INPROJ_TN:COL_K % INPROJ_TN + ATT_WIDTH]
        head_rows(k4_ref, k)
        if kv_bf16:
            rest[0][...] = k.astype(BF16)

    @pl.when(n == COL_V // INPROJ_TN)
    def _():
        v = tile[:, COL_V % INPROJ_TN:COL_V % INPROJ_TN + ATT_WIDTH]
        head_rows(v4_ref, v)
        if kv_bf16:
            rest[1][0] = v.T.astype(BF16)


def _inproj(x2, g, w_t, gk, *, tm, seq_len=None):
    m, d = x2.shape
    kv_bf16 = seq_len is not None
    assert IN_Z + INPROJ_TN == IN_B and IN_KI == 2 * INPROJ_TN
    assert (IN_Z - IN_KI) % (2 * SUBLANES) == 0 and IN_B % (2 * SUBLANES) == 0
    big_rows = lambda i, j: (pl.multiple_of(j * INPROJ_TN + (j // 2) * (IN_Z - IN_KI), 2 * SUBLANES), 0)
    out_specs = [
        pl.BlockSpec((tm, INPROJ_TN), lambda i, j: (i, j)),
        pl.BlockSpec((tm, BC_WIDTH), lambda i, j: (i, 0)),
        pl.BlockSpec((tm, SMALL_WIDTH), lambda i, j: (i, 0)),
        pl.BlockSpec((tm, IDX_DIM), lambda i, j: (i, 0)),
        pl.BlockSpec((tm * N_ATT_HEADS, ATT_HEAD_DIM), lambda i, j: (i, 0)),
        pl.BlockSpec((tm * N_ATT_HEADS, ATT_HEAD_DIM), lambda i, j: (i, 0)),
    ]
    out_shape = [
        jax.ShapeDtypeStruct((m, BIG_WIDTH), F32),
        jax.ShapeDtypeStruct((m, BC_WIDTH), F32),
        jax.ShapeDtypeStruct((m, SMALL_WIDTH), F32),
        jax.ShapeDtypeStruct((m, IDX_DIM), F32),
        jax.ShapeDtypeStruct((m * N_ATT_HEADS, ATT_HEAD_DIM), F32),
        jax.ShapeDtypeStruct((m * N_ATT_HEADS, ATT_HEAD_DIM), F32),
    ]
    if kv_bf16:
        assert seq_len % tm == 0
        per_seq = seq_len // tm
        out_specs += [
            pl.BlockSpec((tm, ATT_WIDTH), lambda i, j: (i, 0)),
            pl.BlockSpec((1, ATT_WIDTH, tm), lambda i, j: (i // per_seq, 0, i % per_seq)),
        ]
        out_shape += [
            jax.ShapeDtypeStruct((m, ATT_WIDTH), BF16),
            jax.ShapeDtypeStruct((m // seq_len, ATT_WIDTH, seq_len), BF16),
        ]
    return pl.pallas_call(
        functools.partial(_inproj_body, kv_bf16=kv_bf16),
        grid=(m // tm, BIG_WIDTH // INPROJ_TN),
        in_specs=[
            pl.BlockSpec((tm, d), lambda i, j: (i, 0)),
            pl.BlockSpec((1, d), lambda i, j: (0, 0)),
            pl.BlockSpec((pl.Element(INPROJ_TN), pl.Element(d)), big_rows),
            pl.BlockSpec((pl.Element(IN_Z - IN_KI), pl.Element(d)), lambda i, j: (IN_KI, 0),
                         pipeline_mode=pl.Buffered(1)),
            pl.BlockSpec((pl.Element(IN_WIDTH - IN_B), pl.Element(d)), lambda i, j: (IN_B, 0),
                         pipeline_mode=pl.Buffered(1)),
            pl.BlockSpec((1, IDX_DIM), lambda i, j: (0, 0)),
        ],
        out_specs=out_specs,
        out_shape=out_shape,
        scratch_shapes=[pltpu.VMEM((tm, d), BF16)],
        compiler_params=pltpu.CompilerParams(
            dimension_semantics=("parallel", "arbitrary"), vmem_limit_bytes=INPROJ_VMEM_LIMIT),
        name="inproj",
    )(x2, g, w_t, w_t, w_t, gk)


def _sortable_key(s):
    s = jnp.where(s == 0.0, 0.0, s)
    b = pltpu.bitcast(s, jnp.int32)
    return b ^ ((b >> 31) & 0x7FFFFFFF)


def _count_keys(key_ref, nblk, pred, key_axis):
    def body(kb, acc):
        return acc + jnp.where(pred(key_ref[kb], kb), 1.0, 0.0)

    acc = jnp.zeros(key_ref.shape[1:], F32)
    if isinstance(nblk, int):
        for kb in range(nblk):
            acc = body(kb, acc)
    else:
        acc = lax.fori_loop(0, nblk, body, acc)
    return jnp.sum(acc, axis=key_axis, keepdims=True)


_I16_MIN = -(2 ** 15)


def _pack_halves(key):
    r = key.shape[0] // 2
    a, b = key[:r], key[r:]
    hi = (a & jnp.int32(-65536)) | ((b >> 16) & 0xFFFF)
    lo = ((a ^ 0x8000) << 16) | ((b ^ 0x8000) & 0xFFFF)
    return hi, lo


def _count16(ref, nblk, cmp, cand):
    r = ref.shape[1]
    pair = (cand << 16) | (cand & 0xFFFF)
    cb = pltpu.bitcast(jnp.broadcast_to(pair, (r, LANES)), jnp.int16)
    acc = jnp.zeros((2 * r, LANES), jnp.int16)
    for kb in range(nblk):
        acc = acc + jnp.where(cmp(pltpu.bitcast(ref[kb], jnp.int16), cb), jnp.int16(1), jnp.int16(0))
    words = pltpu.bitcast(acc, jnp.int32)
    both = (words >> 16) + ((words << 16) >> 16)
    return jnp.sum(both.astype(F32), axis=0, keepdims=True)


def _kth_of_16bit(ref, nblk, need):
    ge = lambda k, c: k >= c
    res = jnp.where(_count16(ref, nblk, ge, jnp.zeros((1, LANES), jnp.int32)) >= need, 0, _I16_MIN).astype(jnp.int32)

    def body(it, res):
        cand = res | jnp.left_shift(jnp.int32(1), 14 - it)
        return jnp.where(_count16(ref, nblk, ge, cand) >= need, cand, res)

    return lax.fori_loop(0, 15, body, res)


def _kth_key_two_phase(hi_ref, lo_ref, nblk, topk):
    kf = float(topk)
    top = _kth_of_16bit(hi_ref, nblk, kf)
    n_above = _count16(hi_ref, nblk, lambda k, c: k > c, top)
    r = hi_ref.shape[1]
    top_b = pltpu.bitcast(jnp.broadcast_to((top << 16) | (top & 0xFFFF), (r, LANES)), jnp.int16)
    for kb in range(nblk):
        band = pltpu.bitcast(hi_ref[kb], jnp.int16) == top_b
        lo_ref[kb] = pltpu.bitcast(jnp.where(band, pltpu.bitcast(lo_ref[kb], jnp.int16), jnp.int16(_I16_MIN)),
                                   jnp.int32)
    low = _kth_of_16bit(lo_ref, nblk, kf - n_above)
    return (top << 16) | ((low ^ 0x8000) & 0xFFFF)


def _topk_select_params(key_ref, nblk, topk, key_axis, thr=None):
    shape = key_ref.shape[1:]
    qshape = tuple(1 if a == key_axis else n for a, n in enumerate(shape))
    blk = shape[key_axis]
    kf = float(topk)

    def count_ge(cand):
        cb = jnp.broadcast_to(cand, shape)
        return _count_keys(key_ref, nblk, lambda k, kb: k >= cb, key_axis)

    if thr is None:
        res = jnp.where(count_ge(jnp.zeros(qshape, jnp.int32)) >= kf, 0, _INT_MIN).astype(jnp.int32)

        def bit_body(it, res):
            cand = res | jnp.left_shift(jnp.int32(1), 30 - it)
            return jnp.where(count_ge(cand) >= kf, cand, res)

        thr = lax.fori_loop(0, 31, bit_body, res)
    thr_b = jnp.broadcast_to(thr, shape)
    n_gt = _count_keys(key_ref, nblk, lambda k, kb: k > thr_b, key_axis)
    n_ge = count_ge(thr)
    tied = jnp.logical_and(n_ge > kf, thr > _KEY_NEG_INF)
    any_tied = jnp.max(jnp.where(tied, 1.0, 0.0)) > 0.0
    need = kf - n_gt
    kpos = lax.broadcasted_iota(jnp.int32, shape, key_axis)

    def tie_break():
        def body(it, cut):
            cand = cut | jnp.left_shift(jnp.int32(1), 30 - it)
            cand_b = jnp.broadcast_to(cand, shape)
            n = _count_keys(key_ref, nblk,
                            lambda k, kb: jnp.logical_and(k == thr_b, kpos + kb * blk < cand_b), key_axis)
            return jnp.where(n <= need, cand, cut)

        return lax.fori_loop(0, 31, body, jnp.zeros(qshape, jnp.int32))

    cut = lax.cond(any_tied, tie_break, lambda: jnp.full(qshape, _INT_MAX, jnp.int32))
    return thr, cut


def _attn_prompt_body(q_ref, qi_ref, smt_ref, kin_ref, k_ref, vt_ref, bias_ref, o_ref,
                      key_scr, hi_scr, lo_scr, mask_scr, lg_scr, qi_scr, *, topk, group):
    i = pl.program_id(1)
    q0 = i * LANES
    blk = (LANES, LANES)
    scale = ATT_HEAD_DIM ** -0.5
    nq = key_scr.shape[0]

    w_t = smt_ref[0, SM_WI:SM_WI + N_IDX_HEADS, :]
    qi_all = qi_ref[0]
    for h in range(N_IDX_HEADS):
        qi_scr[h * LANES:(h + 1) * LANES, :] = qi_all[:, h * IDX_DIM:(h + 1) * IDX_DIM].astype(BF16)
    qi = qi_scr[...]
    kpos = lax.broadcasted_iota(jnp.int32, blk, 0)
    qpos = lax.broadcasted_iota(jnp.int32, blk, 1) + q0
    qb = q_ref[0].astype(BF16)

    def tile(nb, first_near):
        if nb * LANES <= topk:
            for kb in range(nb):
                mask_scr[kb * LANES:(kb + 1) * LANES, :] = jnp.where(kpos + kb * LANES <= qpos, 0.0, -jnp.inf)
        else:
            select_mask(nb, first_near)
        attend(nb, first_near)

    def select_mask(nb, first_near):
        for kb in range(nb):
            kin = kin_ref[0, kb * LANES:(kb + 1) * LANES, :].astype(BF16)
            s = _dot_nt(kin, qi)
            score = jnp.zeros(blk, F32)
            for h in range(N_IDX_HEADS):
                score = score + jnp.maximum(s[:, h * LANES:(h + 1) * LANES], 0.0) * w_t[h:h + 1, :]
            if kb >= first_near:
                score = jnp.where(kpos + kb * LANES <= qpos, score, -jnp.inf)
            key = _sortable_key(score)
            key_scr[kb] = key
            hi_scr[kb], lo_scr[kb] = _pack_halves(key)

        thr = _kth_key_two_phase(hi_scr, lo_scr, nb, topk)
        thr, cut = _topk_select_params(key_scr, nb, topk, key_axis=0, thr=thr)
        thr_b = jnp.broadcast_to(thr, blk)
        cut_b = jnp.broadcast_to(cut, blk)
        for kb in range(nb):
            kp = kpos + kb * LANES
            key = key_scr[kb]
            valid = jnp.logical_or(key > thr_b, jnp.logical_and(key == thr_b, kp < cut_b))
            if kb >= first_near:
                valid = jnp.logical_and(valid, kp <= qpos)
            mask_scr[kb * LANES:(kb + 1) * LANES, :] = jnp.where(valid, 0.0, -jnp.inf)

    def attend(nb, first_near):
        nk = nb * LANES
        for h in range(N_ATT_HEADS):
            hs = slice(h * ATT_HEAD_DIM, (h + 1) * ATT_HEAD_DIM)
            lg = _dot_nt(k_ref[0, 0:nk, hs], qb[:, hs]) * scale + mask_scr[0:nk, :]
            for kb in range(first_near, nb):
                tix = 2 - jnp.clip(i - kb, 0, 2)
                lg_scr[kb * LANES:(kb + 1) * LANES, :] = lg[kb * LANES:(kb + 1) * LANES, :] + bias_ref[tix, h]
            if first_near > 0:
                lg_scr[0:first_near * LANES, :] = lg[0:first_near * LANES, :]
            lg = lg_scr[0:nk, :]
            m = jnp.max(lg, axis=0, keepdims=True)
            p = jnp.exp(lg - m)
            l = jnp.sum(p, axis=0, keepdims=True)
            out_t = _dot(vt_ref[0, hs, 0:nk], p.astype(BF16)) / l
            o_ref[0, :, hs] = out_t.T.astype(o_ref.dtype)

    ngroups = -(-nq // group)
    for c in range(ngroups):
        nb = min((c + 1) * group, nq)

        @pl.when(i // group == c)
        def _(nb=nb, c=c):
            tile(nb, max(c * group - 1, 0))


def _attn_prompt(big3, small3_t, kin3, k16, vt16, bias_tiles_t, *, topk, group):
    b, s, _ = big3.shape
    nq = s // LANES
    body = functools.partial(_attn_prompt_body, topk=topk, group=group)
    return pl.pallas_call(
        body,
        grid=(b, nq),
        in_specs=[
            pl.BlockSpec((1, LANES, ATT_WIDTH), lambda bi, i: (bi, i, COL_Q // ATT_WIDTH)),
            pl.BlockSpec((1, LANES, N_IDX_HEADS * IDX_DIM), lambda bi, i: (bi, i, COL_QI // (N_IDX_HEADS * IDX_DIM))),
            pl.BlockSpec((1, SMALL_WIDTH, LANES), lambda bi, i: (bi * nq + i, 0, 0)),
            pl.BlockSpec((1, s, IDX_DIM), lambda bi, i: (bi, 0, 0)),
            pl.BlockSpec((1, s, ATT_WIDTH), lambda bi, i: (bi, 0, 0)),
            pl.BlockSpec((1, ATT_WIDTH, s), lambda bi, i: (bi, 0, 0)),
            pl.BlockSpec(bias_tiles_t.shape, lambda bi, i: (0, 0, 0, 0)),
        ],
        out_specs=pl.BlockSpec((1, LANES, ATT_WIDTH), lambda bi, i: (bi, i, 0)),
        out_shape=jax.ShapeDtypeStruct((b, s, ATT_WIDTH), BF16),
        scratch_shapes=[
            pltpu.VMEM((nq, LANES, LANES), jnp.int32),
            pltpu.VMEM((nq, LANES // 2, LANES), jnp.int32),
            pltpu.VMEM((nq, LANES // 2, LANES), jnp.int32),
            pltpu.VMEM((s, LANES), F32),
            pltpu.VMEM((s, LANES), F32),
            pltpu.VMEM((N_IDX_HEADS * LANES, IDX_DIM), BF16),
        ],
        compiler_params=pltpu.CompilerParams(
            dimension_semantics=("parallel", "arbitrary"), vmem_limit_bytes=VMEM_LIMIT),
        name="attn_prompt",
    )(big3, big3, small3_t, kin3, k16, vt16, bias_tiles_t)


def _select_sample_body(pt_ref, qi_ref, sm_ref, kinew_ref, *rest, n_pages, t, topk, ss):
    kidx_refs = rest[:ss * n_pages]
    sel_ref = rest[ss * n_pages]
    key_scr, wb_scr = rest[ss * n_pages + 1:]
    nblk = n_pages + 1
    j = pl.program_id(1)
    trow = lax.broadcasted_iota(jnp.int32, (t, LANES), 0)
    lane_t = lax.broadcasted_iota(jnp.int32, (t, LANES), 1)

    def scores(s):
        return jnp.sum(jnp.maximum(s.reshape(N_IDX_HEADS, t, LANES), 0.0) * wb_scr[...], axis=0)

    for u in range(ss):
        r0 = pl.multiple_of((j * ss + u) * t, t)
        wi = sm_ref[u, :, SM_WI:SM_WI + N_IDX_HEADS]
        for h in range(N_IDX_HEADS):
            wb_scr[h] = jnp.broadcast_to(wi[:, h:h + 1], (t, LANES))
        qi_all = qi_ref[u]
        qi = jnp.concatenate([qi_all[:, h * IDX_DIM:(h + 1) * IDX_DIM] for h in range(N_IDX_HEADS)],
                             axis=0).astype(BF16)
        for p in range(n_pages):
            kin_t = kidx_refs[u * n_pages + p][0, 0].astype(BF16)
            key_scr[p, pl.ds(r0, t), :] = _sortable_key(scores(_dot(qi, kin_t)))
        kin_new = jnp.concatenate([kinew_ref[u], jnp.zeros((LANES - t, IDX_DIM), F32)], axis=0)
        s_new = jnp.where(lane_t <= trow, scores(_dot_nt(qi, kin_new.astype(BF16))), -jnp.inf)
        key_scr[n_pages, pl.ds(r0, t), :] = _sortable_key(s_new)

    @pl.when(j == pl.num_programs(1) - 1)
    def _():
        rows = key_scr.shape[1]
        thr, cut = _topk_select_params(key_scr, nblk, topk, key_axis=1)
        thr_b = jnp.broadcast_to(thr, (rows, LANES))
        cut_b = jnp.broadcast_to(cut, (rows, LANES))
        lane = lax.broadcasted_iota(jnp.int32, (rows, LANES), 1)
        tq = lax.broadcasted_iota(jnp.int32, (rows, LANES), 0) & (t - 1)
        for p in range(nblk):
            key = key_scr[p]
            sel = jnp.logical_or(key > thr_b, jnp.logical_and(key == thr_b, lane + p * LANES < cut_b))
            if p == n_pages:
                sel = jnp.logical_and(sel, lane <= tq)
            sel_ref[0, p] = jnp.where(sel, 1.0, 0.0)


def _select_sample(page_table, big3, small3, kin3, cache_kidx_t, *, topk, gs):
    b, t, _ = small3.shape
    n_pages = page_table.shape[1]
    ss = math.gcd(gs, SELECT_SEQS_PER_STEP)
    steps = gs // ss
    assert b % gs == 0 and t & (t - 1) == 0
    body = functools.partial(_select_sample_body, n_pages=n_pages, t=t, topk=topk, ss=ss)
    blk = lambda g, j: g * steps + j
    qi_width = N_IDX_HEADS * IDX_DIM
    in_specs = [
        pl.BlockSpec((ss, t, qi_width), lambda g, j, pt: (blk(g, j), 0, COL_QI // qi_width)),
        pl.BlockSpec((ss, t, SMALL_WIDTH), lambda g, j, pt: (blk(g, j), 0, 0)),
        pl.BlockSpec((ss, t, IDX_DIM), lambda g, j, pt: (blk(g, j), 0, 0)),
    ]
    in_specs += [pl.BlockSpec((1, 1, IDX_DIM, PAGE_SIZE),
                              lambda g, j, pt, u=u, p=p: (0, pt[blk(g, j) * ss + u, p], 0, 0))
                 for u in range(ss) for p in range(n_pages)]
    sel_shape = (b // gs, n_pages + 1, gs * t, LANES)
    return pl.pallas_call(
        body,
        grid_spec=pltpu.PrefetchScalarGridSpec(
            num_scalar_prefetch=1,
            grid=(b // gs, steps),
            in_specs=in_specs,
            out_specs=pl.BlockSpec((1,) + sel_shape[1:], lambda g, j, pt: (g, 0, 0, 0)),
            scratch_shapes=[
                pltpu.VMEM(sel_shape[1:], jnp.int32),
                pltpu.VMEM((N_IDX_HEADS, t, LANES), F32),
            ],
        ),
        out_shape=jax.ShapeDtypeStruct(sel_shape, F32),
        compiler_params=pltpu.CompilerParams(
            dimension_semantics=("parallel", "arbitrary"), vmem_limit_bytes=VMEM_LIMIT),
        name="select_sample",
    )(page_table, big3, small3, kin3, *([cache_kidx_t] * (ss * n_pages)))


def _attn_sample_body(pt_ref, q_ref, knew_ref, vnew_ref, sel_ref, bias_ref, *rest, n_pages, t):
    k_refs = rest[:n_pages]
    v_refs = rest[n_pages:2 * n_pages]
    o_ref = rest[2 * n_pages]
    (lg_scr,) = rest[2 * n_pages + 1:]
    nblk = n_pages + 1
    rows = N_ATT_HEADS * t
    flat = PAGE_SIZE * N_ATT_HEADS
    scale = ATT_HEAD_DIM ** -0.5

    q = q_ref[0]
    q64 = jnp.concatenate([q[:, h * ATT_HEAD_DIM:(h + 1) * ATT_HEAD_DIM] for h in range(N_ATT_HEADS)],
                          axis=0).astype(BF16)
    er = lax.broadcasted_iota(jnp.int32, (LANES, flat), 0)
    ec = lax.broadcasted_iota(jnp.int32, (LANES, flat), 1)
    expand = jnp.where((ec >> (N_ATT_HEADS.bit_length() - 1)) == er, 1.0, 0.0).astype(BF16)

    def flat_kv(refs, p):
        if p < n_pages:
            return refs[p][0, 0].reshape(flat, ATT_HEAD_DIM).astype(BF16), flat
        new = refs[p][0].reshape(t * N_ATT_HEADS, ATT_HEAD_DIM)
        pad = jnp.zeros((LANES - t * N_ATT_HEADS, ATT_HEAD_DIM), F32)
        return jnp.concatenate([new, pad], axis=0).astype(BF16), LANES

    k_all = list(k_refs) + [knew_ref]
    v_all = list(v_refs) + [vnew_ref]
    m = jnp.full((rows, 1), _NEG_BIG, F32)
    for p in range(nblk):
        kflat, width = flat_kv(k_all, p)
        selx = _dot(sel_ref[0, p].astype(BF16), expand[:, :width])
        unsel = (selx - 1.0) * (-_NEG_BIG)
        tix = 2 - min(n_pages - p, 2)
        lg = _dot_nt(q64, kflat) * scale + bias_ref[tix, :, :width] + jnp.concatenate([unsel] * N_ATT_HEADS, axis=0)
        lg_scr[p, :, :width] = lg
        m = jnp.maximum(m, jnp.max(lg, axis=-1, keepdims=True))

    l = jnp.zeros((rows, 1), F32)
    acc = jnp.zeros((rows, ATT_HEAD_DIM), F32)
    for p in range(nblk):
        vflat, width = flat_kv(v_all, p)
        pr = jnp.exp(lg_scr[p, :, :width] - m)
        l = l + jnp.sum(pr, axis=-1, keepdims=True)
        acc = acc + _dot(pr.astype(BF16), vflat)
    out = acc / l
    for h in range(N_ATT_HEADS):
        o_ref[0, :, h * ATT_HEAD_DIM:(h + 1) * ATT_HEAD_DIM] = out[h * t:(h + 1) * t, :].astype(o_ref.dtype)


def _attn_sample(page_table, big3, k_new, v_new, sel, bias_rows, cache_k, cache_v, *, gs):
    b, t, _ = big3.shape
    n_pages = page_table.shape[1]
    assert t == SUBLANES and N_ATT_HEADS & (N_ATT_HEADS - 1) == 0 and t * N_ATT_HEADS <= LANES
    body = functools.partial(_attn_sample_body, n_pages=n_pages, t=t)
    kv_block = (1, 1, PAGE_SIZE, N_ATT_HEADS, ATT_HEAD_DIM)
    new_block = (1, t, N_ATT_HEADS, ATT_HEAD_DIM)

    in_specs = [
        pl.BlockSpec((1, t, ATT_WIDTH), lambda bi, pt: (bi, 0, COL_Q // ATT_WIDTH)),
        pl.BlockSpec(new_block, lambda bi, pt: (bi, 0, 0, 0)),
        pl.BlockSpec(new_block, lambda bi, pt: (bi, 0, 0, 0)),
        pl.BlockSpec((1, n_pages + 1, t, LANES), lambda bi, pt: (bi // gs, 0, bi % gs, 0)),
        pl.BlockSpec(bias_rows.shape, lambda bi, pt: (0, 0, 0)),
    ]
    in_specs += [pl.BlockSpec(kv_block, lambda bi, pt, p=p: (0, pt[bi, p], 0, 0, 0)) for p in range(n_pages)] * 2
    rows = N_ATT_HEADS * t
    return pl.pallas_call(
        body,
        grid_spec=pltpu.PrefetchScalarGridSpec(
            num_scalar_prefetch=1,
            grid=(b,),
            in_specs=in_specs,
            out_specs=pl.BlockSpec((1, t, ATT_WIDTH), lambda bi, pt: (bi, 0, 0)),
            scratch_shapes=[pltpu.VMEM((n_pages + 1, rows, PAGE_SIZE * N_ATT_HEADS), F32)],
        ),
        out_shape=jax.ShapeDtypeStruct((b, t, ATT_WIDTH), BF16),
        compiler_params=pltpu.CompilerParams(
            dimension_semantics=("arbitrary",), vmem_limit_bytes=VMEM_LIMIT),
        name="attn_sample",
    )(page_table, big3, k_new, v_new, sel, bias_rows, *([cache_k] * n_pages), *([cache_v] * n_pages))


def _split3(a):
    a1 = a.astype(BF16)
    r1 = a - a1.astype(F32)
    a2 = r1.astype(BF16)
    a3 = (r1 - a2.astype(F32)).astype(BF16)
    return a1, a2, a3


def _softplus(v):
    return jnp.maximum(v, 0.0) + jnp.log1p(jnp.exp(-jnp.abs(v)))


def _head_expander():
    r = lax.broadcasted_iota(jnp.int32, (SMALL_WIDTH, SSM_WIDTH), 0)
    c = lax.broadcasted_iota(jnp.int32, (SMALL_WIDTH, SSM_WIDTH), 1)
    return jnp.where(r - SM_DT == c // SSM_HEAD_DIM, 1.0, 0.0).astype(BF16)


def _ssd_body(x_ref, bm_ref, cm_ref, z_ref, sm_ref, smt_ref, pre_ref, h0_ref,
              cw_ref, cb_ref, dtb_ref, dtbt_ref, alog_ref, alogt_ref, dskipx_ref, gn_ref,
              y_ref, tail_ref, hout_ref, xp_scr, ht_scr, y_scr, *, l):
    c = pl.program_id(1)
    nc = pl.num_programs(1)
    off = SUBLANES
    heads_per_group = N_SSM_HEADS // N_SSM_GROUPS
    gcols = heads_per_group * SSM_HEAD_DIM

    @pl.when(c == 0)
    def _():
        xp_scr[off - (CONV_W - 1):off, :] = pre_ref[0]
        for h in range(N_SSM_HEADS):
            g, hh = divmod(h, heads_per_group)
            ht_scr[g, :, hh * SSM_HEAD_DIM:(hh + 1) * SSM_HEAD_DIM] = h0_ref[0, h].T

    xp_scr[off:off + l, 0:SSM_WIDTH] = x_ref[0]
    xp_scr[off:off + l, SSM_WIDTH:SSM_WIDTH + GROUP_WIDTH] = bm_ref[0]
    xp_scr[off:off + l, SSM_WIDTH + GROUP_WIDTH:CONV_DIM] = cm_ref[0]

    conv = cb_ref[...]
    for j in range(CONV_W):
        s0 = off - (CONV_W - 1) + j
        conv = conv + xp_scr[s0:s0 + l, :] * cw_ref[j:j + 1, :]
    xc = _silu(conv)
    tail = xp_scr[off + l - (CONV_W - 1):off + l, :]
    tail_ref[0] = tail
    xp_scr[off - (CONV_W - 1):off, :] = tail

    xs = xc[:, :SSM_WIDTH]
    bc = [xc[:, SSM_WIDTH + g * SSM_STATE:SSM_WIDTH + (g + 1) * SSM_STATE].astype(BF16)
          for g in range(N_SSM_GROUPS)]
    cc = [xc[:, SSM_WIDTH + GROUP_WIDTH + g * SSM_STATE:SSM_WIDTH + GROUP_WIDTH + (g + 1) * SSM_STATE]
          .astype(BF16) for g in range(N_SSM_GROUPS)]

    dt = _softplus(sm_ref[0] + dtb_ref[...])
    a = dt * (-jnp.exp(alog_ref[...]))
    dtt = _softplus(smt_ref[0, SM_DT:SM_DT + N_SSM_HEADS, :] + dtbt_ref[...])
    at = dtt * (-jnp.exp(alogt_ref[...]))

    ri = lax.broadcasted_iota(jnp.int32, (l, l), 0)
    ci = lax.broadcasted_iota(jnp.int32, (l, l), 1)
    causal = ri >= ci
    tri = jnp.where(causal, 1.0, 0.0).astype(BF16)
    tri_u = jnp.where(ri <= ci, 1.0, 0.0).astype(BF16)
    acum = sum(_dot(tri, piece) for piece in _split3(a))
    acum_t = sum(_dot(piece, tri_u) for piece in _split3(at))
    last = acum[l - 1:l, :]
    wts = jnp.exp(last - acum) * dt
    expand = _head_expander()
    acum_x = sum(_dot(piece, expand) for piece in _split3(acum))
    wts_x = sum(_dot(piece, expand) for piece in _split3(wts))
    exp_last_x = jnp.exp(acum_x[l - 1:l, :])

    cb = [_dot_nt(cc[g], bc[g]) for g in range(N_SSM_GROUPS)]
    for h in range(N_SSM_HEADS):
        g = h // heads_per_group
        hs = slice(h * SSM_HEAD_DIM, (h + 1) * SSM_HEAD_DIM)
        col = acum[:, SM_DT + h:SM_DT + h + 1]
        decay = jnp.exp(jnp.where(causal, col - acum_t[h:h + 1, :], -jnp.inf))
        mmat = cb[g] * decay * dtt[h:h + 1, :]
        y_scr[:, hs] = _dot(mmat.astype(BF16), xs[:, hs].astype(BF16))

    xw = (xs * wts_x).astype(BF16)
    y_inter = []
    for g in range(N_SSM_GROUPS):
        gsl = slice(g * gcols, (g + 1) * gcols)
        ht = ht_scr[g]
        y_inter.append(_dot(cc[g], ht.astype(BF16)))
        ht_scr[g] = ht * exp_last_x[:, gsl] + _dot_tn(bc[g], xw[:, gsl])
    y = y_scr[...] + jnp.concatenate(y_inter, axis=1) * jnp.exp(acum_x) + dskipx_ref[...] * xs
    y_ref[0] = _rms(y * _silu(z_ref[0]), gn_ref[...]).astype(y_ref.dtype)

    @pl.when(c == nc - 1)
    def _():
        for h in range(N_SSM_HEADS):
            g, hh = divmod(h, heads_per_group)
            hout_ref[0, h] = ht_scr[g, :, hh * SSM_HEAD_DIM:(hh + 1) * SSM_HEAD_DIM].T


def _pad_lanes(v):
    return jnp.zeros((1, SMALL_WIDTH), F32).at[0, SM_DT:SM_DT + N_SSM_HEADS].set(v)


def _ssd(big3, bc3, small3, small3_t, prefix, h0, conv_w, conv_b, dt_bias, a_log, d_skip, ssm_norm, *, l):
    b, t, _ = big3.shape
    nc = t // l
    body = functools.partial(_ssd_body, l=l)
    row = lambda v: v.reshape(1, -1)
    colv = lambda v: v.reshape(-1, 1)
    const2 = lambda shape: pl.BlockSpec(shape, lambda bi, c: (0, 0))
    return pl.pallas_call(
        body,
        grid=(b, nc),
        in_specs=[
            pl.BlockSpec((1, l, SSM_WIDTH), lambda bi, c: (bi, c, COL_X // SSM_WIDTH)),
            pl.BlockSpec((1, l, GROUP_WIDTH), lambda bi, c: (bi, c, 0)),
            pl.BlockSpec((1, l, GROUP_WIDTH), lambda bi, c: (bi, c, 1)),
            pl.BlockSpec((1, l, SSM_WIDTH), lambda bi, c: (bi, c, COL_Z // SSM_WIDTH)),
            pl.BlockSpec((1, l, SMALL_WIDTH), lambda bi, c: (bi, c, 0)),
            pl.BlockSpec((1, SMALL_WIDTH, l), lambda bi, c: (bi * nc + c, 0, 0)),
            pl.BlockSpec((1, CONV_W - 1, CONV_DIM), lambda bi, c: (bi, 0, 0)),
            pl.BlockSpec((1, N_SSM_HEADS, SSM_HEAD_DIM, SSM_STATE), lambda bi, c: (bi, 0, 0, 0)),
            const2((CONV_W, CONV_DIM)),
            const2((1, CONV_DIM)),
            const2((1, SMALL_WIDTH)),
            const2((N_SSM_HEADS, 1)),
            const2((1, SMALL_WIDTH)),
            const2((N_SSM_HEADS, 1)),
            const2((1, SSM_WIDTH)),
            const2((1, SSM_WIDTH)),
        ],
        out_specs=[
            pl.BlockSpec((1, l, SSM_WIDTH), lambda bi, c: (bi, c, 0)),
            pl.BlockSpec((1, CONV_W - 1, CONV_DIM), lambda bi, c: (bi, 0, 0)),
            pl.BlockSpec((1, N_SSM_HEADS, SSM_HEAD_DIM, SSM_STATE), lambda bi, c: (bi, 0, 0, 0)),
        ],
        out_shape=[
            jax.ShapeDtypeStruct((b, t, SSM_WIDTH), BF16),
            jax.ShapeDtypeStruct((b, CONV_W - 1, CONV_DIM), F32),
            jax.ShapeDtypeStruct((b, N_SSM_HEADS, SSM_HEAD_DIM, SSM_STATE), F32),
        ],
        scratch_shapes=[
            pltpu.VMEM((SUBLANES + l, CONV_DIM), F32),
            pltpu.VMEM((N_SSM_GROUPS, SSM_STATE, SSM_WIDTH // N_SSM_GROUPS), F32),
            pltpu.VMEM((l, SSM_WIDTH), F32),
        ],
        compiler_params=pltpu.CompilerParams(
            dimension_semantics=("parallel", "arbitrary"), vmem_limit_bytes=VMEM_LIMIT),
        name="ssd",
    )(big3, bc3, bc3, big3, small3, small3_t, prefix, h0,
      conv_w, row(conv_b), _pad_lanes(dt_bias), colv(dt_bias), _pad_lanes(a_log), colv(a_log),
      row(jnp.repeat(d_skip, SSM_HEAD_DIM)), row(ssm_norm))


def _ssd_sample_body(x_ref, bm_ref, cm_ref, z_ref, sm_ref, smt_ref, pre_ref, h0_ref,
                     cw_ref, cb_ref, dtb_ref, dtbt_ref, alog_ref, alogt_ref, dskip_ref, gn_ref,
                     y_ref, tail_ref, hout_ref, xp_scr, xc_scr, yi_scr, xw_scr, y_scr, *, gs, t):
    rows = gs * t
    off = SUBLANES
    tshift = t.bit_length() - 1
    gw = N_SSM_GROUPS * SSM_STATE
    heads_per_group = N_SSM_HEADS // N_SSM_GROUPS
    gcols = heads_per_group * SSM_HEAD_DIM

    for s in range(gs):
        xp_scr[s, off - (CONV_W - 1):off, :] = pre_ref[s]
        xp_scr[s, off:off + t, 0:SSM_WIDTH] = x_ref[s]
        xp_scr[s, off:off + t, SSM_WIDTH:SSM_WIDTH + gw] = bm_ref[s]
        xp_scr[s, off:off + t, SSM_WIDTH + gw:CONV_DIM] = cm_ref[s]
        conv = cb_ref[...]
        for j in range(CONV_W):
            s0 = off - (CONV_W - 1) + j
            conv = conv + xp_scr[s, s0:s0 + t, :] * cw_ref[j:j + 1, :]
        xc_scr[s * t:(s + 1) * t, :] = _silu(conv)
        tail_ref[s] = xp_scr[s, off + t - (CONV_W - 1):off + t, :]

    xc = xc_scr[...]
    xs = xc[:, :SSM_WIDTH]
    bcf = [xc[:, SSM_WIDTH + g * SSM_STATE:SSM_WIDTH + (g + 1) * SSM_STATE] for g in range(N_SSM_GROUPS)]
    ccf = [xc[:, SSM_WIDTH + gw + g * SSM_STATE:SSM_WIDTH + gw + (g + 1) * SSM_STATE] for g in range(N_SSM_GROUPS)]

    dt = _softplus(sm_ref[...].reshape(rows, SMALL_WIDTH)[:, SM_DT:SM_DT + N_SSM_HEADS] + dtb_ref[...])
    dtt = _softplus(smt_ref[0, SM_DT:SM_DT + N_SSM_HEADS, :] + dtbt_ref[...])
    a = dt * (-jnp.exp(alog_ref[...]))
    at = dtt * (-jnp.exp(alogt_ref[...]))

    ri = lax.broadcasted_iota(jnp.int32, (rows, rows), 0)
    ci = lax.broadcasted_iota(jnp.int32, (rows, rows), 1)
    same = (ri >> tshift) == (ci >> tshift)
    causal = jnp.logical_and(same, ri >= ci)
    tri = jnp.where(causal, 1.0, 0.0).astype(BF16)
    tri_u = jnp.where(jnp.logical_and(same, ri <= ci), 1.0, 0.0).astype(BF16)
    pick_last = jnp.where(ci == (((ri >> tshift) << tshift) + (t - 1)), 1.0, 0.0).astype(BF16)
    acum = sum(_dot(tri, piece) for piece in _split3(a))
    acum_t = sum(_dot(piece, tri_u) for piece in _split3(at))
    last = sum(_dot(pick_last, piece) for piece in _split3(acum))
    wts = jnp.exp(last - acum) * dt
    exp_last = jnp.exp(last)

    for s in range(gs):
        rs = slice(s * t, (s + 1) * t)
        for g in range(N_SSM_GROUPS):
            h0g = h0_ref[s, g * heads_per_group:(g + 1) * heads_per_group].reshape(gcols, SSM_STATE)
            yi_scr[rs, g * gcols:(g + 1) * gcols] = _dot_nt(ccf[g][rs, :], h0g)

    cb = [_dot_nt(ccf[g].astype(BF16), bcf[g].astype(BF16)) for g in range(N_SSM_GROUPS)]
    for h in range(N_SSM_HEADS):
        g = h // heads_per_group
        hs = slice(h * SSM_HEAD_DIM, (h + 1) * SSM_HEAD_DIM)
        col = acum[:, h:h + 1]
        decay = jnp.exp(jnp.where(causal, col - acum_t[h:h + 1, :], -jnp.inf))
        mmat = cb[g] * decay * dtt[h:h + 1, :]
        xh = xs[:, hs]
        y_intra = _dot(mmat.astype(BF16), xh.astype(BF16))
        y_scr[:, hs] = y_intra + yi_scr[:, hs] * jnp.exp(col) + dskip_ref[h] * xh
        xw_scr[:, hs] = xh * wts[:, h:h + 1]

    out = _rms(y_scr[...] * _silu(z_ref[...].reshape(rows, SSM_WIDTH)), gn_ref[...])
    for s in range(gs):
        y_ref[s] = out[s * t:(s + 1) * t, :].astype(y_ref.dtype)

    for s in range(gs):
        rs = slice(s * t, (s + 1) * t)
        for g in range(N_SSM_GROUPS):
            upd = _dot_tn(xw_scr[rs, g * gcols:(g + 1) * gcols], bcf[g][rs, :])
            for hh in range(heads_per_group):
                h = g * heads_per_group + hh
                scale = jnp.broadcast_to(exp_last[s * t:s * t + 1, h:h + 1], (SSM_HEAD_DIM, SSM_STATE))
                hout_ref[s, h] = h0_ref[s, h] * scale + upd[hh * SSM_HEAD_DIM:(hh + 1) * SSM_HEAD_DIM, :]


def _ssd_sample(big3, bc3, small3, small3_t, prefix, h0, conv_w, conv_b, dt_bias, a_log, d_skip, ssm_norm, *, gs):
    b, t, _ = big3.shape
    assert b % gs == 0 and t & (t - 1) == 0 and t == SUBLANES
    rows = gs * t
    body = functools.partial(_ssd_sample_body, gs=gs, t=t)
    gw = N_SSM_GROUPS * SSM_STATE
    row = lambda v: v.reshape(1, -1)
    colv = lambda v: v.reshape(-1, 1)
    const2 = lambda shape: pl.BlockSpec(shape, lambda gi: (0, 0))
    state_block = (gs, N_SSM_HEADS, SSM_HEAD_DIM, SSM_STATE)
    return pl.pallas_call(
        body,
        grid=(b // gs,),
        in_specs=[
            pl.BlockSpec((gs, t, SSM_WIDTH), lambda gi: (gi, 0, COL_X // SSM_WIDTH)),
            pl.BlockSpec((gs, t, gw), lambda gi: (gi, 0, 0)),
            pl.BlockSpec((gs, t, gw), lambda gi: (gi, 0, 1)),
            pl.BlockSpec((gs, t, SSM_WIDTH), lambda gi: (gi, 0, COL_Z // SSM_WIDTH)),
            pl.BlockSpec((gs, t, SMALL_WIDTH), lambda gi: (gi, 0, 0)),
            pl.BlockSpec((1, SMALL_WIDTH, rows), lambda gi: (gi, 0, 0)),
            pl.BlockSpec((gs, CONV_W - 1, CONV_DIM), lambda gi: (gi, 0, 0)),
            pl.BlockSpec(state_block, lambda gi: (gi, 0, 0, 0)),
            const2((CONV_W, CONV_DIM)),
            const2((1, CONV_DIM)),
            const2((1, N_SSM_HEADS)),
            const2((N_SSM_HEADS, 1)),
            const2((1, N_SSM_HEADS)),
            const2((N_SSM_HEADS, 1)),
            pl.BlockSpec(memory_space=pltpu.SMEM),
            const2((1, SSM_WIDTH)),
        ],
        out_specs=[
            pl.BlockSpec((gs, t, SSM_WIDTH), lambda gi: (gi, 0, 0)),
            pl.BlockSpec((gs, CONV_W - 1, CONV_DIM), lambda gi: (gi, 0, 0)),
            pl.BlockSpec(state_block, lambda gi: (gi, 0, 0, 0)),
        ],
        out_shape=[
            jax.ShapeDtypeStruct((b, t, SSM_WIDTH), BF16),
            jax.ShapeDtypeStruct((b, CONV_W - 1, CONV_DIM), F32),
            jax.ShapeDtypeStruct((b, N_SSM_HEADS, SSM_HEAD_DIM, SSM_STATE), F32),
        ],
        scratch_shapes=[
            pltpu.VMEM((gs, SUBLANES + t, CONV_DIM), F32),
            pltpu.VMEM((rows, CONV_DIM), F32),
            pltpu.VMEM((rows, SSM_WIDTH), F32),
            pltpu.VMEM((rows, SSM_WIDTH), F32),
            pltpu.VMEM((rows, SSM_WIDTH), F32),
        ],
        compiler_params=pltpu.CompilerParams(
            dimension_semantics=("parallel",), vmem_limit_bytes=VMEM_LIMIT),
        name="ssd_sample",
    )(big3, bc3, bc3, big3, small3, small3_t, prefix, h0,
      conv_w, row(conv_b), row(dt_bias), colv(dt_bias), row(a_log), colv(a_log), d_skip, row(ssm_norm))


def _ffn_body(x_ref, att_ref, y_ref, woa_ref, woy_ref, gf_ref, wg_ref, wu_ref, wd_ref, gl_ref,
              o_ref, f_scr):
    j = pl.program_id(1)

    @pl.when(j == 0)
    def _():
        h = x_ref[...] + _dot(att_ref[...], woa_ref[...]) + _dot(y_ref[...], woy_ref[...])
        o_ref[...] = h
        f_scr[...] = _rms(h, gf_ref[...]).astype(BF16)

    f = f_scr[...]
    act = _silu(_dot(f, wg_ref[...])) * _dot(f, wu_ref[...])
    o_ref[...] += _dot(act.astype(BF16), wd_ref[...])

    @pl.when(j == pl.num_programs(1) - 1)
    def _():
        o_ref[...] = _rms(o_ref[...], gl_ref[...])


def _out_ffn(x2, att2, y2, woa, woy, gf, wg, wu, wd, gl, *, tm, th):
    m, d = x2.shape
    hidden = wg.shape[1]
    resident = dict(pipeline_mode=pl.Buffered(1))
    return pl.pallas_call(
        _ffn_body,
        grid=(m // tm, hidden // th),
        in_specs=[
            pl.BlockSpec((tm, d), lambda i, j: (i, 0)),
            pl.BlockSpec((tm, ATT_WIDTH), lambda i, j: (i, 0)),
            pl.BlockSpec((tm, SSM_WIDTH), lambda i, j: (i, 0)),
            pl.BlockSpec((ATT_WIDTH, d), lambda i, j: (0, 0), **resident),
            pl.BlockSpec((SSM_WIDTH, d), lambda i, j: (0, 0), **resident),
            pl.BlockSpec((1, d), lambda i, j: (0, 0)),
            pl.BlockSpec((d, th), lambda i, j: (0, j)),
            pl.BlockSpec((d, th), lambda i, j: (0, j)),
            pl.BlockSpec((th, d), lambda i, j: (j, 0)),
            pl.BlockSpec((1, d), lambda i, j: (0, 0)),
        ],
        out_specs=pl.BlockSpec((tm, d), lambda i, j: (i, 0)),
        out_shape=jax.ShapeDtypeStruct((m, d), F32),
        scratch_shapes=[pltpu.VMEM((tm, d), BF16)],
        compiler_params=pltpu.CompilerParams(
            dimension_semantics=("parallel", "arbitrary"), vmem_limit_bytes=VMEM_LIMIT),
        name="out_ffn",
    )(x2, att2, y2, woa, woy, gf, wg, wu, wd, gl)


def _t5_bucket(rel):
    max_exact = N_BUCKETS // 2
    relf = jnp.maximum(rel, 1).astype(jnp.float32)
    large = max_exact + (jnp.log(relf / max_exact) / math.log(MAX_DISTANCE / max_exact)
                         * (N_BUCKETS - max_exact)).astype(jnp.int32)
    large = jnp.minimum(large, N_BUCKETS - 1)
    return jnp.where(rel < max_exact, rel, large)


def _toeplitz(u, rows, cols):
    nh, period = u.shape
    assert cols < period
    flat = jnp.tile(u, (1, rows))
    return flat[:, :rows * (period - 1)].reshape(nh, rows, period - 1)[:, :, :cols]


def _bias_tiles(rel_bias, rows, key_major):
    assert MAX_DISTANCE <= LANES and rows <= LANES
    table = rel_bias[_t5_bucket(jnp.arange(3 * LANES, dtype=jnp.int32))].T
    tiles = []
    for dd in (2, 1, 0):
        neg = table[:, (dd - 1) * LANES:dd * LANES] if dd >= 1 else jnp.tile(table[:, 0:1], (1, LANES))
        u = jnp.concatenate([table[:, dd * LANES:(dd + 1) * LANES], neg], axis=1)
        if key_major:
            tiles.append(_toeplitz(u, LANES, rows))
        else:
            w = jnp.concatenate([u[:, 0:1], jnp.flip(u[:, 1:], axis=1)], axis=1)
            tiles.append(_toeplitz(w, rows, LANES))
    return jnp.stack(tiles)


def _layer(x, att_call, ssd_call, blk_rows, p, *, kv_bf16):
    b, t, d = x.shape
    m = b * t
    tm = min(ROW_TILE, m)
    proj = _inproj(x.reshape(m, d), p["norm_attn"], p["w_t"], p["idx_k_norm"], tm=tm,
                   seq_len=t if kv_bf16 else None)
    big, bc, small, kin, k_rows, v_rows = proj[:6]
    big3 = big.reshape(b, t, BIG_WIDTH)
    bc3 = bc.reshape(b, t, BC_WIDTH)
    small3 = small.reshape(b, t, SMALL_WIDTH)
    kin3 = kin.reshape(b, t, IDX_DIM)
    hd = (b, t, N_ATT_HEADS, ATT_HEAD_DIM)
    k4 = k_rows.reshape(hd)
    v4 = v_rows.reshape(hd)
    small3_t = jnp.transpose(small.reshape(m // blk_rows, blk_rows, SMALL_WIDTH), (0, 2, 1))
    att = att_call(big3, small3, small3_t, kin3, k4, v4, *proj[6:])
    y, tail, h_fin = ssd_call(big3, bc3, small3, small3_t)
    out = _out_ffn(x.reshape(m, d), att.reshape(m, ATT_WIDTH), y.reshape(m, SSM_WIDTH), p["woa"], p["woy"],
                   p["norm_ffn"], p["wg"], p["wu"], p["wd"], p["norm_final"], tm=tm, th=FFN_TILE)
    return out.reshape(b, t, d), k4, v4, kin3, tail, h_fin


def kernel(x_prompt, x_sample, cache_k, cache_v, cache_kidx, state_conv, state_ssm, page_table, rel_bias,
           norm_attn, w_in, idx_k_norm, conv_w, conv_b, dt_bias, a_log, d_skip, ssm_norm, w_out, norm_ffn,
           w_gate, w_up, w_down, norm_final):
    depth = w_in.shape[0]
    assert depth == 1
    bp, s, d = x_prompt.shape
    bs, t, _ = x_sample.shape
    n_pages = page_table.shape[1]
    past = n_pages * PAGE_SIZE
    lyr = 0

    assert w_in.shape[2] == IN_WIDTH
    w_t = jnp.swapaxes(w_in[lyr], 0, 1).astype(BF16)
    row = lambda v: v.reshape(1, -1)
    p = dict(
        norm_attn=row(norm_attn[lyr]), w_t=w_t, idx_k_norm=row(idx_k_norm[lyr]),
        conv_w=conv_w[lyr], conv_b=conv_b[lyr], dt_bias=dt_bias[lyr], a_log=a_log[lyr], d_skip=d_skip[lyr],
        ssm_norm=ssm_norm[lyr],
        woa=w_out[lyr, :ATT_WIDTH].astype(BF16), woy=w_out[lyr, ATT_WIDTH:].astype(BF16),
        norm_ffn=row(norm_ffn[lyr]), wg=w_gate[lyr].astype(BF16), wu=w_up[lyr].astype(BF16),
        wd=w_down[lyr].astype(BF16), norm_final=row(norm_final),
    )

    topk_p = min(TOPK_MAX, s // 4)
    assert SSD_CHUNK == LANES and s % LANES == 0
    tiles_p = _bias_tiles(rel_bias, LANES, key_major=True)
    tiles_p = tiles_p - tiles_p[0:1]

    def att_prompt(big3, small3, small3_t, kin3, k4, v4, k16, vt16):
        return _attn_prompt(big3, small3_t, kin3, k16.reshape(bp, s, ATT_WIDTH), vt16, tiles_p,
                            topk=topk_p, group=CAUSAL_GROUP)

    zero_conv = jnp.zeros((bp, CONV_W - 1, CONV_DIM), F32)
    zero_ssm = jnp.zeros((bp, N_SSM_HEADS, SSM_HEAD_DIM, SSM_STATE), F32)
    ssm_params = (p["conv_w"], p["conv_b"], p["dt_bias"], p["a_log"], p["d_skip"], p["ssm_norm"])

    def ssd_prompt(big3, bc3, small3, small3_t):
        return _ssd(big3, bc3, small3, small3_t, zero_conv, zero_ssm, *ssm_params, l=SSD_CHUNK)

    yp, kp, vp, kip, cp, sp = _layer(x_prompt, att_prompt, ssd_prompt, SSD_CHUNK, p, kv_bf16=True)

    topk_s = min(TOPK_MAX, (past + t) // 4)
    tiles_s = jnp.repeat(_bias_tiles(rel_bias, t, key_major=False).reshape(3, N_ATT_HEADS * t, LANES),
                         N_ATT_HEADS, axis=-1)
    lane_head = jnp.arange(LANES * N_ATT_HEADS, dtype=jnp.int32) % N_ATT_HEADS
    row_head = jnp.arange(N_ATT_HEADS * t, dtype=jnp.int32) // t
    tiles_s = jnp.where(lane_head[None, :] == row_head[:, None], tiles_s, -jnp.inf)

    cache_kidx_t = jnp.swapaxes(cache_kidx, -1, -2)

    gs = math.gcd(bs, LANES // t)

    def att_sample(big3, small3, small3_t, kin3, k4, v4):
        sel = _select_sample(page_table, big3, small3, kin3, cache_kidx_t, topk=topk_s, gs=gs)
        return _attn_sample(page_table, big3, k4, v4, sel, tiles_s, cache_k, cache_v, gs=gs)

    def ssd_sample(big3, bc3, small3, small3_t):
        return _ssd_sample(big3, bc3, small3, small3_t, state_conv[lyr], state_ssm[lyr], *ssm_params, gs=gs)

    ys, ks, vs, kis, cs, ss = _layer(x_sample, att_sample, ssd_sample, gs * t, p, kv_bf16=False)

    st = lambda a: a[None]
    return (yp, ys, st(kp), st(vp), st(kip), st(cp), st(sp), st(ks), st(vs), st(kis), st(cs), st(ss))
```

```python
import functools
import math

import numpy as np
import jax
import jax.numpy as jnp
from jax import lax
from jax.experimental import pallas as pl
from jax.experimental.pallas import tpu as pltpu

N_ATT_HEADS = 8
ATT_HEAD_DIM = 128
ATT_WIDTH = N_ATT_HEADS * ATT_HEAD_DIM
N_IDX_HEADS = 16
IDX_DIM = 64
TOPK_MAX = 256
N_SSM_HEADS = 16
SSM_HEAD_DIM = 64
SSM_WIDTH = N_SSM_HEADS * SSM_HEAD_DIM
N_SSM_GROUPS = 2
SSM_STATE = 128
CONV_W = 4
CONV_DIM = SSM_WIDTH + 2 * N_SSM_GROUPS * SSM_STATE
SSD_CHUNK = 128
PAGE_SIZE = 128
N_BUCKETS = 32
MAX_DISTANCE = 128
NORM_EPS = 1e-6

LANES = 128
SUBLANES = 8
VMEM_LIMIT = 56 * 1024 * 1024
INPROJ_VMEM_LIMIT = 60000 * 1024

ROW_TILE = 512
FFN_TILE = 512
CAUSAL_GROUP = 2
SELECT_SEQS_PER_STEP = 8

COL_Q, COL_QI, COL_Z, COL_X = (i * ATT_WIDTH for i in range(4))
BIG_WIDTH = 4 * ATT_WIDTH
GROUP_WIDTH = N_SSM_GROUPS * SSM_STATE
BC_WIDTH = 2 * GROUP_WIDTH
SM_KI, SM_WI, SM_DT = 0, IDX_DIM, IDX_DIM + N_IDX_HEADS
SMALL_WIDTH = LANES
INPROJ_TN = 2 * ATT_WIDTH
INPROJ_STEPS = 3
IN_KI = 3 * ATT_WIDTH + N_IDX_HEADS * IDX_DIM
IN_Z = IN_KI + IDX_DIM + N_IDX_HEADS
IN_B = IN_Z + 2 * SSM_WIDTH
IN_WIDTH = IN_B + BC_WIDTH + N_SSM_HEADS

F32 = jnp.float32
BF16 = jnp.bfloat16

_INT_MIN = -(2 ** 31)
_INT_MAX = 2 ** 31 - 1
_KEY_NEG_INF = int(np.array([-np.inf], np.float32).view(np.int32)[0]) ^ 0x7FFFFFFF
_NEG_BIG = -1e30


def _dot(a, b):
    return jnp.dot(a, b, preferred_element_type=F32)


def _dot_nt(a, b):
    return lax.dot_general(a, b, (((1,), (1,)), ((), ())), preferred_element_type=F32)


def _dot_tn(a, b):
    return lax.dot_general(a, b, (((0,), (0,)), ((), ())), preferred_element_type=F32)


def _rms(x, g):
    return x * lax.rsqrt(jnp.mean(x * x, axis=-1, keepdims=True) + NORM_EPS) * g


def _silu(x):
    return x * (1.0 / (1.0 + jnp.exp(-x)))


def _inproj_body(x_ref, g_ref, wa_ref, wb_ref, wsa_ref, wsb_ref, gk_ref, big_ref, bc_ref, small_ref, kin_ref,
                 k4_ref, v4_ref, *rest, kv_bf16):
    u_scr = rest[-1]
    n = pl.program_id(1)
    rows = x_ref.shape[0]

    def head_rows(out_ref, cols):
        for h in range(N_ATT_HEADS):
            out_ref[pl.ds(h, rows, stride=N_ATT_HEADS), :] = cols[:, h * ATT_HEAD_DIM:(h + 1) * ATT_HEAD_DIM]

    @pl.when(n == 0)
    def _():
        ub = _rms(x_ref[...], g_ref[...]).astype(BF16)
        u_scr[...] = ub
        nar_a = _dot_nt(ub, wsa_ref[...])
        nar_b = _dot_nt(ub, wsb_ref[...])
        bc_ref[...] = nar_b[:, :BC_WIDTH]
        pad = jnp.zeros((ub.shape[0], SMALL_WIDTH - SM_DT - N_SSM_HEADS), F32)
        small_ref[...] = jnp.concatenate([nar_a, nar_b[:, BC_WIDTH:], pad], axis=1)
        kin_ref[...] = _rms(nar_a[:, :IDX_DIM], gk_ref[...])

    u = u_scr[...]
    half_a = _dot_nt(u, wa_ref[...])
    half_b = _dot_nt(u, wb_ref[...])

    @pl.when(n < INPROJ_STEPS - 1)
    def _():
        big_ref[:, :ATT_WIDTH] = half_a
        big_ref[:, ATT_WIDTH:] = half_b

    @pl.when(n == INPROJ_STEPS - 1)
    def _():
        head_rows(k4_ref, half_a)
        head_rows(v4_ref, half_b)
        if kv_bf16:
            rest[0][...] = half_a.astype(BF16)
            rest[1][0] = half_b.T.astype(BF16)


def _inproj(x2, g, w_t, gk, *, tm, seq_len=None):
    m, d = x2.shape
    kv_bf16 = seq_len is not None
    assert ATT_WIDTH == SSM_WIDTH == N_IDX_HEADS * IDX_DIM and IN_Z + 2 * SSM_WIDTH == IN_B
    assert IN_Z % (2 * SUBLANES) == 0 and IN_B % (2 * SUBLANES) == 0
    starts_a = (0, IN_Z, ATT_WIDTH)
    starts_b = (3 * ATT_WIDTH, IN_Z + SSM_WIDTH, 2 * ATT_WIDTH)

    def half_rows(starts):
        def index_map(i, j):
            row = jnp.where(j == 0, starts[0], jnp.where(j == 1, starts[1], starts[2]))
            return pl.multiple_of(row, 2 * SUBLANES), 0
        return pl.BlockSpec((pl.Element(ATT_WIDTH), pl.Element(d)), index_map)

    out_specs = [
        pl.BlockSpec((tm, INPROJ_TN), lambda i, j: (i, jnp.minimum(j, INPROJ_STEPS - 2))),
        pl.BlockSpec((tm, BC_WIDTH), lambda i, j: (i, 0)),
        pl.BlockSpec((tm, SMALL_WIDTH), lambda i, j: (i, 0)),
        pl.BlockSpec((tm, IDX_DIM), lambda i, j: (i, 0)),
        pl.BlockSpec((tm * N_ATT_HEADS, ATT_HEAD_DIM), lambda i, j: (i, 0)),
        pl.BlockSpec((tm * N_ATT_HEADS, ATT_HEAD_DIM), lambda i, j: (i, 0)),
    ]
    out_shape = [
        jax.ShapeDtypeStruct((m, BIG_WIDTH), F32),
        jax.ShapeDtypeStruct((m, BC_WIDTH), F32),
        jax.ShapeDtypeStruct((m, SMALL_WIDTH), F32),
        jax.ShapeDtypeStruct((m, IDX_DIM), F32),
        jax.ShapeDtypeStruct((m * N_ATT_HEADS, ATT_HEAD_DIM), F32),
        jax.ShapeDtypeStruct((m * N_ATT_HEADS, ATT_HEAD_DIM), F32),
    ]
    if kv_bf16:
        assert seq_len % tm == 0
        per_seq = seq_len // tm
        out_specs += [
            pl.BlockSpec((tm, ATT_WIDTH), lambda i, j: (i, 0)),
            pl.BlockSpec((1, ATT_WIDTH, tm), lambda i, j: (i // per_seq, 0, i % per_seq)),
        ]
        out_shape += [
            jax.ShapeDtypeStruct((m, ATT_WIDTH), BF16),
            jax.ShapeDtypeStruct((m // seq_len, ATT_WIDTH, seq_len), BF16),
        ]
    return pl.pallas_call(
        functools.partial(_inproj_body, kv_bf16=kv_bf16),
        grid=(m // tm, INPROJ_STEPS),
        in_specs=[
            pl.BlockSpec((tm, d), lambda i, j: (i, 0)),
            pl.BlockSpec((1, d), lambda i, j: (0, 0)),
            half_rows(starts_a),
            half_rows(starts_b),
            pl.BlockSpec((pl.Element(IN_Z - IN_KI), pl.Element(d)), lambda i, j: (IN_KI, 0),
                         pipeline_mode=pl.Buffered(1)),
            pl.BlockSpec((pl.Element(IN_WIDTH - IN_B), pl.Element(d)), lambda i, j: (IN_B, 0),
                         pipeline_mode=pl.Buffered(1)),
            pl.BlockSpec((1, IDX_DIM), lambda i, j: (0, 0)),
        ],
        out_specs=out_specs,
        out_shape=out_shape,
        scratch_shapes=[pltpu.VMEM((tm, d), BF16)],
        compiler_params=pltpu.CompilerParams(
            dimension_semantics=("parallel", "arbitrary"), vmem_limit_bytes=INPROJ_VMEM_LIMIT),
        name="inproj",
    )(x2, g, w_t, w_t, w_t, w_t, gk)


def _sortable_key(s):
    s = jnp.where(s == 0.0, 0.0, s)
    b = pltpu.bitcast(s, jnp.int32)
    return b ^ ((b >> 31) & 0x7FFFFFFF)


def _count_keys(key_ref, nblk, pred, key_axis):
    def body(kb, acc):
        return acc + jnp.where(pred(key_ref[kb], kb), 1.0, 0.0)

    acc = jnp.zeros(key_ref.shape[1:], F32)
    if isinstance(nblk, int):
        for kb in range(nblk):
            acc = body(kb, acc)
    else:
        acc = lax.fori_loop(0, nblk, body, acc)
    return jnp.sum(acc, axis=key_axis, keepdims=True)


_I16_MIN = -(2 ** 15)


def _pack_halves(key):
    r = key.shape[0] // 2
    a, b = key[:r], key[r:]
    hi = (a & jnp.int32(-65536)) | ((b >> 16) & 0xFFFF)
    lo = ((a ^ 0x8000) << 16) | ((b ^ 0x8000) & 0xFFFF)
    return hi, lo


def _count16(ref, nblk, cmp, cand):
    r = ref.shape[1]
    pair = (cand << 16) | (cand & 0xFFFF)
    cb = pltpu.bitcast(jnp.broadcast_to(pair, (r, LANES)), jnp.int16)
    acc = jnp.zeros((2 * r, LANES), jnp.int16)
    for kb in range(nblk):
        acc = acc + jnp.where(cmp(pltpu.bitcast(ref[kb], jnp.int16), cb), jnp.int16(1), jnp.int16(0))
    words = pltpu.bitcast(acc, jnp.int32)
    both = (words >> 16) + ((words << 16) >> 16)
    return jnp.sum(both.astype(F32), axis=0, keepdims=True)


def _kth_of_16bit(ref, nblk, need):
    ge = lambda k, c: k >= c
    res = jnp.where(_count16(ref, nblk, ge, jnp.zeros((1, LANES), jnp.int32)) >= need, 0, _I16_MIN).astype(jnp.int32)

    def body(it, res):
        cand = res | jnp.left_shift(jnp.int32(1), 14 - it)
        return jnp.where(_count16(ref, nblk, ge, cand) >= need, cand, res)

    return lax.fori_loop(0, 15, body, res)


def _kth_key_two_phase(hi_ref, lo_ref, nblk, topk):
    kf = float(topk)
    top = _kth_of_16bit(hi_ref, nblk, kf)
    n_above = _count16(hi_ref, nblk, lambda k, c: k > c, top)
    r = hi_ref.shape[1]
    top_b = pltpu.bitcast(jnp.broadcast_to((top << 16) | (top & 0xFFFF), (r, LANES)), jnp.int16)
    for kb in range(nblk):
        band = pltpu.bitcast(hi_ref[kb], jnp.int16) == top_b
        lo_ref[kb] = pltpu.bitcast(jnp.where(band, pltpu.bitcast(lo_ref[kb], jnp.int16), jnp.int16(_I16_MIN)),
                                   jnp.int32)
    low = _kth_of_16bit(lo_ref, nblk, kf - n_above)
    return (top << 16) | ((low ^ 0x8000) & 0xFFFF)


def _topk_select_params(key_ref, nblk, topk, key_axis, thr=None):
    shape = key_ref.shape[1:]
    qshape = tuple(1 if a == key_axis else n for a, n in enumerate(shape))
    blk = shape[key_axis]
    kf = float(topk)

    def count_ge(cand):
        cb = jnp.broadcast_to(cand, shape)
        return _count_keys(key_ref, nblk, lambda k, kb: k >= cb, key_axis)

    if thr is None:
        res = jnp.where(count_ge(jnp.zeros(qshape, jnp.int32)) >= kf, 0, _INT_MIN).astype(jnp.int32)

        def bit_body(it, res):
            cand = res | jnp.left_shift(jnp.int32(1), 30 - it)
            return jnp.where(count_ge(cand) >= kf, cand, res)

        thr = lax.fori_loop(0, 31, bit_body, res)
    thr_b = jnp.broadcast_to(thr, shape)
    n_gt = _count_keys(key_ref, nblk, lambda k, kb: k > thr_b, key_axis)
    n_ge = count_ge(thr)
    tied = jnp.logical_and(n_ge > kf, thr > _KEY_NEG_INF)
    any_tied = jnp.max(jnp.where(tied, 1.0, 0.0)) > 0.0
    need = kf - n_gt
    kpos = lax.broadcasted_iota(jnp.int32, shape, key_axis)

    def tie_break():
        def body(it, cut):
            cand = cut | jnp.left_shift(jnp.int32(1), 30 - it)
            cand_b = jnp.broadcast_to(cand, shape)
            n = _count_keys(key_ref, nblk,
                            lambda k, kb: jnp.logical_and(k == thr_b, kpos + kb * blk < cand_b), key_axis)
            return jnp.where(n <= need, cand, cut)

        return lax.fori_loop(0, 31, body, jnp.zeros(qshape, jnp.int32))

    cut = lax.cond(any_tied, tie_break, lambda: jnp.full(qshape, _INT_MAX, jnp.int32))
    return thr, cut


def _attn_prompt_body(q_ref, qi_ref, smt_ref, kin_ref, k_ref, vt_ref, bias_ref, o_ref,
                      key_scr, hi_scr, lo_scr, mask_scr, lg_scr, qi_scr, *, topk, group):
    i = pl.program_id(1)
    q0 = i * LANES
    blk = (LANES, LANES)
    scale = ATT_HEAD_DIM ** -0.5
    nq = key_scr.shape[0]

    w_t = smt_ref[0, SM_WI:SM_WI + N_IDX_HEADS, :]
    qi_all = qi_ref[0]
    for h in range(N_IDX_HEADS):
        qi_scr[h * LANES:(h + 1) * LANES, :] = qi_all[:, h * IDX_DIM:(h + 1) * IDX_DIM].astype(BF16)
    qi = qi_scr[...]
    kpos = lax.broadcasted_iota(jnp.int32, blk, 0)
    qpos = lax.broadcasted_iota(jnp.int32, blk, 1) + q0
    qb = q_ref[0].astype(BF16)

    def tile(nb, first_near):
        if nb * LANES <= topk:
            for kb in range(nb):
                mask_scr[kb * LANES:(kb + 1) * LANES, :] = jnp.where(kpos + kb * LANES <= qpos, 0.0, -jnp.inf)
        else:
            select_mask(nb, first_near)
        attend(nb, first_near)

    def select_mask(nb, first_near):
        for kb in range(nb):
            kin = kin_ref[0, kb * LANES:(kb + 1) * LANES, :].astype(BF16)
            s = _dot_nt(kin, qi)
            score = jnp.zeros(blk, F32)
            for h in range(N_IDX_HEADS):
                score = score + jnp.maximum(s[:, h * LANES:(h + 1) * LANES], 0.0) * w_t[h:h + 1, :]
            if kb >= first_near:
                score = jnp.where(kpos + kb * LANES <= qpos, score, -jnp.inf)
            key = _sortable_key(score)
            key_scr[kb] = key
            hi_scr[kb], lo_scr[kb] = _pack_halves(key)

        thr = _kth_key_two_phase(hi_scr, lo_scr, nb, topk)
        thr, cut = _topk_select_params(key_scr, nb, topk, key_axis=0, thr=thr)
        thr_b = jnp.broadcast_to(thr, blk)
        cut_b = jnp.broadcast_to(cut, blk)
        for kb in range(nb):
            kp = kpos + kb * LANES
            key = key_scr[kb]
            valid = jnp.logical_or(key > thr_b, jnp.logical_and(key == thr_b, kp < cut_b))
            if kb >= first_near:
                valid = jnp.logical_and(valid, kp <= qpos)
            mask_scr[kb * LANES:(kb + 1) * LANES, :] = jnp.where(valid, 0.0, -jnp.inf)

    def attend(nb, first_near):
        nk = nb * LANES
        for h in range(N_ATT_HEADS):
            hs = slice(h * ATT_HEAD_DIM, (h + 1) * ATT_HEAD_DIM)
            lg = _dot_nt(k_ref[0, 0:nk, hs], qb[:, hs]) * scale + mask_scr[0:nk, :]
            for kb in range(first_near, nb):
                tix = 2 - jnp.clip(i - kb, 0, 2)
                lg_scr[kb * LANES:(kb + 1) * LANES, :] = lg[kb * LANES:(kb + 1) * LANES, :] + bias_ref[tix, h]
            if first_near > 0:
                lg_scr[0:first_near * LANES, :] = lg[0:first_near * LANES, :]
            lg = lg_scr[0:nk, :]
            m = jnp.max(lg, axis=0, keepdims=True)
            p = jnp.exp(lg - m)
            l = jnp.sum(p, axis=0, keepdims=True)
            out_t = _dot(vt_ref[0, hs, 0:nk], p.astype(BF16)) / l
            o_ref[0, :, hs] = out_t.T.astype(o_ref.dtype)

    ngroups = -(-nq // group)
    for c in range(ngroups):
        nb = min((c + 1) * group, nq)

        @pl.when(i // group == c)
        def _(nb=nb, c=c):
            tile(nb, max(c * group - 1, 0))


def _attn_prompt(big3, small3_t, kin3, k16, vt16, bias_tiles_t, *, topk, group):
    b, s, _ = big3.shape
    nq = s // LANES
    body = functools.partial(_attn_prompt_body, topk=topk, group=group)
    return pl.pallas_call(
        body,
        grid=(b, nq),
        in_specs=[
            pl.BlockSpec((1, LANES, ATT_WIDTH), lambda bi, i: (bi, i, COL_Q // ATT_WIDTH)),
            pl.BlockSpec((1, LANES, N_IDX_HEADS * IDX_DIM), lambda bi, i: (bi, i, COL_QI // (N_IDX_HEADS * IDX_DIM))),
            pl.BlockSpec((1, SMALL_WIDTH, LANES), lambda bi, i: (bi * nq + i, 0, 0)),
            pl.BlockSpec((1, s, IDX_DIM), lambda bi, i: (bi, 0, 0)),
            pl.BlockSpec((1, s, ATT_WIDTH), lambda bi, i: (bi, 0, 0)),
            pl.BlockSpec((1, ATT_WIDTH, s), lambda bi, i: (bi, 0, 0)),
            pl.BlockSpec(bias_tiles_t.shape, lambda bi, i: (0, 0, 0, 0)),
        ],
        out_specs=pl.BlockSpec((1, LANES, ATT_WIDTH), lambda bi, i: (bi, i, 0)),
        out_shape=jax.ShapeDtypeStruct((b, s, ATT_WIDTH), BF16),
        scratch_shapes=[
            pltpu.VMEM((nq, LANES, LANES), jnp.int32),
            pltpu.VMEM((nq, LANES // 2, LANES), jnp.int32),
            pltpu.VMEM((nq, LANES // 2, LANES), jnp.int32),
            pltpu.VMEM((s, LANES), F32),
            pltpu.VMEM((s, LANES), F32),
            pltpu.VMEM((N_IDX_HEADS * LANES, IDX_DIM), BF16),
        ],
        compiler_params=pltpu.CompilerParams(
            dimension_semantics=("parallel", "arbitrary"), vmem_limit_bytes=VMEM_LIMIT),
        name="attn_prompt",
    )(big3, big3, small3_t, kin3, k16, vt16, bias_tiles_t)


def _select_sample_body(pt_ref, qi_ref, sm_ref, kinew_ref, *rest, n_pages, t, topk, ss):
    kidx_refs = rest[:ss * n_pages]
    sel_ref = rest[ss * n_pages]
    key_scr, wb_scr = rest[ss * n_pages + 1:]
    nblk = n_pages + 1
    j = pl.program_id(1)
    trow = lax.broadcasted_iota(jnp.int32, (t, LANES), 0)
    lane_t = lax.broadcasted_iota(jnp.int32, (t, LANES), 1)

    def scores(s):
        return jnp.sum(jnp.maximum(s.reshape(N_IDX_HEADS, t, LANES), 0.0) * wb_scr[...], axis=0)

    for u in range(ss):
        r0 = pl.multiple_of((j * ss + u) * t, t)
        wi = sm_ref[u, :, SM_WI:SM_WI + N_IDX_HEADS]
        for h in range(N_IDX_HEADS):
            wb_scr[h] = jnp.broadcast_to(wi[:, h:h + 1], (t, LANES))
        qi_all = qi_ref[u]
        qi = jnp.concatenate([qi_all[:, h * IDX_DIM:(h + 1) * IDX_DIM] for h in range(N_IDX_HEADS)],
                             axis=0).astype(BF16)
        for p in range(n_pages):
            kin_t = kidx_refs[u * n_pages + p][0, 0].astype(BF16)
            key_scr[p, pl.ds(r0, t), :] = _sortable_key(scores(_dot(qi, kin_t)))
        kin_new = jnp.concatenate([kinew_ref[u], jnp.zeros((LANES - t, IDX_DIM), F32)], axis=0)
        s_new = jnp.where(lane_t <= trow, scores(_dot_nt(qi, kin_new.astype(BF16))), -jnp.inf)
        key_scr[n_pages, pl.ds(r0, t), :] = _sortable_key(s_new)

    @pl.when(j == pl.num_programs(1) - 1)
    def _():
        rows = key_scr.shape[1]
        thr, cut = _topk_select_params(key_scr, nblk, topk, key_axis=1)
        thr_b = jnp.broadcast_to(thr, (rows, LANES))
        cut_b = jnp.broadcast_to(cut, (rows, LANES))
        lane = lax.broadcasted_iota(jnp.int32, (rows, LANES), 1)
        tq = lax.broadcasted_iota(jnp.int32, (rows, LANES), 0) & (t - 1)
        for p in range(nblk):
            key = key_scr[p]
            sel = jnp.logical_or(key > thr_b, jnp.logical_and(key == thr_b, lane + p * LANES < cut_b))
            if p == n_pages:
                sel = jnp.logical_and(sel, lane <= tq)
            sel_ref[0, p] = jnp.where(sel, 1.0, 0.0)


def _select_sample(page_table, big3, small3, kin3, cache_kidx_t, *, topk, gs):
    b, t, _ = small3.shape
    n_pages = page_table.shape[1]
    ss = math.gcd(gs, SELECT_SEQS_PER_STEP)
    steps = gs // ss
    assert b % gs == 0 and t & (t - 1) == 0
    body = functools.partial(_select_sample_body, n_pages=n_pages, t=t, topk=topk, ss=ss)
    blk = lambda g, j: g * steps + j
    qi_width = N_IDX_HEADS * IDX_DIM
    in_specs = [
        pl.BlockSpec((ss, t, qi_width), lambda g, j, pt: (blk(g, j), 0, COL_QI // qi_width)),
        pl.BlockSpec((ss, t, SMALL_WIDTH), lambda g, j, pt: (blk(g, j), 0, 0)),
        pl.BlockSpec((ss, t, IDX_DIM), lambda g, j, pt: (blk(g, j), 0, 0)),
    ]
    in_specs += [pl.BlockSpec((1, 1, IDX_DIM, PAGE_SIZE),
                              lambda g, j, pt, u=u, p=p: (0, pt[blk(g, j) * ss + u, p], 0, 0))
                 for u in range(ss) for p in range(n_pages)]
    sel_shape = (b // gs, n_pages + 1, gs * t, LANES)
    return pl.pallas_call(
        body,
        grid_spec=pltpu.PrefetchScalarGridSpec(
            num_scalar_prefetch=1,
            grid=(b // gs, steps),
            in_specs=in_specs,
            out_specs=pl.BlockSpec((1,) + sel_shape[1:], lambda g, j, pt: (g, 0, 0, 0)),
            scratch_shapes=[
                pltpu.VMEM(sel_shape[1:], jnp.int32),
                pltpu.VMEM((N_IDX_HEADS, t, LANES), F32),
            ],
        ),
        out_shape=jax.ShapeDtypeStruct(sel_shape, F32),
        compiler_params=pltpu.CompilerParams(
            dimension_semantics=("parallel", "arbitrary"), vmem_limit_bytes=VMEM_LIMIT),
        name="select_sample",
    )(page_table, big3, small3, kin3, *([cache_kidx_t] * (ss * n_pages)))


def _attn_sample_body(pt_ref, q_ref, knew_ref, vnew_ref, sel_ref, bias_ref, *rest, n_pages, t):
    k_refs = rest[:n_pages]
    v_refs = rest[n_pages:2 * n_pages]
    o_ref = rest[2 * n_pages]
    (lg_scr,) = rest[2 * n_pages + 1:]
    nblk = n_pages + 1
    rows = N_ATT_HEADS * t
    flat = PAGE_SIZE * N_ATT_HEADS
    scale = ATT_HEAD_DIM ** -0.5

    q = q_ref[0]
    q64 = jnp.concatenate([q[:, h * ATT_HEAD_DIM:(h + 1) * ATT_HEAD_DIM] for h in range(N_ATT_HEADS)],
                          axis=0).astype(BF16)
    er = lax.broadcasted_iota(jnp.int32, (LANES, flat), 0)
    ec = lax.broadcasted_iota(jnp.int32, (LANES, flat), 1)
    expand = jnp.where((ec >> (N_ATT_HEADS.bit_length() - 1)) == er, 1.0, 0.0).astype(BF16)

    def flat_kv(refs, p):
        if p < n_pages:
            return refs[p][0, 0].reshape(flat, ATT_HEAD_DIM).astype(BF16), flat
        new = refs[p][0].reshape(t * N_ATT_HEADS, ATT_HEAD_DIM)
        pad = jnp.zeros((LANES - t * N_ATT_HEADS, ATT_HEAD_DIM), F32)
        return jnp.concatenate([new, pad], axis=0).astype(BF16), LANES

    k_all = list(k_refs) + [knew_ref]
    v_all = list(v_refs) + [vnew_ref]
    m = jnp.full((rows, 1), _NEG_BIG, F32)
    for p in range(nblk):
        kflat, width = flat_kv(k_all, p)
        selx = _dot(sel_ref[0, p].astype(BF16), expand[:, :width])
        unsel = (selx - 1.0) * (-_NEG_BIG)
        tix = 2 - min(n_pages - p, 2)
        lg = _dot_nt(q64, kflat) * scale + bias_ref[tix, :, :width] + jnp.concatenate([unsel] * N_ATT_HEADS, axis=0)
        lg_scr[p, :, :width] = lg
        m = jnp.maximum(m, jnp.max(lg, axis=-1, keepdims=True))

    l = jnp.zeros((rows, 1), F32)
    acc = jnp.zeros((rows, ATT_HEAD_DIM), F32)
    for p in range(nblk):
        vflat, width = flat_kv(v_all, p)
        pr = jnp.exp(lg_scr[p, :, :width] - m)
        l = l + jnp.sum(pr, axis=-1, keepdims=True)
        acc = acc + _dot(pr.astype(BF16), vflat)
    out = acc / l
    for h in range(N_ATT_HEADS):
        o_ref[0, :, h * ATT_HEAD_DIM:(h + 1) * ATT_HEAD_DIM] = out[h * t:(h + 1) * t, :].astype(o_ref.dtype)


def _attn_sample(page_table, big3, k_new, v_new, sel, bias_rows, cache_k, cache_v, *, gs):
    b, t, _ = big3.shape
    n_pages = page_table.shape[1]
    assert t == SUBLANES and N_ATT_HEADS & (N_ATT_HEADS - 1) == 0 and t * N_ATT_HEADS <= LANES
    body = functools.partial(_attn_sample_body, n_pages=n_pages, t=t)
    kv_block = (1, 1, PAGE_SIZE, N_ATT_HEADS, ATT_HEAD_DIM)
    new_block = (1, t, N_ATT_HEADS, ATT_HEAD_DIM)

    in_specs = [
        pl.BlockSpec((1, t, ATT_WIDTH), lambda bi, pt: (bi, 0, COL_Q // ATT_WIDTH)),
        pl.BlockSpec(new_block, lambda bi, pt: (bi, 0, 0, 0)),
        pl.BlockSpec(new_block, lambda bi, pt: (bi, 0, 0, 0)),
        pl.BlockSpec((1, n_pages + 1, t, LANES), lambda bi, pt: (bi // gs, 0, bi % gs, 0)),
        pl.BlockSpec(bias_rows.shape, lambda bi, pt: (0, 0, 0)),
    ]
    in_specs += [pl.BlockSpec(kv_block, lambda bi, pt, p=p: (0, pt[bi, p], 0, 0, 0)) for p in range(n_pages)] * 2
    rows = N_ATT_HEADS * t
    return pl.pallas_call(
        body,
        grid_spec=pltpu.PrefetchScalarGridSpec(
            num_scalar_prefetch=1,
            grid=(b,),
            in_specs=in_specs,
            out_specs=pl.BlockSpec((1, t, ATT_WIDTH), lambda bi, pt: (bi, 0, 0)),
            scratch_shapes=[pltpu.VMEM((n_pages + 1, rows, PAGE_SIZE * N_ATT_HEADS), F32)],
        ),
        out_shape=jax.ShapeDtypeStruct((b, t, ATT_WIDTH), BF16),
        compiler_params=pltpu.CompilerParams(
            dimension_semantics=("arbitrary",), vmem_limit_bytes=VMEM_LIMIT),
        name="attn_sample",
    )(page_table, big3, k_new, v_new, sel, bias_rows, *([cache_k] * n_pages), *([cache_v] * n_pages))


def _split3(a):
    a1 = a.astype(BF16)
    r1 = a - a1.astype(F32)
    a2 = r1.astype(BF16)
    a3 = (r1 - a2.astype(F32)).astype(BF16)
    return a1, a2, a3


def _softplus(v):
    return jnp.maximum(v, 0.0) + jnp.log1p(jnp.exp(-jnp.abs(v)))


def _head_expander():
    r = lax.broadcasted_iota(jnp.int32, (SMALL_WIDTH, SSM_WIDTH), 0)
    c = lax.broadcasted_iota(jnp.int32, (SMALL_WIDTH, SSM_WIDTH), 1)
    return jnp.where(r - SM_DT == c // SSM_HEAD_DIM, 1.0, 0.0).astype(BF16)


def _ssd_body(x_ref, bm_ref, cm_ref, z_ref, sm_ref, smt_ref, pre_ref, h0_ref,
              cw_ref, cb_ref, dtb_ref, dtbt_ref, alog_ref, alogt_ref, dskipx_ref, gn_ref,
              y_ref, tail_ref, hout_ref, xp_scr, ht_scr, y_scr, *, l):
    c = pl.program_id(1)
    nc = pl.num_programs(1)
    off = SUBLANES
    heads_per_group = N_SSM_HEADS // N_SSM_GROUPS
    gcols = heads_per_group * SSM_HEAD_DIM

    @pl.when(c == 0)
    def _():
        xp_scr[off - (CONV_W - 1):off, :] = pre_ref[0]
        for h in range(N_SSM_HEADS):
            g, hh = divmod(h, heads_per_group)
            ht_scr[g, :, hh * SSM_HEAD_DIM:(hh + 1) * SSM_HEAD_DIM] = h0_ref[0, h].T

    xp_scr[off:off + l, 0:SSM_WIDTH] = x_ref[0]
    xp_scr[off:off + l, SSM_WIDTH:SSM_WIDTH + GROUP_WIDTH] = bm_ref[0]
    xp_scr[off:off + l, SSM_WIDTH + GROUP_WIDTH:CONV_DIM] = cm_ref[0]

    conv = cb_ref[...]
    for j in range(CONV_W):
        s0 = off - (CONV_W - 1) + j
        conv = conv + xp_scr[s0:s0 + l, :] * cw_ref[j:j + 1, :]
    xc = _silu(conv)
    tail = xp_scr[off + l - (CONV_W - 1):off + l, :]
    tail_ref[0] = tail
    xp_scr[off - (CONV_W - 1):off, :] = tail

    xs = xc[:, :SSM_WIDTH]
    bc = [xc[:, SSM_WIDTH + g * SSM_STATE:SSM_WIDTH + (g + 1) * SSM_STATE].astype(BF16)
          for g in range(N_SSM_GROUPS)]
    cc = [xc[:, SSM_WIDTH + GROUP_WIDTH + g * SSM_STATE:SSM_WIDTH + GROUP_WIDTH + (g + 1) * SSM_STATE]
          .astype(BF16) for g in range(N_SSM_GROUPS)]

    dt = _softplus(sm_ref[0] + dtb_ref[...])
    a = dt * (-jnp.exp(alog_ref[...]))
    dtt = _softplus(smt_ref[0, SM_DT:SM_DT + N_SSM_HEADS, :] + dtbt_ref[...])
    at = dtt * (-jnp.exp(alogt_ref[...]))

    ri = lax.broadcasted_iota(jnp.int32, (l, l), 0)
    ci = lax.broadcasted_iota(jnp.int32, (l, l), 1)
    causal = ri >= ci
    tri = jnp.where(causal, 1.0, 0.0).astype(BF16)
    tri_u = jnp.where(ri <= ci, 1.0, 0.0).astype(BF16)
    acum = sum(_dot(tri, piece) for piece in _split3(a))
    acum_t = sum(_dot(piece, tri_u) for piece in _split3(at))
    last = acum[l - 1:l, :]
    wts = jnp.exp(last - acum) * dt
    expand = _head_expander()
    acum_x = sum(_dot(piece, expand) for piece in _split3(acum))
    wts_x = sum(_dot(piece, expand) for piece in _split3(wts))
    exp_last_x = jnp.exp(acum_x[l - 1:l, :])

    cb = [_dot_nt(cc[g], bc[g]) for g in range(N_SSM_GROUPS)]
    for h in range(N_SSM_HEADS):
        g = h // heads_per_group
        hs = slice(h * SSM_HEAD_DIM, (h + 1) * SSM_HEAD_DIM)
        col = acum[:, SM_DT + h:SM_DT + h + 1]
        decay = jnp.exp(jnp.where(causal, col - acum_t[h:h + 1, :], -jnp.inf))
        mmat = cb[g] * decay * dtt[h:h + 1, :]
        y_scr[:, hs] = _dot(mmat.astype(BF16), xs[:, hs].astype(BF16))

    xw = (xs * wts_x).astype(BF16)
    y_inter = []
    for g in range(N_SSM_GROUPS):
        gsl = slice(g * gcols, (g + 1) * gcols)
        ht = ht_scr[g]
        y_inter.append(_dot(cc[g], ht.astype(BF16)))
        ht_scr[g] = ht * exp_last_x[:, gsl] + _dot_tn(bc[g], xw[:, gsl])
    y = y_scr[...] + jnp.concatenate(y_inter, axis=1) * jnp.exp(acum_x) + dskipx_ref[...] * xs
    y_ref[0] = _rms(y * _silu(z_ref[0]), gn_ref[...]).astype(y_ref.dtype)

    @pl.when(c == nc - 1)
    def _():
        for h in range(N_SSM_HEADS):
            g, hh = divmod(h, heads_per_group)
            hout_ref[0, h] = ht_scr[g, :, hh * SSM_HEAD_DIM:(hh + 1) * SSM_HEAD_DIM].T


def _pad_lanes(v):
    return jnp.zeros((1, SMALL_WIDTH), F32).at[0, SM_DT:SM_DT + N_SSM_HEADS].set(v)


def _ssd(big3, bc3, small3, small3_t, prefix, h0, conv_w, conv_b, dt_bias, a_log, d_skip, ssm_norm, *, l):
    b, t, _ = big3.shape
    nc = t // l
    body = functools.partial(_ssd_body, l=l)
    row = lambda v: v.reshape(1, -1)
    colv = lambda v: v.reshape(-1, 1)
    const2 = lambda shape: pl.BlockSpec(shape, lambda bi, c: (0, 0))
    return pl.pallas_call(
        body,
        grid=(b, nc),
        in_specs=[
            pl.BlockSpec((1, l, SSM_WIDTH), lambda bi, c: (bi, c, COL_X // SSM_WIDTH)),
            pl.BlockSpec((1, l, GROUP_WIDTH), lambda bi, c: (bi, c, 0)),
            pl.BlockSpec((1, l, GROUP_WIDTH), lambda bi, c: (bi, c, 1)),
            pl.BlockSpec((1, l, SSM_WIDTH), lambda bi, c: (bi, c, COL_Z // SSM_WIDTH)),
            pl.BlockSpec((1, l, SMALL_WIDTH), lambda bi, c: (bi, c, 0)),
            pl.BlockSpec((1, SMALL_WIDTH, l), lambda bi, c: (bi * nc + c, 0, 0)),
            pl.BlockSpec((1, CONV_W - 1, CONV_DIM), lambda bi, c: (bi, 0, 0)),
            pl.BlockSpec((1, N_SSM_HEADS, SSM_HEAD_DIM, SSM_STATE), lambda bi, c: (bi, 0, 0, 0)),
            const2((CONV_W, CONV_DIM)),
            const2((1, CONV_DIM)),
            const2((1, SMALL_WIDTH)),
            const2((N_SSM_HEADS, 1)),
            const2((1, SMALL_WIDTH)),
            const2((N_SSM_HEADS, 1)),
            const2((1, SSM_WIDTH)),
            const2((1, SSM_WIDTH)),
        ],
        out_specs=[
            pl.BlockSpec((1, l, SSM_WIDTH), lambda bi, c: (bi, c, 0)),
            pl.BlockSpec((1, CONV_W - 1, CONV_DIM), lambda bi, c: (bi, 0, 0)),
            pl.BlockSpec((1, N_SSM_HEADS, SSM_HEAD_DIM, SSM_STATE), lambda bi, c: (bi, 0, 0, 0)),
        ],
        out_shape=[
            jax.ShapeDtypeStruct((b, t, SSM_WIDTH), BF16),
            jax.ShapeDtypeStruct((b, CONV_W - 1, CONV_DIM), F32),
            jax.ShapeDtypeStruct((b, N_SSM_HEADS, SSM_HEAD_DIM, SSM_STATE), F32),
        ],
        scratch_shapes=[
            pltpu.VMEM((SUBLANES + l, CONV_DIM), F32),
            pltpu.VMEM((N_SSM_GROUPS, SSM_STATE, SSM_WIDTH // N_SSM_GROUPS), F32),
            pltpu.VMEM((l, SSM_WIDTH), F32),
        ],
        compiler_params=pltpu.CompilerParams(
            dimension_semantics=("parallel", "arbitrary"), vmem_limit_bytes=VMEM_LIMIT),
        name="ssd",
    )(big3, bc3, bc3, big3, small3, small3_t, prefix, h0,
      conv_w, row(conv_b), _pad_lanes(dt_bias), colv(dt_bias), _pad_lanes(a_log), colv(a_log),
      row(jnp.repeat(d_skip, SSM_HEAD_DIM)), row(ssm_norm))


def _ssd_sample_body(x_ref, bm_ref, cm_ref, z_ref, sm_ref, smt_ref, pre_ref, h0_ref,
                     cw_ref, cb_ref, dtb_ref, dtbt_ref, alog_ref, alogt_ref, dskip_ref, gn_ref,
                     y_ref, tail_ref, hout_ref, xp_scr, xc_scr, yi_scr, xw_scr, y_scr, *, gs, t):
    rows = gs * t
    off = SUBLANES
    tshift = t.bit_length() - 1
    gw = N_SSM_GROUPS * SSM_STATE
    heads_per_group = N_SSM_HEADS // N_SSM_GROUPS
    gcols = heads_per_group * SSM_HEAD_DIM

    for s in range(gs):
        xp_scr[s, off - (CONV_W - 1):off, :] = pre_ref[s]
        xp_scr[s, off:off + t, 0:SSM_WIDTH] = x_ref[s]
        xp_scr[s, off:off + t, SSM_WIDTH:SSM_WIDTH + gw] = bm_ref[s]
        xp_scr[s, off:off + t, SSM_WIDTH + gw:CONV_DIM] = cm_ref[s]
        conv = cb_ref[...]
        for j in range(CONV_W):
            s0 = off - (CONV_W - 1) + j
            conv = conv + xp_scr[s, s0:s0 + t, :] * cw_ref[j:j + 1, :]
        xc_scr[s * t:(s + 1) * t, :] = _silu(conv)
        tail_ref[s] = xp_scr[s, off + t - (CONV_W - 1):off + t, :]

    xc = xc_scr[...]
    xs = xc[:, :SSM_WIDTH]
    bcf = [xc[:, SSM_WIDTH + g * SSM_STATE:SSM_WIDTH + (g + 1) * SSM_STATE] for g in range(N_SSM_GROUPS)]
    ccf = [xc[:, SSM_WIDTH + gw + g * SSM_STATE:SSM_WIDTH + gw + (g + 1) * SSM_STATE] for g in range(N_SSM_GROUPS)]

    dt = _softplus(sm_ref[...].reshape(rows, SMALL_WIDTH)[:, SM_DT:SM_DT + N_SSM_HEADS] + dtb_ref[...])
    dtt = _softplus(smt_ref[0, SM_DT:SM_DT + N_SSM_HEADS, :] + dtbt_ref[...])
    a = dt * (-jnp.exp(alog_ref[...]))
    at = dtt * (-jnp.exp(alogt_ref[...]))

    ri = lax.broadcasted_iota(jnp.int32, (rows, rows), 0)
    ci = lax.broadcasted_iota(jnp.int32, (rows, rows), 1)
    same = (ri >> tshift) == (ci >> tshift)
    causal = jnp.logical_and(same, ri >= ci)
    tri = jnp.where(causal, 1.0, 0.0).astype(BF16)
    tri_u = jnp.where(jnp.logical_and(same, ri <= ci), 1.0, 0.0).astype(BF16)
    pick_last = jnp.where(ci == (((ri >> tshift) << tshift) + (t - 1)), 1.0, 0.0).astype(BF16)
    acum = sum(_dot(tri, piece) for piece in _split3(a))
    acum_t = sum(_dot(piece, tri_u) for piece in _split3(at))
    last = sum(_dot(pick_last, piece) for piece in _split3(acum))
    wts = jnp.exp(last - acum) * dt
    exp_last = jnp.exp(last)

    for s in range(gs):
        rs = slice(s * t, (s + 1) * t)
        for g in range(N_SSM_GROUPS):
            h0g = h0_ref[s, g * heads_per_group:(g + 1) * heads_per_group].reshape(gcols, SSM_STATE)
            yi_scr[rs, g * gcols:(g + 1) * gcols] = _dot_nt(ccf[g][rs, :], h0g)

    cb = [_dot_nt(ccf[g].astype(BF16), bcf[g].astype(BF16)) for g in range(N_SSM_GROUPS)]
    for h in range(N_SSM_HEADS):
        g = h // heads_per_group
        hs = slice(h * SSM_HEAD_DIM, (h + 1) * SSM_HEAD_DIM)
        col = acum[:, h:h + 1]
        decay = jnp.exp(jnp.where(causal, col - acum_t[h:h + 1, :], -jnp.inf))
        mmat = cb[g] * decay * dtt[h:h + 1, :]
        xh = xs[:, hs]
        y_intra = _dot(mmat.astype(BF16), xh.astype(BF16))
        y_scr[:, hs] = y_intra + yi_scr[:, hs] * jnp.exp(col) + dskip_ref[h] * xh
        xw_scr[:, hs] = xh * wts[:, h:h + 1]

    out = _rms(y_scr[...] * _silu(z_ref[...].reshape(rows, SSM_WIDTH)), gn_ref[...])
    for s in range(gs):
        y_ref[s] = out[s * t:(s + 1) * t, :].astype(y_ref.dtype)

    for s in range(gs):
        rs = slice(s * t, (s + 1) * t)
        for g in range(N_SSM_GROUPS):
            upd = _dot_tn(xw_scr[rs, g * gcols:(g + 1) * gcols], bcf[g][rs, :])
            for hh in range(heads_per_group):
                h = g * heads_per_group + hh
                scale = jnp.broadcast_to(exp_last[s * t:s * t + 1, h:h + 1], (SSM_HEAD_DIM, SSM_STATE))
                hout_ref[s, h] = h0_ref[s, h] * scale + upd[hh * SSM_HEAD_DIM:(hh + 1) * SSM_HEAD_DIM, :]


def _ssd_sample(big3, bc3, small3, small3_t, prefix, h0, conv_w, conv_b, dt_bias, a_log, d_skip, ssm_norm, *, gs):
    b, t, _ = big3.shape
    assert b % gs == 0 and t & (t - 1) == 0 and t == SUBLANES
    rows = gs * t
    body = functools.partial(_ssd_sample_body, gs=gs, t=t)
    gw = N_SSM_GROUPS * SSM_STATE
    row = lambda v: v.reshape(1, -1)
    colv = lambda v: v.reshape(-1, 1)
    const2 = lambda shape: pl.BlockSpec(shape, lambda gi: (0, 0))
    state_block = (gs, N_SSM_HEADS, SSM_HEAD_DIM, SSM_STATE)
    return pl.pallas_call(
        body,
        grid=(b // gs,),
        in_specs=[
            pl.BlockSpec((gs, t, SSM_WIDTH), lambda gi: (gi, 0, COL_X // SSM_WIDTH)),
            pl.BlockSpec((gs, t, gw), lambda gi: (gi, 0, 0)),
            pl.BlockSpec((gs, t, gw), lambda gi: (gi, 0, 1)),
            pl.BlockSpec((gs, t, SSM_WIDTH), lambda gi: (gi, 0, COL_Z // SSM_WIDTH)),
            pl.BlockSpec((gs, t, SMALL_WIDTH), lambda gi: (gi, 0, 0)),
            pl.BlockSpec((1, SMALL_WIDTH, rows), lambda gi: (gi, 0, 0)),
            pl.BlockSpec((gs, CONV_W - 1, CONV_DIM), lambda gi: (gi, 0, 0)),
            pl.BlockSpec(state_block, lambda gi: (gi, 0, 0, 0)),
            const2((CONV_W, CONV_DIM)),
            const2((1, CONV_DIM)),
            const2((1, N_SSM_HEADS)),
            const2((N_SSM_HEADS, 1)),
            const2((1, N_SSM_HEADS)),
            const2((N_SSM_HEADS, 1)),
            pl.BlockSpec(memory_space=pltpu.SMEM),
            const2((1, SSM_WIDTH)),
        ],
        out_specs=[
            pl.BlockSpec((gs, t, SSM_WIDTH), lambda gi: (gi, 0, 0)),
            pl.BlockSpec((gs, CONV_W - 1, CONV_DIM), lambda gi: (gi, 0, 0)),
            pl.BlockSpec(state_block, lambda gi: (gi, 0, 0, 0)),
        ],
        out_shape=[
            jax.ShapeDtypeStruct((b, t, SSM_WIDTH), BF16),
            jax.ShapeDtypeStruct((b, CONV_W - 1, CONV_DIM), F32),
            jax.ShapeDtypeStruct((b, N_SSM_HEADS, SSM_HEAD_DIM, SSM_STATE), F32),
        ],
        scratch_shapes=[
            pltpu.VMEM((gs, SUBLANES + t, CONV_DIM), F32),
            pltpu.VMEM((rows, CONV_DIM), F32),
            pltpu.VMEM((rows, SSM_WIDTH), F32),
            pltpu.VMEM((rows, SSM_WIDTH), F32),
            pltpu.VMEM((rows, SSM_WIDTH), F32),
        ],
        compiler_params=pltpu.CompilerParams(
            dimension_semantics=("parallel",), vmem_limit_bytes=VMEM_LIMIT),
        name="ssd_sample",
    )(big3, bc3, bc3, big3, small3, small3_t, prefix, h0,
      conv_w, row(conv_b), row(dt_bias), colv(dt_bias), row(a_log), colv(a_log), d_skip, row(ssm_norm))


def _ffn_body(x_ref, att_ref, y_ref, woa_ref, woy_ref, gf_ref, wg_ref, wu_ref, wd_ref, gl_ref,
              o_ref, f_scr):
    j = pl.program_id(1)

    @pl.when(j == 0)
    def _():
        h = x_ref[...] + _dot(att_ref[...], woa_ref[...]) + _dot(y_ref[...], woy_ref[...])
        o_ref[...] = h
        f_scr[...] = _rms(h, gf_ref[...]).astype(BF16)

    f = f_scr[...]
    act = _silu(_dot(f, wg_ref[...])) * _dot(f, wu_ref[...])
    o_ref[...] += _dot(act.astype(BF16), wd_ref[...])

    @pl.when(j == pl.num_programs(1) - 1)
    def _():
        o_ref[...] = _rms(o_ref[...], gl_ref[...])


def _out_ffn(x2, att2, y2, woa, woy, gf, wg, wu, wd, gl, *, tm, th):
    m, d = x2.shape
    hidden = wg.shape[1]
    resident = dict(pipeline_mode=pl.Buffered(1))
    return pl.pallas_call(
        _ffn_body,
        grid=(m // tm, hidden // th),
        in_specs=[
            pl.BlockSpec((tm, d), lambda i, j: (i, 0)),
            pl.BlockSpec((tm, ATT_WIDTH), lambda i, j: (i, 0)),
            pl.BlockSpec((tm, SSM_WIDTH), lambda i, j: (i, 0)),
            pl.BlockSpec((ATT_WIDTH, d), lambda i, j: (0, 0), **resident),
            pl.BlockSpec((SSM_WIDTH, d), lambda i, j: (0, 0), **resident),
            pl.BlockSpec((1, d), lambda i, j: (0, 0)),
            pl.BlockSpec((d, th), lambda i, j: (0, j)),
            pl.BlockSpec((d, th), lambda i, j: (0, j)),
            pl.BlockSpec((th, d), lambda i, j: (j, 0)),
            pl.BlockSpec((1, d), lambda i, j: (0, 0)),
        ],
        out_specs=pl.BlockSpec((tm, d), lambda i, j: (i, 0)),
        out_shape=jax.ShapeDtypeStruct((m, d), F32),
        scratch_shapes=[pltpu.VMEM((tm, d), BF16)],
        compiler_params=pltpu.CompilerParams(
            dimension_semantics=("parallel", "arbitrary"), vmem_limit_bytes=VMEM_LIMIT),
        name="out_ffn",
    )(x2, att2, y2, woa, woy, gf, wg, wu, wd, gl)


def _t5_bucket(rel):
    max_exact = N_BUCKETS // 2
    relf = jnp.maximum(rel, 1).astype(jnp.float32)
    large = max_exact + (jnp.log(relf / max_exact) / math.log(MAX_DISTANCE / max_exact)
                         * (N_BUCKETS - max_exact)).astype(jnp.int32)
    large = jnp.minimum(large, N_BUCKETS - 1)
    return jnp.where(rel < max_exact, rel, large)


def _toeplitz(u, rows, cols):
    nh, period = u.shape
    assert cols < period
    flat = jnp.tile(u, (1, rows))
    return flat[:, :rows * (period - 1)].reshape(nh, rows, period - 1)[:, :, :cols]


def _bias_tiles(rel_bias, rows, key_major):
    assert MAX_DISTANCE <= LANES and rows <= LANES
    table = rel_bias[_t5_bucket(jnp.arange(3 * LANES, dtype=jnp.int32))].T
    tiles = []
    for dd in (2, 1, 0):
        neg = table[:, (dd - 1) * LANES:dd * LANES] if dd >= 1 else jnp.tile(table[:, 0:1], (1, LANES))
        u = jnp.concatenate([table[:, dd * LANES:(dd + 1) * LANES], neg], axis=1)
        if key_major:
            tiles.append(_toeplitz(u, LANES, rows))
        else:
            w = jnp.concatenate([u[:, 0:1], jnp.flip(u[:, 1:], axis=1)], axis=1)
            tiles.append(_toeplitz(w, rows, LANES))
    return jnp.stack(tiles)


def _layer(x, att_call, ssd_call, blk_rows, p, *, kv_bf16):
    b, t, d = x.shape
    m = b * t
    tm = min(ROW_TILE, m)
    proj = _inproj(x.reshape(m, d), p["norm_attn"], p["w_t"], p["idx_k_norm"], tm=tm,
                   seq_len=t if kv_bf16 else None)
    big, bc, small, kin, k_rows, v_rows = proj[:6]
    big3 = big.reshape(b, t, BIG_WIDTH)
    bc3 = bc.reshape(b, t, BC_WIDTH)
    small3 = small.reshape(b, t, SMALL_WIDTH)
    kin3 = kin.reshape(b, t, IDX_DIM)
    hd = (b, t, N_ATT_HEADS, ATT_HEAD_DIM)
    k4 = k_rows.reshape(hd)
    v4 = v_rows.reshape(hd)
    small3_t = jnp.transpose(small.reshape(m // blk_rows, blk_rows, SMALL_WIDTH), (0, 2, 1))
    att = att_call(big3, small3, small3_t, kin3, k4, v4, *proj[6:])
    y, tail, h_fin = ssd_call(big3, bc3, small3, small3_t)
    out = _out_ffn(x.reshape(m, d), att.reshape(m, ATT_WIDTH), y.reshape(m, SSM_WIDTH), p["woa"], p["woy"],
                   p["norm_ffn"], p["wg"], p["wu"], p["wd"], p["norm_final"], tm=tm, th=FFN_TILE)
    return out.reshape(b, t, d), k4, v4, kin3, tail, h_fin


def kernel(x_prompt, x_sample, cache_k, cache_v, cache_kidx, state_conv, state_ssm, page_table, rel_bias,
           norm_attn, w_in, idx_k_norm, conv_w, conv_b, dt_bias, a_log, d_skip, ssm_norm, w_out, norm_ffn,
           w_gate, w_up, w_down, norm_final):
    depth = w_in.shape[0]
    assert depth == 1
    bp, s, d = x_prompt.shape
    bs, t, _ = x_sample.shape
    n_pages = page_table.shape[1]
    past = n_pages * PAGE_SIZE
    lyr = 0

    assert w_in.shape[2] == IN_WIDTH
    w_t = jnp.swapaxes(w_in[lyr], 0, 1).astype(BF16)
    row = lambda v: v.reshape(1, -1)
    p = dict(
        norm_attn=row(norm_attn[lyr]), w_t=w_t, idx_k_norm=row(idx_k_norm[lyr]),
        conv_w=conv_w[lyr], conv_b=conv_b[lyr], dt_bias=dt_bias[lyr], a_log=a_log[lyr], d_skip=d_skip[lyr],
        ssm_norm=ssm_norm[lyr],
        woa=w_out[lyr, :ATT_WIDTH].astype(BF16), woy=w_out[lyr, ATT_WIDTH:].astype(BF16),
        norm_ffn=row(norm_ffn[lyr]), wg=w_gate[lyr].astype(BF16), wu=w_up[lyr].astype(BF16),
        wd=w_down[lyr].astype(BF16), norm_final=row(norm_final),
    )

    topk_p = min(TOPK_MAX, s // 4)
    assert SSD_CHUNK == LANES and s % LANES == 0
    tiles_p = _bias_tiles(rel_bias, LANES, key_major=True)
    tiles_p = tiles_p - tiles_p[0:1]

    def att_prompt(big3, small3, small3_t, kin3, k4, v4, k16, vt16):
        return _attn_prompt(big3, small3_t, kin3, k16.reshape(bp, s, ATT_WIDTH), vt16, tiles_p,
                            topk=topk_p, group=CAUSAL_GROUP)

    zero_conv = jnp.zeros((bp, CONV_W - 1, CONV_DIM), F32)
    zero_ssm = jnp.zeros((bp, N_SSM_HEADS, SSM_HEAD_DIM, SSM_STATE), F32)
    ssm_params = (p["conv_w"], p["conv_b"], p["dt_bias"], p["a_log"], p["d_skip"], p["ssm_norm"])

    def ssd_prompt(big3, bc3, small3, small3_t):
        return _ssd(big3, bc3, small3, small3_t, zero_conv, zero_ssm, *ssm_params, l=SSD_CHUNK)

    yp, kp, vp, kip, cp, sp = _layer(x_prompt, att_prompt, ssd_prompt, SSD_CHUNK, p, kv_bf16=True)

    topk_s = min(TOPK_MAX, (past + t) // 4)
    tiles_s = jnp.repeat(_bias_tiles(rel_bias, t, key_major=False).reshape(3, N_ATT_HEADS * t, LANES),
                         N_ATT_HEADS, axis=-1)
    lane_head = jnp.arange(LANES * N_ATT_HEADS, dtype=jnp.int32) % N_ATT_HEADS
    row_head = jnp.arange(N_ATT_HEADS * t, dtype=jnp.int32) // t
    tiles_s = jnp.where(lane_head[None, :] == row_head[:, None], tiles_s, -jnp.inf)

    cache_kidx_t = jnp.swapaxes(cache_kidx, -1, -2)

    gs = math.gcd(bs, LANES // t)

    def att_sample(big3, small3, small3_t, kin3, k4, v4):
        sel = _select_sample(page_table, big3, small3, kin3, cache_kidx_t, topk=topk_s, gs=gs)
        return _attn_sample(page_table, big3, k4, v4, sel, tiles_s, cache_k, cache_v, gs=gs)

    def ssd_sample(big3, bc3, small3, small3_t):
        return _ssd_sample(big3, bc3, small3, small3_t, state_conv[lyr], state_ssm[lyr], *ssm_params, gs=gs)

    ys, ks, vs, kis, cs, ss = _layer(x_sample, att_sample, ssd_sample, gs * t, p, kv_bf16=False)

    st = lambda a: a[None]
    return (yp, ys, st(kp), st(vp), st(kip), st(cp), st(sp), st(ks), st(vs), st(kis), st(cs), st(ss))
```

```python
import functools
import math

import numpy as np
import jax
import jax.numpy as jnp
from jax import lax
from jax.experimental import pallas as pl
from jax.experimental.pallas import tpu as pltpu

N_ATT_HEADS = 8
ATT_HEAD_DIM = 128
ATT_WIDTH = N_ATT_HEADS * ATT_HEAD_DIM
N_IDX_HEADS = 16
IDX_DIM = 64
TOPK_MAX = 256
N_SSM_HEADS = 16
SSM_HEAD_DIM = 64
SSM_WIDTH = N_SSM_HEADS * SSM_HEAD_DIM
N_SSM_GROUPS = 2
SSM_STATE = 128
CONV_W = 4
CONV_DIM = SSM_WIDTH + 2 * N_SSM_GROUPS * SSM_STATE
SSD_CHUNK = 128
PAGE_SIZE = 128
N_BUCKETS = 32
MAX_DISTANCE = 128
NORM_EPS = 1e-6

LANES = 128
SUBLANES = 8
VMEM_LIMIT = 56 * 1024 * 1024
INPROJ_VMEM_LIMIT = 60000 * 1024

ROW_TILE = 512
FFN_TILE = 512
CAUSAL_GROUP = 2
SELECT_SEQS_PER_STEP = 16

COL_Q, COL_QI, COL_Z, COL_X = (i * ATT_WIDTH for i in range(4))
BIG_WIDTH = 4 * ATT_WIDTH
GROUP_WIDTH = N_SSM_GROUPS * SSM_STATE
BC_WIDTH = 2 * GROUP_WIDTH
SM_KI, SM_WI, SM_DT = 0, IDX_DIM, IDX_DIM + N_IDX_HEADS
SMALL_WIDTH = LANES
INPROJ_TN = 2 * ATT_WIDTH
INPROJ_STEPS = 3
IN_KI = 3 * ATT_WIDTH + N_IDX_HEADS * IDX_DIM
IN_Z = IN_KI + IDX_DIM + N_IDX_HEADS
IN_B = IN_Z + 2 * SSM_WIDTH
IN_WIDTH = IN_B + BC_WIDTH + N_SSM_HEADS

F32 = jnp.float32
BF16 = jnp.bfloat16

_INT_MIN = -(2 ** 31)
_INT_MAX = 2 ** 31 - 1
_KEY_NEG_INF = int(np.array([-np.inf], np.float32).view(np.int32)[0]) ^ 0x7FFFFFFF
_NEG_BIG = -1e30


def _dot(a, b):
    return jnp.dot(a, b, preferred_element_type=F32)


def _dot_nt(a, b):
    return lax.dot_general(a, b, (((1,), (1,)), ((), ())), preferred_element_type=F32)


def _dot_tn(a, b):
    return lax.dot_general(a, b, (((0,), (0,)), ((), ())), preferred_element_type=F32)


def _rms(x, g):
    return x * lax.rsqrt(jnp.mean(x * x, axis=-1, keepdims=True) + NORM_EPS) * g


def _silu(x):
    return x * (1.0 / (1.0 + jnp.exp(-x)))


def _inproj_body(x_ref, g_ref, wa_ref, wb_ref, wsa_ref, wsb_ref, gk_ref, big_ref, bc_ref, small_ref, kin_ref,
                 k4_ref, v4_ref, *rest, kv_bf16):
    u_scr = rest[-1]
    n = pl.program_id(1)
    rows = x_ref.shape[0]

    def head_rows(out_ref, cols):
        for h in range(N_ATT_HEADS):
            out_ref[pl.ds(h, rows, stride=N_ATT_HEADS), :] = cols[:, h * ATT_HEAD_DIM:(h + 1) * ATT_HEAD_DIM]

    @pl.when(n == 0)
    def _():
        ub = _rms(x_ref[...], g_ref[...]).astype(BF16)
        u_scr[...] = ub
        nar_a = _dot_nt(ub, wsa_ref[...])
        nar_b = _dot_nt(ub, wsb_ref[...])
        bc_ref[...] = nar_b[:, :BC_WIDTH]
        pad = jnp.zeros((ub.shape[0], SMALL_WIDTH - SM_DT - N_SSM_HEADS), F32)
        small_ref[...] = jnp.concatenate([nar_a, nar_b[:, BC_WIDTH:], pad], axis=1)
        kin_ref[...] = _rms(nar_a[:, :IDX_DIM], gk_ref[...])

    u = u_scr[...]
    half_a = _dot_nt(u, wa_ref[...])
    half_b = _dot_nt(u, wb_ref[...])

    @pl.when(n < INPROJ_STEPS - 1)
    def _():
        big_ref[:, :ATT_WIDTH] = half_a
        big_ref[:, ATT_WIDTH:] = half_b

    @pl.when(n == INPROJ_STEPS - 1)
    def _():
        head_rows(k4_ref, half_a)
        head_rows(v4_ref, half_b)
        if kv_bf16:
            rest[0][...] = half_a.astype(BF16)
            rest[1][0] = half_b.T.astype(BF16)


def _inproj(x2, g, w_t, gk, *, tm, seq_len=None):
    m, d = x2.shape
    kv_bf16 = seq_len is not None
    assert ATT_WIDTH == SSM_WIDTH == N_IDX_HEADS * IDX_DIM and IN_Z + 2 * SSM_WIDTH == IN_B
    assert IN_Z % (2 * SUBLANES) == 0 and IN_B % (2 * SUBLANES) == 0
    starts_a = (0, IN_Z, ATT_WIDTH)
    starts_b = (3 * ATT_WIDTH, IN_Z + SSM_WIDTH, 2 * ATT_WIDTH)

    def half_rows(starts):
        def index_map(i, j):
            row = jnp.where(j == 0, starts[0], jnp.where(j == 1, starts[1], starts[2]))
            return pl.multiple_of(row, 2 * SUBLANES), 0
        return pl.BlockSpec((pl.Element(ATT_WIDTH), pl.Element(d)), index_map)

    out_specs = [
        pl.BlockSpec((tm, INPROJ_TN), lambda i, j: (i, jnp.minimum(j, INPROJ_STEPS - 2))),
        pl.BlockSpec((tm, BC_WIDTH), lambda i, j: (i, 0)),
        pl.BlockSpec((tm, SMALL_WIDTH), lambda i, j: (i, 0)),
        pl.BlockSpec((tm, IDX_DIM), lambda i, j: (i, 0)),
        pl.BlockSpec((tm * N_ATT_HEADS, ATT_HEAD_DIM), lambda i, j: (i, 0)),
        pl.BlockSpec((tm * N_ATT_HEADS, ATT_HEAD_DIM), lambda i, j: (i, 0)),
    ]
    out_shape = [
        jax.ShapeDtypeStruct((m, BIG_WIDTH), F32),
        jax.ShapeDtypeStruct((m, BC_WIDTH), F32),
        jax.ShapeDtypeStruct((m, SMALL_WIDTH), F32),
        jax.ShapeDtypeStruct((m, IDX_DIM), F32),
        jax.ShapeDtypeStruct((m * N_ATT_HEADS, ATT_HEAD_DIM), F32),
        jax.ShapeDtypeStruct((m * N_ATT_HEADS, ATT_HEAD_DIM), F32),
    ]
    if kv_bf16:
        assert seq_len % tm == 0
        per_seq = seq_len // tm
        out_specs += [
            pl.BlockSpec((tm, ATT_WIDTH), lambda i, j: (i, 0)),
            pl.BlockSpec((1, ATT_WIDTH, tm), lambda i, j: (i // per_seq, 0, i % per_seq)),
        ]
        out_shape += [
            jax.ShapeDtypeStruct((m, ATT_WIDTH), BF16),
            jax.ShapeDtypeStruct((m // seq_len, ATT_WIDTH, seq_len), BF16),
        ]
    return pl.pallas_call(
        functools.partial(_inproj_body, kv_bf16=kv_bf16),
        grid=(m // tm, INPROJ_STEPS),
        in_specs=[
            pl.BlockSpec((tm, d), lambda i, j: (i, 0)),
            pl.BlockSpec((1, d), lambda i, j: (0, 0)),
            half_rows(starts_a),
            half_rows(starts_b),
            pl.BlockSpec((pl.Element(IN_Z - IN_KI), pl.Element(d)), lambda i, j: (IN_KI, 0),
                         pipeline_mode=pl.Buffered(1)),
            pl.BlockSpec((pl.Element(IN_WIDTH - IN_B), pl.Element(d)), lambda i, j: (IN_B, 0),
                         pipeline_mode=pl.Buffered(1)),
            pl.BlockSpec((1, IDX_DIM), lambda i, j: (0, 0)),
        ],
        out_specs=out_specs,
        out_shape=out_shape,
        scratch_shapes=[pltpu.VMEM((tm, d), BF16)],
        compiler_params=pltpu.CompilerParams(
            dimension_semantics=("parallel", "arbitrary"), vmem_limit_bytes=INPROJ_VMEM_LIMIT),
        name="inproj",
    )(x2, g, w_t, w_t, w_t, w_t, gk)


def _sortable_key(s):
    s = jnp.where(s == 0.0, 0.0, s)
    b = pltpu.bitcast(s, jnp.int32)
    return b ^ ((b >> 31) & 0x7FFFFFFF)


def _count_keys(key_ref, nblk, pred, key_axis):
    def body(kb, acc):
        return acc + jnp.where(pred(key_ref[kb], kb), 1.0, 0.0)

    acc = jnp.zeros(key_ref.shape[1:], F32)
    if isinstance(nblk, int):
        for kb in range(nblk):
            acc = body(kb, acc)
    else:
        acc = lax.fori_loop(0, nblk, body, acc)
    return jnp.sum(acc, axis=key_axis, keepdims=True)


_I16_MIN = -(2 ** 15)


def _pack_halves(key):
    r = key.shape[0] // 2
    a, b = key[:r], key[r:]
    hi = (a & jnp.int32(-65536)) | ((b >> 16) & 0xFFFF)
    lo = ((a ^ 0x8000) << 16) | ((b ^ 0x8000) & 0xFFFF)
    return hi, lo


def _count16(ref, nblk, cmp, cand):
    r = ref.shape[1]
    pair = (cand << 16) | (cand & 0xFFFF)
    cb = pltpu.bitcast(jnp.broadcast_to(pair, (r, LANES)), jnp.int16)
    acc = jnp.zeros((2 * r, LANES), jnp.int16)
    for kb in range(nblk):
        acc = acc + jnp.where(cmp(pltpu.bitcast(ref[kb], jnp.int16), cb), jnp.int16(1), jnp.int16(0))
    words = pltpu.bitcast(acc, jnp.int32)
    both = (words >> 16) + ((words << 16) >> 16)
    return jnp.sum(both.astype(F32), axis=0, keepdims=True)


def _kth_of_16bit(ref, nblk, need):
    ge = lambda k, c: k >= c
    res = jnp.where(_count16(ref, nblk, ge, jnp.zeros((1, LANES), jnp.int32)) >= need, 0, _I16_MIN).astype(jnp.int32)

    def body(it, res):
        cand = res | jnp.left_shift(jnp.int32(1), 14 - it)
        return jnp.where(_count16(ref, nblk, ge, cand) >= need, cand, res)

    return lax.fori_loop(0, 15, body, res)


def _kth_key_two_phase(hi_ref, lo_ref, nblk, topk):
    kf = float(topk)
    top = _kth_of_16bit(hi_ref, nblk, kf)
    n_above = _count16(hi_ref, nblk, lambda k, c: k > c, top)
    r = hi_ref.shape[1]
    top_b = pltpu.bitcast(jnp.broadcast_to((top << 16) | (top & 0xFFFF), (r, LANES)), jnp.int16)
    for kb in range(nblk):
        band = pltpu.bitcast(hi_ref[kb], jnp.int16) == top_b
        lo_ref[kb] = pltpu.bitcast(jnp.where(band, pltpu.bitcast(lo_ref[kb], jnp.int16), jnp.int16(_I16_MIN)),
                                   jnp.int32)
    low = _kth_of_16bit(lo_ref, nblk, kf - n_above)
    return (top << 16) | ((low ^ 0x8000) & 0xFFFF)


def _topk_select_params(key_ref, nblk, topk, key_axis, thr=None):
    shape = key_ref.shape[1:]
    qshape = tuple(1 if a == key_axis else n for a, n in enumerate(shape))
    blk = shape[key_axis]
    kf = float(topk)

    def count_ge(cand):
        cb = jnp.broadcast_to(cand, shape)
        return _count_keys(key_ref, nblk, lambda k, kb: k >= cb, key_axis)

    if thr is None:
        res = jnp.where(count_ge(jnp.zeros(qshape, jnp.int32)) >= kf, 0, _INT_MIN).astype(jnp.int32)

        def bit_body(it, res):
            cand = res | jnp.left_shift(jnp.int32(1), 30 - it)
            return jnp.where(count_ge(cand) >= kf, cand, res)

        thr = lax.fori_loop(0, 31, bit_body, res)
    thr_b = jnp.broadcast_to(thr, shape)
    n_gt = _count_keys(key_ref, nblk, lambda k, kb: k > thr_b, key_axis)
    n_ge = count_ge(thr)
    tied = jnp.logical_and(n_ge > kf, thr > _KEY_NEG_INF)
    any_tied = jnp.max(jnp.where(tied, 1.0, 0.0)) > 0.0
    need = kf - n_gt
    kpos = lax.broadcasted_iota(jnp.int32, shape, key_axis)

    def tie_break():
        def body(it, cut):
            cand = cut | jnp.left_shift(jnp.int32(1), 30 - it)
            cand_b = jnp.broadcast_to(cand, shape)
            n = _count_keys(key_ref, nblk,
                            lambda k, kb: jnp.logical_and(k == thr_b, kpos + kb * blk < cand_b), key_axis)
            return jnp.where(n <= need, cand, cut)

        return lax.fori_loop(0, 31, body, jnp.zeros(qshape, jnp.int32))

    cut = lax.cond(any_tied, tie_break, lambda: jnp.full(qshape, _INT_MAX, jnp.int32))
    return thr, cut


def _attn_prompt_body(q_ref, qi_ref, smt_ref, kin_ref, k_ref, vt_ref, bias_ref, o_ref,
                      key_scr, hi_scr, lo_scr, mask_scr, lg_scr, qi_scr, *, topk, group):
    i = pl.program_id(1)
    q0 = i * LANES
    blk = (LANES, LANES)
    scale = ATT_HEAD_DIM ** -0.5
    nq = key_scr.shape[0]

    w_t = smt_ref[0, SM_WI:SM_WI + N_IDX_HEADS, :]
    qi_all = qi_ref[0]
    for h in range(N_IDX_HEADS):
        qi_scr[h * LANES:(h + 1) * LANES, :] = qi_all[:, h * IDX_DIM:(h + 1) * IDX_DIM].astype(BF16)
    qi = qi_scr[...]
    kpos = lax.broadcasted_iota(jnp.int32, blk, 0)
    qpos = lax.broadcasted_iota(jnp.int32, blk, 1) + q0
    qb = q_ref[0].astype(BF16)

    def tile(nb, first_near):
        if nb * LANES <= topk:
            for kb in range(nb):
                mask_scr[kb * LANES:(kb + 1) * LANES, :] = jnp.where(kpos + kb * LANES <= qpos, 0.0, -jnp.inf)
        else:
            select_mask(nb, first_near)
        attend(nb, first_near)

    def select_mask(nb, first_near):
        for kb in range(nb):
            kin = kin_ref[0, kb * LANES:(kb + 1) * LANES, :].astype(BF16)
            s = _dot_nt(kin, qi)
            score = jnp.zeros(blk, F32)
            for h in range(N_IDX_HEADS):
                score = score + jnp.maximum(s[:, h * LANES:(h + 1) * LANES], 0.0) * w_t[h:h + 1, :]
            if kb >= first_near:
                score = jnp.where(kpos + kb * LANES <= qpos, score, -jnp.inf)
            key = _sortable_key(score)
            key_scr[kb] = key
            hi_scr[kb], lo_scr[kb] = _pack_halves(key)

        thr = _kth_key_two_phase(hi_scr, lo_scr, nb, topk)
        thr, cut = _topk_select_params(key_scr, nb, topk, key_axis=0, thr=thr)
        thr_b = jnp.broadcast_to(thr, blk)
        cut_b = jnp.broadcast_to(cut, blk)
        for kb in range(nb):
            kp = kpos + kb * LANES
            key = key_scr[kb]
            valid = jnp.logical_or(key > thr_b, jnp.logical_and(key == thr_b, kp < cut_b))
            if kb >= first_near:
                valid = jnp.logical_and(valid, kp <= qpos)
            mask_scr[kb * LANES:(kb + 1) * LANES, :] = jnp.where(valid, 0.0, -jnp.inf)

    def attend(nb, first_near):
        nk = nb * LANES
        for h in range(N_ATT_HEADS):
            hs = slice(h * ATT_HEAD_DIM, (h + 1) * ATT_HEAD_DIM)
            lg = _dot_nt(k_ref[0, 0:nk, hs], qb[:, hs]) * scale + mask_scr[0:nk, :]
            for kb in range(first_near, nb):
                tix = 2 - jnp.clip(i - kb, 0, 2)
                lg_scr[kb * LANES:(kb + 1) * LANES, :] = lg[kb * LANES:(kb + 1) * LANES, :] + bias_ref[tix, h]
            if first_near > 0:
                lg_scr[0:first_near * LANES, :] = lg[0:first_near * LANES, :]
            lg = lg_scr[0:nk, :]
            m = jnp.max(lg, axis=0, keepdims=True)
            p = jnp.exp(lg - m)
            l = jnp.sum(p, axis=0, keepdims=True)
            out_t = _dot(vt_ref[0, hs, 0:nk], p.astype(BF16)) / l
            o_ref[0, :, hs] = out_t.T.astype(o_ref.dtype)

    ngroups = -(-nq // group)
    for c in range(ngroups):
        nb = min((c + 1) * group, nq)

        @pl.when(i // group == c)
        def _(nb=nb, c=c):
            tile(nb, max(c * group - 1, 0))


def _attn_prompt(big3, small3_t, kin3, k16, vt16, bias_tiles_t, *, topk, group):
    b, s, _ = big3.shape
    nq = s // LANES
    body = functools.partial(_attn_prompt_body, topk=topk, group=group)
    return pl.pallas_call(
        body,
        grid=(b, nq),
        in_specs=[
            pl.BlockSpec((1, LANES, ATT_WIDTH), lambda bi, i: (bi, i, COL_Q // ATT_WIDTH)),
            pl.BlockSpec((1, LANES, N_IDX_HEADS * IDX_DIM), lambda bi, i: (bi, i, COL_QI // (N_IDX_HEADS * IDX_DIM))),
            pl.BlockSpec((1, SMALL_WIDTH, LANES), lambda bi, i: (bi * nq + i, 0, 0)),
            pl.BlockSpec((1, s, IDX_DIM), lambda bi, i: (bi, 0, 0)),
            pl.BlockSpec((1, s, ATT_WIDTH), lambda bi, i: (bi, 0, 0)),
            pl.BlockSpec((1, ATT_WIDTH, s), lambda bi, i: (bi, 0, 0)),
            pl.BlockSpec(bias_tiles_t.shape, lambda bi, i: (0, 0, 0, 0)),
        ],
        out_specs=pl.BlockSpec((1, LANES, ATT_WIDTH), lambda bi, i: (bi, i, 0)),
        out_shape=jax.ShapeDtypeStruct((b, s, ATT_WIDTH), BF16),
        scratch_shapes=[
            pltpu.VMEM((nq, LANES, LANES), jnp.int32),
            pltpu.VMEM((nq, LANES // 2, LANES), jnp.int32),
            pltpu.VMEM((nq, LANES // 2, LANES), jnp.int32),
            pltpu.VMEM((s, LANES), F32),
            pltpu.VMEM((s, LANES), F32),
            pltpu.VMEM((N_IDX_HEADS * LANES, IDX_DIM), BF16),
        ],
        compiler_params=pltpu.CompilerParams(
            dimension_semantics=("parallel", "arbitrary"), vmem_limit_bytes=VMEM_LIMIT),
        name="attn_prompt",
    )(big3, big3, small3_t, kin3, k16, vt16, bias_tiles_t)


def _select_sample_body(pt_ref, qi_ref, sm_ref, kinew_ref, *rest, n_pages, t, topk, ss):
    kidx_refs = rest[:ss * n_pages]
    sel_ref = rest[ss * n_pages]
    key_scr, wb_scr = rest[ss * n_pages + 1:]
    nblk = n_pages + 1
    j = pl.program_id(1)
    trow = lax.broadcasted_iota(jnp.int32, (t, LANES), 0)
    lane_t = lax.broadcasted_iota(jnp.int32, (t, LANES), 1)

    def scores(s):
        return jnp.sum(jnp.maximum(s.reshape(N_IDX_HEADS, t, LANES), 0.0) * wb_scr[...], axis=0)

    for u in range(ss):
        r0 = pl.multiple_of((j * ss + u) * t, t)
        wi = sm_ref[u, :, SM_WI:SM_WI + N_IDX_HEADS]
        for h in range(N_IDX_HEADS):
            wb_scr[h] = jnp.broadcast_to(wi[:, h:h + 1], (t, LANES))
        qi_all = qi_ref[u]
        qi = jnp.concatenate([qi_all[:, h * IDX_DIM:(h + 1) * IDX_DIM] for h in range(N_IDX_HEADS)],
                             axis=0).astype(BF16)
        for p in range(n_pages):
            kin_t = kidx_refs[u * n_pages + p][0, 0].astype(BF16)
            key_scr[p, pl.ds(r0, t), :] = _sortable_key(scores(_dot(qi, kin_t)))
        kin_new = jnp.concatenate([kinew_ref[u], jnp.zeros((LANES - t, IDX_DIM), F32)], axis=0)
        s_new = jnp.where(lane_t <= trow, scores(_dot_nt(qi, kin_new.astype(BF16))), -jnp.inf)
        key_scr[n_pages, pl.ds(r0, t), :] = _sortable_key(s_new)

    @pl.when(j == pl.num_programs(1) - 1)
    def _():
        rows = key_scr.shape[1]
        thr, cut = _topk_select_params(key_scr, nblk, topk, key_axis=1)
        thr_b = jnp.broadcast_to(thr, (rows, LANES))
        cut_b = jnp.broadcast_to(cut, (rows, LANES))
        lane = lax.broadcasted_iota(jnp.int32, (rows, LANES), 1)
        tq = lax.broadcasted_iota(jnp.int32, (rows, LANES), 0) & (t - 1)
        for p in range(nblk):
            key = key_scr[p]
            sel = jnp.logical_or(key > thr_b, jnp.logical_and(key == thr_b, lane + p * LANES < cut_b))
            if p == n_pages:
                sel = jnp.logical_and(sel, lane <= tq)
            sel_ref[0, p] = jnp.where(sel, 1.0, 0.0)


def _select_sample(page_table, big3, small3, kin3, cache_kidx_t, *, topk, gs):
    b, t, _ = small3.shape
    n_pages = page_table.shape[1]
    ss = math.gcd(gs, SELECT_SEQS_PER_STEP)
    steps = gs // ss
    assert b % gs == 0 and t & (t - 1) == 0
    body = functools.partial(_select_sample_body, n_pages=n_pages, t=t, topk=topk, ss=ss)
    blk = lambda g, j: g * steps + j
    qi_width = N_IDX_HEADS * IDX_DIM
    in_specs = [
        pl.BlockSpec((ss, t, qi_width), lambda g, j, pt: (blk(g, j), 0, COL_QI // qi_width)),
        pl.BlockSpec((ss, t, SMALL_WIDTH), lambda g, j, pt: (blk(g, j), 0, 0)),
        pl.BlockSpec((ss, t, IDX_DIM), lambda g, j, pt: (blk(g, j), 0, 0)),
    ]
    in_specs += [pl.BlockSpec((1, 1, IDX_DIM, PAGE_SIZE),
                              lambda g, j, pt, u=u, p=p: (0, pt[blk(g, j) * ss + u, p], 0, 0))
                 for u in range(ss) for p in range(n_pages)]
    sel_shape = (b // gs, n_pages + 1, gs * t, LANES)
    return pl.pallas_call(
        body,
        grid_spec=pltpu.PrefetchScalarGridSpec(
            num_scalar_prefetch=1,
            grid=(b // gs, steps),
            in_specs=in_specs,
            out_specs=pl.BlockSpec((1,) + sel_shape[1:], lambda g, j, pt: (g, 0, 0, 0)),
            scratch_shapes=[
                pltpu.VMEM(sel_shape[1:], jnp.int32),
                pltpu.VMEM((N_IDX_HEADS, t, LANES), F32),
            ],
        ),
        out_shape=jax.ShapeDtypeStruct(sel_shape, F32),
        compiler_params=pltpu.CompilerParams(
            dimension_semantics=("parallel", "arbitrary"), vmem_limit_bytes=VMEM_LIMIT),
        name="select_sample",
    )(page_table, big3, small3, kin3, *([cache_kidx_t] * (ss * n_pages)))


def _attn_sample_body(pt_ref, q_ref, knew_ref, vnew_ref, sel_ref, bias_ref, *rest, n_pages, t):
    k_refs = rest[:n_pages]
    v_refs = rest[n_pages:2 * n_pages]
    o_ref = rest[2 * n_pages]
    (lg_scr,) = rest[2 * n_pages + 1:]
    nblk = n_pages + 1
    rows = N_ATT_HEADS * t
    flat = PAGE_SIZE * N_ATT_HEADS
    scale = ATT_HEAD_DIM ** -0.5

    q = q_ref[0]
    q64 = jnp.concatenate([q[:, h * ATT_HEAD_DIM:(h + 1) * ATT_HEAD_DIM] for h in range(N_ATT_HEADS)],
                          axis=0).astype(BF16)
    er = lax.broadcasted_iota(jnp.int32, (LANES, flat), 0)
    ec = lax.broadcasted_iota(jnp.int32, (LANES, flat), 1)
    expand = jnp.where((ec >> (N_ATT_HEADS.bit_length() - 1)) == er, 1.0, 0.0).astype(BF16)

    def flat_kv(refs, p):
        if p < n_pages:
            return refs[p][0, 0].reshape(flat, ATT_HEAD_DIM).astype(BF16), flat
        new = refs[p][0].reshape(t * N_ATT_HEADS, ATT_HEAD_DIM)
        pad = jnp.zeros((LANES - t * N_ATT_HEADS, ATT_HEAD_DIM), F32)
        return jnp.concatenate([new, pad], axis=0).astype(BF16), LANES

    k_all = list(k_refs) + [knew_ref]
    v_all = list(v_refs) + [vnew_ref]
    m = jnp.full((rows, 1), _NEG_BIG, F32)
    for p in range(nblk):
        kflat, width = flat_kv(k_all, p)
        selx = _dot(sel_ref[0, p].astype(BF16), expand[:, :width])
        unsel = (selx - 1.0) * (-_NEG_BIG)
        tix = 2 - min(n_pages - p, 2)
        lg = _dot_nt(q64, kflat) * scale + bias_ref[tix, :, :width] + jnp.concatenate([unsel] * N_ATT_HEADS, axis=0)
        lg_scr[p, :, :width] = lg
        m = jnp.maximum(m, jnp.max(lg, axis=-1, keepdims=True))

    l = jnp.zeros((rows, 1), F32)
    acc = jnp.zeros((rows, ATT_HEAD_DIM), F32)
    for p in range(nblk):
        vflat, width = flat_kv(v_all, p)
        pr = jnp.exp(lg_scr[p, :, :width] - m)
        l = l + jnp.sum(pr, axis=-1, keepdims=True)
        acc = acc + _dot(pr.astype(BF16), vflat)
    out = acc / l
    for h in range(N_ATT_HEADS):
        o_ref[0, :, h * ATT_HEAD_DIM:(h + 1) * ATT_HEAD_DIM] = out[h * t:(h + 1) * t, :].astype(o_ref.dtype)


def _attn_sample(page_table, big3, k_new, v_new, sel, bias_rows, cache_k, cache_v, *, gs):
    b, t, _ = big3.shape
    n_pages = page_table.shape[1]
    assert t == SUBLANES and N_ATT_HEADS & (N_ATT_HEADS - 1) == 0 and t * N_ATT_HEADS <= LANES
    body = functools.partial(_attn_sample_body, n_pages=n_pages, t=t)
    kv_block = (1, 1, PAGE_SIZE, N_ATT_HEADS, ATT_HEAD_DIM)
    new_block = (1, t, N_ATT_HEADS, ATT_HEAD_DIM)

    in_specs = [
        pl.BlockSpec((1, t, ATT_WIDTH), lambda bi, pt: (bi, 0, COL_Q // ATT_WIDTH)),
        pl.BlockSpec(new_block, lambda bi, pt: (bi, 0, 0, 0)),
        pl.BlockSpec(new_block, lambda bi, pt: (bi, 0, 0, 0)),
        pl.BlockSpec((1, n_pages + 1, t, LANES), lambda bi, pt: (bi // gs, 0, bi % gs, 0)),
        pl.BlockSpec(bias_rows.shape, lambda bi, pt: (0, 0, 0)),
    ]
    in_specs += [pl.BlockSpec(kv_block, lambda bi, pt, p=p: (0, pt[bi, p], 0, 0, 0)) for p in range(n_pages)] * 2
    rows = N_ATT_HEADS * t
    return pl.pallas_call(
        body,
        grid_spec=pltpu.PrefetchScalarGridSpec(
            num_scalar_prefetch=1,
            grid=(b,),
            in_specs=in_specs,
            out_specs=pl.BlockSpec((1, t, ATT_WIDTH), lambda bi, pt: (bi, 0, 0)),
            scratch_shapes=[pltpu.VMEM((n_pages + 1, rows, PAGE_SIZE * N_ATT_HEADS), F32)],
        ),
        out_shape=jax.ShapeDtypeStruct((b, t, ATT_WIDTH), BF16),
        compiler_params=pltpu.CompilerParams(
            dimension_semantics=("arbitrary",), vmem_limit_bytes=VMEM_LIMIT),
        name="attn_sample",
    )(page_table, big3, k_new, v_new, sel, bias_rows, *([cache_k] * n_pages), *([cache_v] * n_pages))


def _split3(a):
    a1 = a.astype(BF16)
    r1 = a - a1.astype(F32)
    a2 = r1.astype(BF16)
    a3 = (r1 - a2.astype(F32)).astype(BF16)
    return a1, a2, a3


def _softplus(v):
    return jnp.maximum(v, 0.0) + jnp.log1p(jnp.exp(-jnp.abs(v)))


def _head_expander():
    r = lax.broadcasted_iota(jnp.int32, (SMALL_WIDTH, SSM_WIDTH), 0)
    c = lax.broadcasted_iota(jnp.int32, (SMALL_WIDTH, SSM_WIDTH), 1)
    return jnp.where(r - SM_DT == c // SSM_HEAD_DIM, 1.0, 0.0).astype(BF16)


def _ssd_body(x_ref, bm_ref, cm_ref, z_ref, sm_ref, smt_ref, pre_ref, h0_ref,
              cw_ref, cb_ref, dtb_ref, dtbt_ref, alog_ref, alogt_ref, dskipx_ref, gn_ref,
              y_ref, tail_ref, hout_ref, xp_scr, ht_scr, y_scr, *, l):
    c = pl.program_id(1)
    nc = pl.num_programs(1)
    off = SUBLANES
    heads_per_group = N_SSM_HEADS // N_SSM_GROUPS
    gcols = heads_per_group * SSM_HEAD_DIM

    @pl.when(c == 0)
    def _():
        xp_scr[off - (CONV_W - 1):off, :] = pre_ref[0]
        for h in range(N_SSM_HEADS):
            g, hh = divmod(h, heads_per_group)
            ht_scr[g, :, hh * SSM_HEAD_DIM:(hh + 1) * SSM_HEAD_DIM] = h0_ref[0, h].T

    xp_scr[off:off + l, 0:SSM_WIDTH] = x_ref[0]
    xp_scr[off:off + l, SSM_WIDTH:SSM_WIDTH + GROUP_WIDTH] = bm_ref[0]
    xp_scr[off:off + l, SSM_WIDTH + GROUP_WIDTH:CONV_DIM] = cm_ref[0]

    conv = cb_ref[...]
    for j in range(CONV_W):
        s0 = off - (CONV_W - 1) + j
        conv = conv + xp_scr[s0:s0 + l, :] * cw_ref[j:j + 1, :]
    xc = _silu(conv)
    tail = xp_scr[off + l - (CONV_W - 1):off + l, :]
    tail_ref[0] = tail
    xp_scr[off - (CONV_W - 1):off, :] = tail

    xs = xc[:, :SSM_WIDTH]
    bc = [xc[:, SSM_WIDTH + g * SSM_STATE:SSM_WIDTH + (g + 1) * SSM_STATE].astype(BF16)
          for g in range(N_SSM_GROUPS)]
    cc = [xc[:, SSM_WIDTH + GROUP_WIDTH + g * SSM_STATE:SSM_WIDTH + GROUP_WIDTH + (g + 1) * SSM_STATE]
          .astype(BF16) for g in range(N_SSM_GROUPS)]

    dt = _softplus(sm_ref[0] + dtb_ref[...])
    a = dt * (-jnp.exp(alog_ref[...]))
    dtt = _softplus(smt_ref[0, SM_DT:SM_DT + N_SSM_HEADS, :] + dtbt_ref[...])
    at = dtt * (-jnp.exp(alogt_ref[...]))

    ri = lax.broadcasted_iota(jnp.int32, (l, l), 0)
    ci = lax.broadcasted_iota(jnp.int32, (l, l), 1)
    causal = ri >= ci
    tri = jnp.where(causal, 1.0, 0.0).astype(BF16)
    tri_u = jnp.where(ri <= ci, 1.0, 0.0).astype(BF16)
    acum = sum(_dot(tri, piece) for piece in _split3(a))
    acum_t = sum(_dot(piece, tri_u) for piece in _split3(at))
    last = acum[l - 1:l, :]
    wts = jnp.exp(last - acum) * dt
    expand = _head_expander()
    acum_x = sum(_dot(piece, expand) for piece in _split3(acum))
    wts_x = sum(_dot(piece, expand) for piece in _split3(wts))
    exp_last_x = jnp.exp(acum_x[l - 1:l, :])

    cb = [_dot_nt(cc[g], bc[g]) for g in range(N_SSM_GROUPS)]
    for h in range(N_SSM_HEADS):
        g = h // heads_per_group
        hs = slice(h * SSM_HEAD_DIM, (h + 1) * SSM_HEAD_DIM)
        col = acum[:, SM_DT + h:SM_DT + h + 1]
        decay = jnp.exp(jnp.where(causal, col - acum_t[h:h + 1, :], -jnp.inf))
        mmat = cb[g] * decay * dtt[h:h + 1, :]
        y_scr[:, hs] = _dot(mmat.astype(BF16), xs[:, hs].astype(BF16))

    xw = (xs * wts_x).astype(BF16)
    y_inter = []
    for g in range(N_SSM_GROUPS):
        gsl = slice(g * gcols, (g + 1) * gcols)
        ht = ht_scr[g]
        y_inter.append(_dot(cc[g], ht.astype(BF16)))
        ht_scr[g] = ht * exp_last_x[:, gsl] + _dot_tn(bc[g], xw[:, gsl])
    y = y_scr[...] + jnp.concatenate(y_inter, axis=1) * jnp.exp(acum_x) + dskipx_ref[...] * xs
    y_ref[0] = _rms(y * _silu(z_ref[0]), gn_ref[...]).astype(y_ref.dtype)

    @pl.when(c == nc - 1)
    def _():
        for h in range(N_SSM_HEADS):
            g, hh = divmod(h, heads_per_group)
            hout_ref[0, h] = ht_scr[g, :, hh * SSM_HEAD_DIM:(hh + 1) * SSM_HEAD_DIM].T


def _pad_lanes(v):
    return jnp.zeros((1, SMALL_WIDTH), F32).at[0, SM_DT:SM_DT + N_SSM_HEADS].set(v)


def _ssd(big3, bc3, small3, small3_t, prefix, h0, conv_w, conv_b, dt_bias, a_log, d_skip, ssm_norm, *, l):
    b, t, _ = big3.shape
    nc = t // l
    body = functools.partial(_ssd_body, l=l)
    row = lambda v: v.reshape(1, -1)
    colv = lambda v: v.reshape(-1, 1)
    const2 = lambda shape: pl.BlockSpec(shape, lambda bi, c: (0, 0))
    return pl.pallas_call(
        body,
        grid=(b, nc),
        in_specs=[
            pl.BlockSpec((1, l, SSM_WIDTH), lambda bi, c: (bi, c, COL_X // SSM_WIDTH)),
            pl.BlockSpec((1, l, GROUP_WIDTH), lambda bi, c: (bi, c, 0)),
            pl.BlockSpec((1, l, GROUP_WIDTH), lambda bi, c: (bi, c, 1)),
            pl.BlockSpec((1, l, SSM_WIDTH), lambda bi, c: (bi, c, COL_Z // SSM_WIDTH)),
            pl.BlockSpec((1, l, SMALL_WIDTH), lambda bi, c: (bi, c, 0)),
            pl.BlockSpec((1, SMALL_WIDTH, l), lambda bi, c: (bi * nc + c, 0, 0)),
            pl.BlockSpec((1, CONV_W - 1, CONV_DIM), lambda bi, c: (bi, 0, 0)),
            pl.BlockSpec((1, N_SSM_HEADS, SSM_HEAD_DIM, SSM_STATE), lambda bi, c: (bi, 0, 0, 0)),
            const2((CONV_W, CONV_DIM)),
            const2((1, CONV_DIM)),
            const2((1, SMALL_WIDTH)),
            const2((N_SSM_HEADS, 1)),
            const2((1, SMALL_WIDTH)),
            const2((N_SSM_HEADS, 1)),
            const2((1, SSM_WIDTH)),
            const2((1, SSM_WIDTH)),
        ],
        out_specs=[
            pl.BlockSpec((1, l, SSM_WIDTH), lambda bi, c: (bi, c, 0)),
            pl.BlockSpec((1, CONV_W - 1, CONV_DIM), lambda bi, c: (bi, 0, 0)),
            pl.BlockSpec((1, N_SSM_HEADS, SSM_HEAD_DIM, SSM_STATE), lambda bi, c: (bi, 0, 0, 0)),
        ],
        out_shape=[
            jax.ShapeDtypeStruct((b, t, SSM_WIDTH), BF16),
            jax.ShapeDtypeStruct((b, CONV_W - 1, CONV_DIM), F32),
            jax.ShapeDtypeStruct((b, N_SSM_HEADS, SSM_HEAD_DIM, SSM_STATE), F32),
        ],
        scratch_shapes=[
            pltpu.VMEM((SUBLANES + l, CONV_DIM), F32),
            pltpu.VMEM((N_SSM_GROUPS, SSM_STATE, SSM_WIDTH // N_SSM_GROUPS), F32),
            pltpu.VMEM((l, SSM_WIDTH), F32),
        ],
        compiler_params=pltpu.CompilerParams(
            dimension_semantics=("parallel", "arbitrary"), vmem_limit_bytes=VMEM_LIMIT),
        name="ssd",
    )(big3, bc3, bc3, big3, small3, small3_t, prefix, h0,
      conv_w, row(conv_b), _pad_lanes(dt_bias), colv(dt_bias), _pad_lanes(a_log), colv(a_log),
      row(jnp.repeat(d_skip, SSM_HEAD_DIM)), row(ssm_norm))


def _ssd_sample_body(x_ref, bm_ref, cm_ref, z_ref, sm_ref, smt_ref, pre_ref, h0_ref,
                     cw_ref, cb_ref, dtb_ref, dtbt_ref, alog_ref, alogt_ref, dskip_ref, gn_ref,
                     y_ref, tail_ref, hout_ref, xp_scr, xc_scr, yi_scr, xw_scr, y_scr, *, gs, t):
    rows = gs * t
    off = SUBLANES
    tshift = t.bit_length() - 1
    gw = N_SSM_GROUPS * SSM_STATE
    heads_per_group = N_SSM_HEADS // N_SSM_GROUPS
    gcols = heads_per_group * SSM_HEAD_DIM

    for s in range(gs):
        xp_scr[s, off - (CONV_W - 1):off, :] = pre_ref[s]
        xp_scr[s, off:off + t, 0:SSM_WIDTH] = x_ref[s]
        xp_scr[s, off:off + t, SSM_WIDTH:SSM_WIDTH + gw] = bm_ref[s]
        xp_scr[s, off:off + t, SSM_WIDTH + gw:CONV_DIM] = cm_ref[s]
        conv = cb_ref[...]
        for j in range(CONV_W):
            s0 = off - (CONV_W - 1) + j
            conv = conv + xp_scr[s, s0:s0 + t, :] * cw_ref[j:j + 1, :]
        xc_scr[s * t:(s + 1) * t, :] = _silu(conv)
        tail_ref[s] = xp_scr[s, off + t - (CONV_W - 1):off + t, :]

    xc = xc_scr[...]
    xs = xc[:, :SSM_WIDTH]
    bcf = [xc[:, SSM_WIDTH + g * SSM_STATE:SSM_WIDTH + (g + 1) * SSM_STATE] for g in range(N_SSM_GROUPS)]
    ccf = [xc[:, SSM_WIDTH + gw + g * SSM_STATE:SSM_WIDTH + gw + (g + 1) * SSM_STATE] for g in range(N_SSM_GROUPS)]

    dt = _softplus(sm_ref[...].reshape(rows, SMALL_WIDTH)[:, SM_DT:SM_DT + N_SSM_HEADS] + dtb_ref[...])
    dtt = _softplus(smt_ref[0, SM_DT:SM_DT + N_SSM_HEADS, :] + dtbt_ref[...])
    a = dt * (-jnp.exp(alog_ref[...]))
    at = dtt * (-jnp.exp(alogt_ref[...]))

    ri = lax.broadcasted_iota(jnp.int32, (rows, rows), 0)
    ci = lax.broadcasted_iota(jnp.int32, (rows, rows), 1)
    same = (ri >> tshift) == (ci >> tshift)
    causal = jnp.logical_and(same, ri >= ci)
    tri = jnp.where(causal, 1.0, 0.0).astype(BF16)
    tri_u = jnp.where(jnp.logical_and(same, ri <= ci), 1.0, 0.0).astype(BF16)
    pick_last = jnp.where(ci == (((ri >> tshift) << tshift) + (t - 1)), 1.0, 0.0).astype(BF16)
    acum = sum(_dot(tri, piece) for piece in _split3(a))
    acum_t = sum(_dot(piece, tri_u) for piece in _split3(at))
    last = sum(_dot(pick_last, piece) for piece in _split3(acum))
    wts = jnp.exp(last - acum) * dt
    exp_last = jnp.exp(last)

    for s in range(gs):
        rs = slice(s * t, (s + 1) * t)
        for g in range(N_SSM_GROUPS):
            h0g = h0_ref[s, g * heads_per_group:(g + 1) * heads_per_group].reshape(gcols, SSM_STATE)
            yi_scr[rs, g * gcols:(g + 1) * gcols] = _dot_nt(ccf[g][rs, :], h0g)

    cb = [_dot_nt(ccf[g].astype(BF16), bcf[g].astype(BF16)) for g in range(N_SSM_GROUPS)]
    for h in range(N_SSM_HEADS):
        g = h // heads_per_group
        hs = slice(h * SSM_HEAD_DIM, (h + 1) * SSM_HEAD_DIM)
        col = acum[:, h:h + 1]
        decay = jnp.exp(jnp.where(causal, col - acum_t[h:h + 1, :], -jnp.inf))
        mmat = cb[g] * decay * dtt[h:h + 1, :]
        xh = xs[:, hs]
        y_intra = _dot(mmat.astype(BF16), xh.astype(BF16))
        y_scr[:, hs] = y_intra + yi_scr[:, hs] * jnp.exp(col) + dskip_ref[h] * xh
        xw_scr[:, hs] = xh * wts[:, h:h + 1]

    out = _rms(y_scr[...] * _silu(z_ref[...].reshape(rows, SSM_WIDTH)), gn_ref[...])
    for s in range(gs):
        y_ref[s] = out[s * t:(s + 1) * t, :].astype(y_ref.dtype)

    for s in range(gs):
        rs = slice(s * t, (s + 1) * t)
        for g in range(N_SSM_GROUPS):
            upd = _dot_tn(xw_scr[rs, g * gcols:(g + 1) * gcols], bcf[g][rs, :])
            for hh in range(heads_per_group):
                h = g * heads_per_group + hh
                scale = jnp.broadcast_to(exp_last[s * t:s * t + 1, h:h + 1], (SSM_HEAD_DIM, SSM_STATE))
                hout_ref[s, h] = h0_ref[s, h] * scale + upd[hh * SSM_HEAD_DIM:(hh + 1) * SSM_HEAD_DIM, :]


def _ssd_sample(big3, bc3, small3, small3_t, prefix, h0, conv_w, conv_b, dt_bias, a_log, d_skip, ssm_norm, *, gs):
    b, t, _ = big3.shape
    assert b % gs == 0 and t & (t - 1) == 0 and t == SUBLANES
    rows = gs * t
    body = functools.partial(_ssd_sample_body, gs=gs, t=t)
    gw = N_SSM_GROUPS * SSM_STATE
    row = lambda v: v.reshape(1, -1)
    colv = lambda v: v.reshape(-1, 1)
    const2 = lambda shape: pl.BlockSpec(shape, lambda gi: (0, 0))
    state_block = (gs, N_SSM_HEADS, SSM_HEAD_DIM, SSM_STATE)
    return pl.pallas_call(
        body,
        grid=(b // gs,),
        in_specs=[
            pl.BlockSpec((gs, t, SSM_WIDTH), lambda gi: (gi, 0, COL_X // SSM_WIDTH)),
            pl.BlockSpec((gs, t, gw), lambda gi: (gi, 0, 0)),
            pl.BlockSpec((gs, t, gw), lambda gi: (gi, 0, 1)),
            pl.BlockSpec((gs, t, SSM_WIDTH), lambda gi: (gi, 0, COL_Z // SSM_WIDTH)),
            pl.BlockSpec((gs, t, SMALL_WIDTH), lambda gi: (gi, 0, 0)),
            pl.BlockSpec((1, SMALL_WIDTH, rows), lambda gi: (gi, 0, 0)),
            pl.BlockSpec((gs, CONV_W - 1, CONV_DIM), lambda gi: (gi, 0, 0)),
            pl.BlockSpec(state_block, lambda gi: (gi, 0, 0, 0)),
            const2((CONV_W, CONV_DIM)),
            const2((1, CONV_DIM)),
            const2((1, N_SSM_HEADS)),
            const2((N_SSM_HEADS, 1)),
            const2((1, N_SSM_HEADS)),
            const2((N_SSM_HEADS, 1)),
            pl.BlockSpec(memory_space=pltpu.SMEM),
            const2((1, SSM_WIDTH)),
        ],
        out_specs=[
            pl.BlockSpec((gs, t, SSM_WIDTH), lambda gi: (gi, 0, 0)),
            pl.BlockSpec((gs, CONV_W - 1, CONV_DIM), lambda gi: (gi, 0, 0)),
            pl.BlockSpec(state_block, lambda gi: (gi, 0, 0, 0)),
        ],
        out_shape=[
            jax.ShapeDtypeStruct((b, t, SSM_WIDTH), BF16),
            jax.ShapeDtypeStruct((b, CONV_W - 1, CONV_DIM), F32),
            jax.ShapeDtypeStruct((b, N_SSM_HEADS, SSM_HEAD_DIM, SSM_STATE), F32),
        ],
        scratch_shapes=[
            pltpu.VMEM((gs, SUBLANES + t, CONV_DIM), F32),
            pltpu.VMEM((rows, CONV_DIM), F32),
            pltpu.VMEM((rows, SSM_WIDTH), F32),
            pltpu.VMEM((rows, SSM_WIDTH), F32),
            pltpu.VMEM((rows, SSM_WIDTH), F32),
        ],
        compiler_params=pltpu.CompilerParams(
            dimension_semantics=("parallel",), vmem_limit_bytes=VMEM_LIMIT),
        name="ssd_sample",
    )(big3, bc3, bc3, big3, small3, small3_t, prefix, h0,
      conv_w, row(conv_b), row(dt_bias), colv(dt_bias), row(a_log), colv(a_log), d_skip, row(ssm_norm))


def _ffn_body(x_ref, att_ref, y_ref, woa_ref, woy_ref, gf_ref, wg_ref, wu_ref, wd_ref, gl_ref,
              o_ref, f_scr):
    j = pl.program_id(1)

    @pl.when(j == 0)
    def _():
        h = x_ref[...] + _dot(att_ref[...], woa_ref[...]) + _dot(y_ref[...], woy_ref[...])
        o_ref[...] = h
        f_scr[...] = _rms(h, gf_ref[...]).astype(BF16)

    f = f_scr[...]
    act = _silu(_dot(f, wg_ref[...])) * _dot(f, wu_ref[...])
    o_ref[...] += _dot(act.astype(BF16), wd_ref[...])

    @pl.when(j == pl.num_programs(1) - 1)
    def _():
        o_ref[...] = _rms(o_ref[...], gl_ref[...])


def _out_ffn(x2, att2, y2, woa, woy, gf, wg, wu, wd, gl, *, tm, th):
    m, d = x2.shape
    hidden = wg.shape[1]
    resident = dict(pipeline_mode=pl.Buffered(1))
    return pl.pallas_call(
        _ffn_body,
        grid=(m // tm, hidden // th),
        in_specs=[
            pl.BlockSpec((tm, d), lambda i, j: (i, 0)),
            pl.BlockSpec((tm, ATT_WIDTH), lambda i, j: (i, 0)),
            pl.BlockSpec((tm, SSM_WIDTH), lambda i, j: (i, 0)),
            pl.BlockSpec((ATT_WIDTH, d), lambda i, j: (0, 0), **resident),
            pl.BlockSpec((SSM_WIDTH, d), lambda i, j: (0, 0), **resident),
            pl.BlockSpec((1, d), lambda i, j: (0, 0)),
            pl.BlockSpec((d, th), lambda i, j: (0, j)),
            pl.BlockSpec((d, th), lambda i, j: (0, j)),
            pl.BlockSpec((th, d), lambda i, j: (j, 0)),
            pl.BlockSpec((1, d), lambda i, j: (0, 0)),
        ],
        out_specs=pl.BlockSpec((tm, d), lambda i, j: (i, 0)),
        out_shape=jax.ShapeDtypeStruct((m, d), F32),
        scratch_shapes=[pltpu.VMEM((tm, d), BF16)],
        compiler_params=pltpu.CompilerParams(
            dimension_semantics=("parallel", "arbitrary"), vmem_limit_bytes=VMEM_LIMIT),
        name="out_ffn",
    )(x2, att2, y2, woa, woy, gf, wg, wu, wd, gl)


def _t5_bucket(rel):
    max_exact = N_BUCKETS // 2
    relf = jnp.maximum(rel, 1).astype(jnp.float32)
    large = max_exact + (jnp.log(relf / max_exact) / math.log(MAX_DISTANCE / max_exact)
                         * (N_BUCKETS - max_exact)).astype(jnp.int32)
    large = jnp.minimum(large, N_BUCKETS - 1)
    return jnp.where(rel < max_exact, rel, large)


def _toeplitz(u, rows, cols):
    nh, period = u.shape
    assert cols < period
    flat = jnp.tile(u, (1, rows))
    return flat[:, :rows * (period - 1)].reshape(nh, rows, period - 1)[:, :, :cols]


def _bias_tiles(rel_bias, rows, key_major):
    assert MAX_DISTANCE <= LANES and rows <= LANES
    table = rel_bias[_t5_bucket(jnp.arange(3 * LANES, dtype=jnp.int32))].T
    tiles = []
    for dd in (2, 1, 0):
        neg = table[:, (dd - 1) * LANES:dd * LANES] if dd >= 1 else jnp.tile(table[:, 0:1], (1, LANES))
        u = jnp.concatenate([table[:, dd * LANES:(dd + 1) * LANES], neg], axis=1)
        if key_major:
            tiles.append(_toeplitz(u, LANES, rows))
        else:
            w = jnp.concatenate([u[:, 0:1], jnp.flip(u[:, 1:], axis=1)], axis=1)
            tiles.append(_toeplitz(w, rows, LANES))
    return jnp.stack(tiles)


def _layer(x, att_call, ssd_call, blk_rows, p, *, kv_bf16):
    b, t, d = x.shape
    m = b * t
    tm = min(ROW_TILE, m)
    proj = _inproj(x.reshape(m, d), p["norm_attn"], p["w_t"], p["idx_k_norm"], tm=tm,
                   seq_len=t if kv_bf16 else None)
    big, bc, small, kin, k_rows, v_rows = proj[:6]
    big3 = big.reshape(b, t, BIG_WIDTH)
    bc3 = bc.reshape(b, t, BC_WIDTH)
    small3 = small.reshape(b, t, SMALL_WIDTH)
    kin3 = kin.reshape(b, t, IDX_DIM)
    hd = (b, t, N_ATT_HEADS, ATT_HEAD_DIM)
    k4 = k_rows.reshape(hd)
    v4 = v_rows.reshape(hd)
    small3_t = jnp.transpose(small.reshape(m // blk_rows, blk_rows, SMALL_WIDTH), (0, 2, 1))
    att = att_call(big3, small3, small3_t, kin3, k4, v4, *proj[6:])
    y, tail, h_fin = ssd_call(big3, bc3, small3, small3_t)
    out = _out_ffn(x.reshape(m, d), att.reshape(m, ATT_WIDTH), y.reshape(m, SSM_WIDTH), p["woa"], p["woy"],
                   p["norm_ffn"], p["wg"], p["wu"], p["wd"], p["norm_final"], tm=tm, th=FFN_TILE)
    return out.reshape(b, t, d), k4, v4, kin3, tail, h_fin


def kernel(x_prompt, x_sample, cache_k, cache_v, cache_kidx, state_conv, state_ssm, page_table, rel_bias,
           norm_attn, w_in, idx_k_norm, conv_w, conv_b, dt_bias, a_log, d_skip, ssm_norm, w_out, norm_ffn,
           w_gate, w_up, w_down, norm_final):
    depth = w_in.shape[0]
    assert depth == 1
    bp, s, d = x_prompt.shape
    bs, t, _ = x_sample.shape
    n_pages = page_table.shape[1]
    past = n_pages * PAGE_SIZE
    lyr = 0

    assert w_in.shape[2] == IN_WIDTH
    w_t = jnp.swapaxes(w_in[lyr], 0, 1).astype(BF16)
    row = lambda v: v.reshape(1, -1)
    p = dict(
        norm_attn=row(norm_attn[lyr]), w_t=w_t, idx_k_norm=row(idx_k_norm[lyr]),
        conv_w=conv_w[lyr], conv_b=conv_b[lyr], dt_bias=dt_bias[lyr], a_log=a_log[lyr], d_skip=d_skip[lyr],
        ssm_norm=ssm_norm[lyr],
        woa=w_out[lyr, :ATT_WIDTH].astype(BF16), woy=w_out[lyr, ATT_WIDTH:].astype(BF16),
        norm_ffn=row(norm_ffn[lyr]), wg=w_gate[lyr].astype(BF16), wu=w_up[lyr].astype(BF16),
        wd=w_down[lyr].astype(BF16), norm_final=row(norm_final),
    )

    topk_p = min(TOPK_MAX, s // 4)
    assert SSD_CHUNK == LANES and s % LANES == 0
    tiles_p = _bias_tiles(rel_bias, LANES, key_major=True)
    tiles_p = tiles_p - tiles_p[0:1]

    def att_prompt(big3, small3, small3_t, kin3, k4, v4, k16, vt16):
        return _attn_prompt(big3, small3_t, kin3, k16.reshape(bp, s, ATT_WIDTH), vt16, tiles_p,
                            topk=topk_p, group=CAUSAL_GROUP)

    zero_conv = jnp.zeros((bp, CONV_W - 1, CONV_DIM), F32)
    zero_ssm = jnp.zeros((bp, N_SSM_HEADS, SSM_HEAD_DIM, SSM_STATE), F32)
    ssm_params = (p["conv_w"], p["conv_b"], p["dt_bias"], p["a_log"], p["d_skip"], p["ssm_norm"])

    def ssd_prompt(big3, bc3, small3, small3_t):
        return _ssd(big3, bc3, small3, small3_t, zero_conv, zero_ssm, *ssm_params, l=SSD_CHUNK)

    yp, kp, vp, kip, cp, sp = _layer(x_prompt, att_prompt, ssd_prompt, SSD_CHUNK, p, kv_bf16=True)

    topk_s = min(TOPK_MAX, (past + t) // 4)
    tiles_s = jnp.repeat(_bias_tiles(rel_bias, t, key_major=False).reshape(3, N_ATT_HEADS * t, LANES),
                         N_ATT_HEADS, axis=-1)
    lane_head = jnp.arange(LANES * N_ATT_HEADS, dtype=jnp.int32) % N_ATT_HEADS
    row_head = jnp.arange(N_ATT_HEADS * t, dtype=jnp.int32) // t
    tiles_s = jnp.where(lane_head[None, :] == row_head[:, None], tiles_s, -jnp.inf)

    cache_kidx_t = jnp.swapaxes(cache_kidx, -1, -2)

    gs = math.gcd(bs, LANES // t)

    def att_sample(big3, small3, small3_t, kin3, k4, v4):
        sel = _select_sample(page_table, big3, small3, kin3, cache_kidx_t, topk=topk_s, gs=gs)
        return _attn_sample(page_table, big3, k4, v4, sel, tiles_s, cache_k, cache_v, gs=gs)

    def ssd_sample(big3, bc3, small3, small3_t):
        return _ssd_sample(big3, bc3, small3, small3_t, state_conv[lyr], state_ssm[lyr], *ssm_params, gs=gs)

    ys, ks, vs, kis, cs, ss = _layer(x_sample, att_sample, ssd_sample, gs * t, p, kv_bf16=False)

    st = lambda a: a[None]
    return (yp, ys, st(kp), st(vp), st(kip), st(cp), st(sp), st(ks), st(vs), st(kis), st(cs), st(ss))
```
